```python
import math
import jax, jax.numpy as jnp
from jax import lax
import numpy as np

D_MODEL = 1024
BATCH = 8
SEQ = 4096
DEPTH = 1

RNN_WIDTH = D_MODEL
RNN_BLOCKS = 8
RNN_BLOCK = RNN_WIDTH // RNN_BLOCKS
CONV_WIDTH = 4
LRU_C = 8.0
LRU_A_MIN, LRU_A_MAX = 0.9, 0.999
S5_WIDTH = D_MODEL // 2
S5_GROUP = 16
S5_GROUPS = S5_WIDTH // S5_GROUP
S5_STATE = 64
DT_MIN, DT_MAX = 1e-3, 1e-1
N_BRANCHES = 2
IN_COLS = 2 * RNN_WIDTH + S5_WIDTH + N_BRANCHES * D_MODEL
N_EXPERTS = 32
TOP_K = 4
D_FF = D_MODEL
SWIGLU_LIMIT = 7.0
SWIGLU_ALPHA = 1.702
ROUTE_BLOCK = 128
LN_EPS = 1e-5
DEEPNORM_ALPHA = (2.0 * DEPTH) ** 0.25
DEEPNORM_BETA = (8.0 * DEPTH) ** -0.25
N_MOD = 6

kernel_name = "hybrid_rglru_s5_moe_deepnorm_adaln"


def layer_norm(x, gain, bias):
    xf = x.astype(jnp.float32)
    mu = jnp.mean(xf, axis=-1, keepdims=True)
    var = jnp.mean(jnp.square(xf - mu), axis=-1, keepdims=True)
    y = (xf - mu) * lax.rsqrt(var + LN_EPS)
    return (y * gain.astype(jnp.float32) + bias.astype(jnp.float32)).astype(x.dtype)


def causal_depthwise_conv(x, w, b):
    C = x.shape[-1]
    y = lax.conv_general_dilated(
        x, w[:, None, :].astype(x.dtype), window_strides=(1,),
        padding=[(CONV_WIDTH - 1, 0)],
        dimension_numbers=("NWC", "WIO", "NWC"), feature_group_count=C)
    return y + b


def rg_lru(x, w_a, b_a, w_x, b_x, lam):
    Bn, S, C = x.shape
    xf = x.astype(jnp.float32)
    xb = xf.reshape(Bn, S, RNN_BLOCKS, RNN_BLOCK)
    r = jax.nn.sigmoid(jnp.einsum("bshi,hij->bshj", xb, w_a.astype(jnp.float32)).reshape(Bn, S, C) + b_a.astype(jnp.float32))
    i = jax.nn.sigmoid(jnp.einsum("bshi,hij->bshj", xb, w_x.astype(jnp.float32)).reshape(Bn, S, C) + b_x.astype(jnp.float32))
    log_a = -LRU_C * r * jax.nn.softplus(-lam.astype(jnp.float32))
    a = jnp.exp(log_a)
    u = jnp.sqrt(-jnp.expm1(2.0 * log_a)) * (i * xf)

    def step(h, au):
        a_t, u_t = au
        h = a_t * h + u_t
        return h, h

    h0 = jnp.zeros((Bn, C), jnp.float32)
    _, hs = lax.scan(step, h0, (jnp.swapaxes(a, 0, 1), jnp.swapaxes(u, 0, 1)))
    return jnp.swapaxes(hs, 0, 1).astype(x.dtype)


def s5_ssm(u, lam_re, lam_im, log_dt, b_re, b_im, c_re, c_im, d_skip):
    Bn, S, _ = u.shape
    f32 = jnp.float32
    ug = u.reshape(Bn, S, S5_GROUPS, S5_GROUP).astype(f32)
    lr, li = lam_re.astype(f32), lam_im.astype(f32)
    dt = jnp.exp(log_dt.astype(f32))[:, None]
    mag = jnp.exp(lr * dt)
    ab_re, ab_im = mag * jnp.cos(li * dt), mag * jnp.sin(li * dt)
    den = lr * lr + li * li
    q_re = ((ab_re - 1.0) * lr + ab_im * li) / den
    q_im = (ab_im * lr - (ab_re - 1.0) * li) / den
    br, bi = b_re.astype(f32), b_im.astype(f32)
    bb_re = q_re[..., None] * br - q_im[..., None] * bi
    bb_im = q_re[..., None] * bi + q_im[..., None] * br
    bu_re = jnp.einsum("bsgc,gpc->bsgp", ug, bb_re)
    bu_im = jnp.einsum("bsgc,gpc->bsgp", ug, bb_im)
    a_re = jnp.broadcast_to(ab_re, bu_re.shape)
    a_im = jnp.broadcast_to(ab_im, bu_im.shape)

    def combine(left, right):
        a1r, a1i, b1r, b1i = left
        a2r, a2i, b2r, b2i = right
        return (a2r * a1r - a2i * a1i,
                a2r * a1i + a2i * a1r,
                a2r * b1r - a2i * b1i + b2r,
                a2r * b1i + a2i * b1r + b2i)

    _, _, st_re, st_im = lax.associative_scan(combine, (a_re, a_im, bu_re, bu_im), axis=1)
    y = (jnp.einsum("bsgp,gcp->bsgc", st_re, c_re.astype(f32))
         - jnp.einsum("bsgp,gcp->bsgc", st_im, c_im.astype(f32))
         + d_skip.astype(f32) * ug)
    return y.reshape(Bn, S, S5_WIDTH).astype(u.dtype)


def mixer_sublayer(h, w_in, b_in, conv_w, conv_b, w_rg_a, b_rg_a, w_rg_x, b_rg_x, lru_lambda,
                   w_rnn_out, s5_lambda_re, s5_lambda_im, s5_log_dt, s5_b_re, s5_b_im,
                   s5_c_re, s5_c_im, s5_d, w_glu, w_out):
    proj = h @ w_in + b_in
    x_rnn, y_rnn, u_s5, g_logits = jnp.split(
        proj, [RNN_WIDTH, 2 * RNN_WIDTH, 2 * RNN_WIDTH + S5_WIDTH], axis=-1)
    xr = causal_depthwise_conv(x_rnn, conv_w, conv_b)
    hr = rg_lru(xr, w_rg_a, b_rg_a, w_rg_x, b_rg_x, lru_lambda)
    branch_a = (jax.nn.gelu(y_rnn) * hr) @ w_rnn_out
    ys = jax.nn.gelu(s5_ssm(u_s5, s5_lambda_re, s5_lambda_im, s5_log_dt, s5_b_re, s5_b_im,
                            s5_c_re, s5_c_im, s5_d))
    glu = ys @ w_glu
    branch_b = glu[..., :D_MODEL] * jax.nn.sigmoid(glu[..., D_MODEL:])
    gates = jax.nn.sigmoid(g_logits).reshape(*g_logits.shape[:-1], N_BRANCHES, D_MODEL)
    merged = gates[..., 0, :] * branch_a + gates[..., 1, :] * branch_b
    return merged @ w_out


def moe_sublayer(h, w_router, b_router, w_gu, b_gu, w_down, b_down):
    Bn, S, D = h.shape
    T = Bn * S
    xt = h.reshape(T, D)
    logits = (xt @ w_router + b_router).astype(jnp.float32)
    top_val, top_idx = lax.top_k(logits, TOP_K)
    probs = jax.nn.softmax(top_val, axis=-1)
    N = T * TOP_K
    flat_e = top_idx.reshape(N).astype(jnp.int32)
    flat_tok = jnp.repeat(jnp.arange(T, dtype=jnp.int32), TOP_K)
    flat_w = probs.reshape(N)
    order = jnp.argsort(flat_e)
    sorted_e = flat_e[order]
    counts = jnp.bincount(flat_e, length=N_EXPERTS).astype(jnp.int32)
    padded = (counts + ROUTE_BLOCK - 1) // ROUTE_BLOCK * ROUTE_BLOCK
    pad_end = jnp.cumsum(padded)
    pad_start = pad_end - padded
    start = jnp.cumsum(counts) - counts
    dest = pad_start[sorted_e] + jnp.arange(N, dtype=jnp.int32) - start[sorted_e]
    n_blocks = -(-(N + N_EXPERTS * (ROUTE_BLOCK - 1)) // ROUTE_BLOCK)
    n_rows = n_blocks * ROUTE_BLOCK
    row_tok = jnp.full((n_rows,), T, jnp.int32).at[dest].set(flat_tok[order])
    row_w = jnp.zeros((n_rows,), jnp.float32).at[dest].set(flat_w[order])
    block_e = jnp.minimum(
        jnp.searchsorted(pad_end, jnp.arange(n_blocks, dtype=jnp.int32) * ROUTE_BLOCK, side="right"),
        N_EXPERTS - 1)
    x_pad = jnp.concatenate([xt, jnp.zeros((1, D), xt.dtype)], axis=0)
    xb = x_pad[row_tok].reshape(n_blocks, ROUTE_BLOCK, D)

    def expert_block(args):
        xblk, e = args
        gu = xblk @ w_gu[e] + b_gu[e]
        gate = jnp.minimum(gu[:, :D_FF], SWIGLU_LIMIT)
        up = jnp.clip(gu[:, D_FF:], -SWIGLU_LIMIT, SWIGLU_LIMIT)
        act = gate * jax.nn.sigmoid(SWIGLU_ALPHA * gate) * (up + 1.0)
        return act @ w_down[e] + b_down[e]

    yb = lax.map(expert_block, (xb, block_e))
    y = jnp.zeros((T + 1, D), jnp.float32).at[row_tok].add(
        yb.reshape(n_rows, D).astype(jnp.float32) * row_w[:, None])
    return y[:T].reshape(Bn, S, D).astype(h.dtype)


def setup_inputs(seed: int = 0) -> dict:
    key = jax.random.key(seed)
    ks = iter(jax.random.split(key, 40))
    nrm = lambda shape, s: jax.random.normal(next(ks), shape, jnp.float32) * s
    L, D = DEPTH, D_MODEL
    a0 = jax.random.uniform(next(ks), (L, RNN_WIDTH), jnp.float32, LRU_A_MIN, LRU_A_MAX)
    return {
        "x": nrm((BATCH, SEQ, D), 1.0),
        "c": nrm((BATCH, D), 1.0),
        "w_ada": nrm((L, D, N_MOD * D), 0.2 * D ** -0.5),
        "b_ada": nrm((L, N_MOD * D), 0.01),
        "w_in": nrm((L, D, IN_COLS), D ** -0.5),
        "b_in": nrm((L, IN_COLS), 0.01),
        "conv_w": nrm((L, CONV_WIDTH, RNN_WIDTH), CONV_WIDTH ** -0.5),
        "conv_b": nrm((L, RNN_WIDTH), 0.01),
        "w_rg_a": nrm((L, RNN_BLOCKS, RNN_BLOCK, RNN_BLOCK), RNN_BLOCK ** -0.5),
        "b_rg_a": nrm((L, RNN_WIDTH), 0.01),
        "w_rg_x": nrm((L, RNN_BLOCKS, RNN_BLOCK, RNN_BLOCK), RNN_BLOCK ** -0.5),
        "b_rg_x": nrm((L, RNN_WIDTH), 0.01),
        "lru_lambda": jnp.log(a0) - jnp.log1p(-a0),
        "w_rnn_out": nrm((L, RNN_WIDTH, D), RNN_WIDTH ** -0.5),
        "s5_lambda_re": -0.5 + nrm((L, S5_GROUPS, S5_STATE), 0.01),
        "s5_lambda_im": jnp.pi * jnp.arange(S5_STATE, dtype=jnp.float32) + nrm((L, S5_GROUPS, S5_STATE), 0.01),
        "s5_log_dt": jax.random.uniform(next(ks), (L, S5_GROUPS), jnp.float32, math.log(DT_MIN), math.log(DT_MAX)),
        "s5_b_re": nrm((L, S5_GROUPS, S5_STATE, S5_GROUP), (2 * S5_GROUP) ** -0.5),
        "s5_b_im": nrm((L, S5_GROUPS, S5_STATE, S5_GROUP), (2 * S5_GROUP) ** -0.5),
        "s5_c_re": nrm((L, S5_GROUPS, S5_GROUP, S5_STATE), S5_STATE ** -0.5),
        "s5_c_im": nrm((L, S5_GROUPS, S5_GROUP, S5_STATE), S5_STATE ** -0.5),
        "s5_d": nrm((L, S5_GROUPS, S5_GROUP), 1.0),
        "w_glu": nrm((L, S5_WIDTH, 2 * D), S5_WIDTH ** -0.5),
        "w_out": nrm((L, D, D), DEEPNORM_BETA * D ** -0.5),
        "ln1_g": 1.0 + nrm((L, D), 0.01),
        "ln1_b": nrm((L, D), 0.01),
        "w_router": nrm((L, D, N_EXPERTS), D ** -0.5),
        "b_router": nrm((L, N_EXPERTS), 0.01),
        "w_gu": nrm((L, N_EXPERTS, D, 2 * D_FF), D ** -0.5),
        "b_gu": nrm((L, N_EXPERTS, 2 * D_FF), 0.01),
        "w_down": nrm((L, N_EXPERTS, D_FF, D), DEEPNORM_BETA * D_FF ** -0.5),
        "b_down": nrm((L, N_EXPERTS, D), 0.01),
        "ln2_g": 1.0 + nrm((L, D), 0.01),
        "ln2_b": nrm((L, D), 0.01),
    }


def reference(x, c, w_ada, b_ada, w_in, b_in, conv_w, conv_b, w_rg_a, b_rg_a, w_rg_x, b_rg_x,
              lru_lambda, w_rnn_out, s5_lambda_re, s5_lambda_im, s5_log_dt, s5_b_re, s5_b_im,
              s5_c_re, s5_c_im, s5_d, w_glu, w_out, ln1_g, ln1_b, w_router, b_router,
              w_gu, b_gu, w_down, b_down, ln2_g, ln2_b):
    c_act = jax.nn.silu(c)
    for l in range(DEPTH):
        mod = (c_act @ w_ada[l] + b_ada[l])[:, None, :]
        sh1, sc1, g1, sh2, sc2, g2 = jnp.split(mod, N_MOD, axis=-1)
        h = x * (1.0 + sc1) + sh1
        mix = mixer_sublayer(h, w_in[l], b_in[l], conv_w[l], conv_b[l], w_rg_a[l], b_rg_a[l],
                             w_rg_x[l], b_rg_x[l], lru_lambda[l], w_rnn_out[l],
                             s5_lambda_re[l], s5_lambda_im[l], s5_log_dt[l], s5_b_re[l], s5_b_im[l],
                             s5_c_re[l], s5_c_im[l], s5_d[l], w_glu[l], w_out[l])
        x = layer_norm(DEEPNORM_ALPHA * x + (1.0 + g1) * mix, ln1_g[l], ln1_b[l])
        h = x * (1.0 + sc2) + sh2
        ffn = moe_sublayer(h, w_router[l], b_router[l], w_gu[l], b_gu[l], w_down[l], b_down[l])
        x = layer_norm(DEEPNORM_ALPHA * x + (1.0 + g2) * ffn, ln2_g[l], ln2_b[l])
    return x
```

```python
import functools
import math

import jax
import jax.numpy as jnp
from jax import lax
from jax.experimental import pallas as pl
from jax.experimental.pallas import tpu as pltpu

V7X_SUBLANES = 8
V7X_LANES = 128
V7X_VMEM_BYTES = 64 * 1024 * 1024

CONV_WIDTH = 4
LRU_C = 8.0
S5_GROUP = 16
S5_STATE = 64
TOP_K = 4
SWIGLU_LIMIT = 7.0
SWIGLU_ALPHA = 1.702
LN_EPS = 1e-5

S5_BLOCK_GROUPS = V7X_LANES // S5_GROUP
S5_BLOCK_STATES = S5_BLOCK_GROUPS * S5_STATE

MIXER_STEPS = 32
EXPERT_ROWS = 256
COMBINE_ROWS = 512
SCAN_UNROLL = 4

_BF16 = jnp.bfloat16
_F32 = jnp.float32


def _dot(a, b):
    return jnp.dot(a, b, preferred_element_type=_F32)


def _vmem_limit(nbytes):
    return int(min(nbytes, V7X_VMEM_BYTES - 4 * 1024 * 1024))


def _layer_norm(z, gain, bias):
    mu = jnp.mean(z, axis=-1, keepdims=True)
    zc = z - mu
    var = jnp.mean(zc * zc, axis=-1, keepdims=True)
    return zc * lax.rsqrt(var + LN_EPS) * gain + bias


def _rows(v, steps):
    return jnp.tile(v, (steps, 1))


def _ada_kernel(c_ref, w_ref, b_ref, o_ref):
    c = c_ref[...]
    c_act = (c * jax.nn.sigmoid(c)).astype(_BF16)
    o_ref[...] = _dot(c_act, w_ref[...].astype(_BF16)) + b_ref[...]


def _ada(c, w_ada, b_ada):
    batch, d = c.shape
    n_out = w_ada.shape[1]
    return pl.pallas_call(
        _ada_kernel,
        grid=(n_out // d,),
        in_specs=[
            pl.BlockSpec((batch, d), lambda j: (0, 0)),
            pl.BlockSpec((d, d), lambda j: (0, j)),
            pl.BlockSpec((1, d), lambda j: (0, j)),
        ],
        out_specs=pl.BlockSpec((batch, d), lambda j: (0, j)),
        out_shape=jax.ShapeDtypeStruct((batch, n_out), _F32),
        name="ada",
    )(c, w_ada, b_ada.reshape(1, n_out))


def _mixer_kernel(alpha, steps, batch, d, n_s5_blocks, n_experts,
                  x_ref, mod_ref, w_in_ref, b_in_ref, conv_w_ref, conv_b_ref, wg_ref, bg_ref,
                  lamc_ref, w_rnn_ref, s5ar_ref, s5ai_ref, s5b_ref, s5c_ref, s5d_ref,
                  w_glu_ref, w_out_ref, ln_g_ref, ln_b_ref, w_r_ref, b_r_ref,
                  x1_ref, h2_ref, idx_ref, prob_ref, rank_ref, cnt_ref,
                  xc_s, a_s, u_s, bu_s, h_state, s5_state, cnt_s):
    m = steps * batch
    halo = (CONV_WIDTH - 1) * batch
    s5w = n_s5_blocks * V7X_LANES
    n_blk = d // V7X_LANES
    bs = S5_BLOCK_STATES

    @pl.when(pl.program_id(0) == 0)
    def _():
        xc_s[0:halo, :] = jnp.zeros((halo, d), _F32)
        h_state[...] = jnp.zeros_like(h_state)
        s5_state[...] = jnp.zeros_like(s5_state)
        cnt_s[...] = jnp.zeros_like(cnt_s)

    def mod(k):
        return mod_ref[:, k * d:(k + 1) * d]

    x = x_ref[...]
    h = x * _rows(1.0 + mod(1), steps) + _rows(mod(0), steps)
    hb = h.astype(_BF16)

    xc_s[halo:halo + m, :] = _dot(hb, w_in_ref[:, 0:d]) + b_in_ref[:, 0:d]
    xr = jnp.zeros((m, d), _F32) + conv_b_ref[...]
    for k in range(CONV_WIDTH):
        xr = xr + conv_w_ref[k:k + 1, :] * xc_s[k * batch:k * batch + m, :]
    xc_s[0:halo, :] = xc_s[m:m + halo, :]
    xrb = xr.astype(_BF16)
    gates = [_dot(xrb[:, j * V7X_LANES:(j + 1) * V7X_LANES], wg_ref[j]) for j in range(n_blk)]
    r_gate = jax.nn.sigmoid(jnp.concatenate([g[:, :V7X_LANES] for g in gates], axis=1) + bg_ref[:, 0:d])
    i_gate = jax.nn.sigmoid(jnp.concatenate([g[:, V7X_LANES:] for g in gates], axis=1) + bg_ref[:, d:2 * d])
    a = jnp.exp(lamc_ref[...] * r_gate)
    a_s[...] = a
    u_s[...] = jnp.sqrt(1.0 - a * a) * (i_gate * xr)

    def lru_step(t, hc):
        r0 = pl.multiple_of(t * batch, batch)
        hn = a_s[pl.ds(r0, batch), :] * hc + u_s[pl.ds(r0, batch), :]
        u_s[pl.ds(r0, batch), :] = hn
        return hn

    h_state[...] = lax.fori_loop(0, steps, lru_step, h_state[...], unroll=SCAN_UNROLL)
    y_rnn = _dot(hb, w_in_ref[:, d:2 * d]) + b_in_ref[:, d:2 * d]
    branch_a = _dot((jax.nn.gelu(y_rnn) * u_s[...]).astype(_BF16), w_rnn_ref[...])

    c0 = 2 * d
    u5 = _dot(hb, w_in_ref[:, c0:c0 + s5w]) + b_in_ref[:, c0:c0 + s5w]
    u5b = u5.astype(_BF16)
    for j in range(n_s5_blocks):
        bu_s[:, 2 * bs * j:2 * bs * (j + 1)] = _dot(u5b[:, j * V7X_LANES:(j + 1) * V7X_LANES], s5b_ref[j])

    for j in range(n_s5_blocks):
        re0, im0 = 2 * bs * j, 2 * bs * j + bs
        ar = s5ar_ref[j]
        ai = s5ai_ref[j]

        def s5_step(t, carry, re0=re0, im0=im0, ar=ar, ai=ai):
            re, im = carry
            r0 = pl.multiple_of(t * batch, batch)
            nre = ar * re - ai * im + bu_s[pl.ds(r0, batch), re0:re0 + bs]
            nim = ar * im + ai * re + bu_s[pl.ds(r0, batch), im0:im0 + bs]
            bu_s[pl.ds(r0, batch), re0:re0 + bs] = nre
            bu_s[pl.ds(r0, batch), im0:im0 + bs] = nim
            return nre, nim

        re_f, im_f = lax.fori_loop(0, steps, s5_step,
                                   (s5_state[:, re0:re0 + bs], s5_state[:, im0:im0 + bs]),
                                   unroll=SCAN_UNROLL)
        s5_state[:, re0:re0 + bs] = re_f
        s5_state[:, im0:im0 + bs] = im_f

    y5 = jnp.concatenate(
        [_dot(bu_s[:, 2 * bs * j:2 * bs * (j + 1)].astype(_BF16), s5c_ref[j]) for j in range(n_s5_blocks)],
        axis=1) + s5d_ref[...] * u5
    glu = _dot(jax.nn.gelu(y5).astype(_BF16), w_glu_ref[...])
    branch_b = glu[:, :d] * jax.nn.sigmoid(glu[:, d:])

    c1 = c0 + s5w
    g_a = jax.nn.sigmoid(_dot(hb, w_in_ref[:, c1:c1 + d]) + b_in_ref[:, c1:c1 + d])
    g_b = jax.nn.sigmoid(_dot(hb, w_in_ref[:, c1 + d:c1 + 2 * d]) + b_in_ref[:, c1 + d:c1 + 2 * d])
    mix = _dot((g_a * branch_a + g_b * branch_b).astype(_BF16), w_out_ref[...])
    x1 = _layer_norm(alpha * x + _rows(1.0 + mod(2), steps) * mix, ln_g_ref[...], ln_b_ref[...])
    x1_ref[...] = x1

    h2b = (x1 * _rows(1.0 + mod(4), steps) + _rows(mod(3), steps)).astype(_BF16)
    h2_ref[...] = h2b
    lane = lax.broadcasted_iota(jnp.int32, (m, V7X_LANES), 1)
    lane_f = lane.astype(_F32)
    neg_inf = jnp.float32(-jnp.inf)
    logits = jnp.where(lane < n_experts, _dot(h2b, w_r_ref[...]) + b_r_ref[...], neg_inf)
    onehot = jnp.zeros((m, V7X_LANES), _F32)
    picks, vals = [], []
    for _ in range(TOP_K):
        v = jnp.max(logits, axis=-1, keepdims=True)
        p = jnp.min(jnp.where(logits == v, lane_f, float(V7X_LANES)), axis=-1, keepdims=True)
        hit = lane_f == p
        onehot = jnp.where(hit, 1.0, onehot)
        logits = jnp.where(hit, neg_inf, logits)
        picks.append(p)
        vals.append(v)
    exps = [jnp.exp(v - vals[0]) for v in vals]
    inv_den = 1.0 / functools.reduce(lambda s, e: s + e, exps)
    row = lax.broadcasted_iota(jnp.int32, (m, m), 0)
    col = lax.broadcasted_iota(jnp.int32, (m, m), 1)
    earlier = jnp.where(col < row, 1.0, 0.0).astype(_BF16)
    before = _dot(earlier, onehot.astype(_BF16)) + cnt_s[0:1, :]
    idx_out = jnp.zeros((m, V7X_LANES), _F32)
    prob_out = jnp.zeros((m, V7X_LANES), _F32)
    rank_out = jnp.zeros((m, V7X_LANES), _F32)
    for k in range(TOP_K):
        rank_k = jnp.sum(jnp.where(lane_f == picks[k], before, 0.0), axis=-1, keepdims=True)
        idx_out = jnp.where(lane == k, picks[k], idx_out)
        prob_out = jnp.where(lane == k, exps[k] * inv_den, prob_out)
        rank_out = jnp.where(lane == k, rank_k, rank_out)
    idx_ref[...] = idx_out.astype(jnp.int32)
    prob_ref[...] = prob_out
    rank_ref[...] = rank_out.astype(jnp.int32)
    cnt_new = cnt_s[...] + jnp.sum(onehot, axis=0, keepdims=True)
    cnt_s[...] = cnt_new
    cnt_ref[...] = cnt_new


def _mixer(xt, mod, p, *, alpha, batch, n_experts):
    tokens, d = xt.shape
    steps = MIXER_STEPS
    m = steps * batch
    n_s5_blocks = p["s5b"].shape[0]
    s5_lanes = n_s5_blocks * 2 * S5_BLOCK_STATES
    halo = (CONV_WIDTH - 1) * batch

    def const(a):
        nd = a.ndim
        return pl.BlockSpec(a.shape, lambda i, nd=nd: (0,) * nd, pipeline_mode=pl.Buffered(1))

    weights = [p["w_in"], p["b_in"], p["conv_w"], p["conv_b"], p["wg"], p["bg"], p["lamc"], p["w_rnn"],
               p["s5ar"], p["s5ai"], p["s5b"], p["s5c"], p["s5d"], p["w_glu"], p["w_out"],
               p["ln1_g"], p["ln1_b"], p["w_r"], p["b_r"]]
    row_spec = lambda width: pl.BlockSpec((m, width), lambda i: (i, 0))
    out_shape = (
        jax.ShapeDtypeStruct((tokens, d), _F32),
        jax.ShapeDtypeStruct((tokens, d), _BF16),
        jax.ShapeDtypeStruct((tokens, V7X_LANES), jnp.int32),
        jax.ShapeDtypeStruct((tokens, V7X_LANES), _F32),
        jax.ShapeDtypeStruct((tokens, V7X_LANES), jnp.int32),
        jax.ShapeDtypeStruct((V7X_SUBLANES, V7X_LANES), _F32),
    )
    scratch = [
        pltpu.VMEM((m + halo, d), _F32),
        pltpu.VMEM((m, d), _F32),
        pltpu.VMEM((m, d), _F32),
        pltpu.VMEM((m, s5_lanes), _F32),
        pltpu.VMEM((batch, d), _F32),
        pltpu.VMEM((batch, s5_lanes), _F32),
        pltpu.VMEM((V7X_SUBLANES, V7X_LANES), _F32),
    ]
    weight_bytes = sum(w.size * w.dtype.itemsize for w in weights)
    act_bytes = m * d * 4
    vmem = weight_bytes + 2 * (2 * act_bytes + act_bytes // 2) + (3 + 4) * act_bytes + 16 * act_bytes
    kern = functools.partial(_mixer_kernel, alpha, steps, batch, d, n_s5_blocks, n_experts)
    return pl.pallas_call(
        kern,
        grid=(tokens // m,),
        in_specs=[row_spec(d), const(mod)] + [const(w) for w in weights],
        out_specs=(row_spec(d), row_spec(d), row_spec(V7X_LANES), row_spec(V7X_LANES), row_spec(V7X_LANES),
                   pl.BlockSpec((V7X_SUBLANES, V7X_LANES), lambda i: (0, 0))),
        out_shape=out_shape,
        scratch_shapes=scratch,
        compiler_params=pltpu.CompilerParams(dimension_semantics=("arbitrary",),
                                             vmem_limit_bytes=_vmem_limit(vmem)),
        name="mixer",
    )(xt, mod, *weights)


def _expert_kernel(d_ff, tile_e_ref, n_used_ref, x_ref, wgu_ref, bgu_ref, wd_ref, bd_ref, y_ref):
    @pl.when(pl.program_id(0) < n_used_ref[0])
    def _():
        gu = _dot(x_ref[...], wgu_ref[0]) + bgu_ref[0]
        gate = jnp.minimum(gu[:, :d_ff], SWIGLU_LIMIT)
        up = jnp.clip(gu[:, d_ff:], -SWIGLU_LIMIT, SWIGLU_LIMIT)
        act = gate * jax.nn.sigmoid(SWIGLU_ALPHA * gate) * (up + 1.0)
        y_ref[...] = (_dot(act.astype(_BF16), wd_ref[0]) + bd_ref[0]).astype(y_ref.dtype)

    @pl.when(pl.program_id(0) >= n_used_ref[0])
    def _():
        y_ref[...] = jnp.zeros_like(y_ref)


def _experts(xb, tile_e, n_used, w_gu, b_gu, w_down, b_down):
    n_rows, d = xb.shape
    n_experts, _, two_ff = w_gu.shape
    d_ff = two_ff // 2
    r = EXPERT_ROWS
    vmem = 2 * (d * two_ff * 2 + d_ff * d * 2) + 4 * r * d * 2 + 6 * r * two_ff * 4
    grid_spec = pltpu.PrefetchScalarGridSpec(
        num_scalar_prefetch=2,
        grid=(n_rows // r,),
        in_specs=[
            pl.BlockSpec((r, d), lambda i, te, nu: (i, 0)),
            pl.BlockSpec((1, d, two_ff), lambda i, te, nu: (te[i], 0, 0)),
            pl.BlockSpec((1, 1, two_ff), lambda i, te, nu: (te[i], 0, 0)),
            pl.BlockSpec((1, d_ff, d), lambda i, te, nu: (te[i], 0, 0)),
            pl.BlockSpec((1, 1, d), lambda i, te, nu: (te[i], 0, 0)),
        ],
        out_specs=pl.BlockSpec((r, d), lambda i, te, nu: (i, 0)),
    )
    return pl.pallas_call(
        functools.partial(_expert_kernel, d_ff),
        grid_spec=grid_spec,
        out_shape=jax.ShapeDtypeStruct((n_rows, d), _BF16),
        compiler_params=pltpu.CompilerParams(dimension_semantics=("arbitrary",),
                                             vmem_limit_bytes=_vmem_limit(vmem)),
        name="experts",
    )(tile_e, n_used, xb, w_gu, b_gu.reshape(n_experts, 1, two_ff), w_down, b_down.reshape(n_experts, 1, d))


def _combine_kernel(alpha, steps, x1_ref, yg_ref, prob_ref, mod_ref, ln_g_ref, ln_b_ref, o_ref):
    d = x1_ref.shape[1]
    ffn = jnp.zeros(x1_ref.shape, _F32)
    for k in range(TOP_K):
        ffn = ffn + prob_ref[:, k:k + 1] * yg_ref[k].astype(_F32)
    gate = _rows(1.0 + mod_ref[:, 5 * d:6 * d], steps)
    o_ref[...] = _layer_norm(alpha * x1_ref[...] + gate * ffn, ln_g_ref[...], ln_b_ref[...])


def _combine(x1, yg, prob, mod, ln_g, ln_b, *, alpha, batch):
    tokens, d = x1.shape
    rows = COMBINE_ROWS
    const = lambda a: pl.BlockSpec(a.shape, lambda i: (0, 0))
    return pl.pallas_call(
        functools.partial(_combine_kernel, alpha, rows // batch),
        grid=(tokens // rows,),
        in_specs=[
            pl.BlockSpec((rows, d), lambda i: (i, 0)),
            pl.BlockSpec((TOP_K, rows, d), lambda i: (0, i, 0)),
            pl.BlockSpec((rows, V7X_LANES), lambda i: (i, 0)),
            const(mod), const(ln_g), const(ln_b),
        ],
        out_specs=pl.BlockSpec((rows, d), lambda i: (i, 0)),
        out_shape=jax.ShapeDtypeStruct((tokens, d), _F32),
        compiler_params=pltpu.CompilerParams(dimension_semantics=("parallel",)),
        name="combine",
    )(x1, yg, prob, mod, ln_g, ln_b)


def _block_diag(blocks):
    n, a, b = blocks.shape
    eye = jnp.eye(n, dtype=blocks.dtype)
    return (eye[:, None, :, None] * blocks[:, :, None, :]).reshape(n * a, n * b)


def _s5_params(lam_re, lam_im, log_dt, b_re, b_im, c_re, c_im, d_skip):
    groups = lam_re.shape[0]
    nb = groups // S5_BLOCK_GROUPS
    dt = jnp.exp(log_dt)[:, None]
    mag = jnp.exp(lam_re * dt)
    ab_re, ab_im = mag * jnp.cos(lam_im * dt), mag * jnp.sin(lam_im * dt)
    den = lam_re * lam_re + lam_im * lam_im
    q_re = ((ab_re - 1.0) * lam_re + ab_im * lam_im) / den
    q_im = (ab_im * lam_re - (ab_re - 1.0) * lam_im) / den
    bb_re = q_re[..., None] * b_re - q_im[..., None] * b_im
    bb_im = q_re[..., None] * b_im + q_im[..., None] * b_re

    def per_block(a):
        return a.reshape(nb, S5_BLOCK_GROUPS, *a.shape[1:])

    bmat = jnp.stack([
        jnp.concatenate([_block_diag(jnp.swapaxes(per_block(bb_re)[j], 1, 2)),
                         _block_diag(jnp.swapaxes(per_block(bb_im)[j], 1, 2))], axis=1)
        for j in range(nb)])
    cmat = jnp.stack([
        jnp.concatenate([_block_diag(jnp.swapaxes(per_block(c_re)[j], 1, 2)),
                         -_block_diag(jnp.swapaxes(per_block(c_im)[j], 1, 2))], axis=0)
        for j in range(nb)])
    tile = lambda a: jnp.broadcast_to(a.reshape(nb, 1, S5_BLOCK_STATES), (nb, V7X_SUBLANES, S5_BLOCK_STATES))
    return dict(s5ar=tile(ab_re), s5ai=tile(ab_im), s5b=bmat.astype(_BF16), s5c=cmat.astype(_BF16),
                s5d=d_skip.reshape(1, -1))


def kernel(x, c, w_ada, b_ada, w_in, b_in, conv_w, conv_b, w_rg_a, b_rg_a, w_rg_x, b_rg_x, lru_lambda, w_rnn_out, s5_lambda_re, s5_lambda_im, s5_log_dt, s5_b_re, s5_b_im, s5_c_re, s5_c_im, s5_d, w_glu, w_out, ln1_g, ln1_b, w_router, b_router, w_gu, b_gu, w_down, b_down, ln2_g, ln2_b):
    batch, seq, d = x.shape
    depth = w_ada.shape[0]
    n_experts = w_router.shape[-1]
    tokens = batch * seq
    alpha = (2.0 * depth) ** 0.25
    assert batch == V7X_SUBLANES and d % V7X_LANES == 0 and n_experts <= V7X_LANES
    assert seq % MIXER_STEPS == 0 and tokens % COMBINE_ROWS == 0

    xt = jnp.swapaxes(x, 0, 1).reshape(tokens, d)
    c_cur = c
    for l in range(depth):
        mod = _ada(c_cur, w_ada[l], b_ada[l])
        row = lambda v: v.reshape(1, -1)
        p = dict(
            w_in=w_in[l].astype(_BF16), b_in=row(b_in[l]), conv_w=conv_w[l], conv_b=row(conv_b[l]),
            wg=jnp.concatenate([w_rg_a[l], w_rg_x[l]], axis=-1).astype(_BF16),
            bg=row(jnp.concatenate([b_rg_a[l], b_rg_x[l]])),
            lamc=row(-LRU_C * jax.nn.softplus(-lru_lambda[l])),
            w_rnn=w_rnn_out[l].astype(_BF16), w_glu=w_glu[l].astype(_BF16), w_out=w_out[l].astype(_BF16),
            ln1_g=row(ln1_g[l]), ln1_b=row(ln1_b[l]),
            w_r=jnp.pad(w_router[l], ((0, 0), (0, V7X_LANES - n_experts))).astype(_BF16),
            b_r=row(jnp.pad(b_router[l], (0, V7X_LANES - n_experts))),
            **_s5_params(s5_lambda_re[l], s5_lambda_im[l], s5_log_dt[l], s5_b_re[l], s5_b_im[l],
                         s5_c_re[l], s5_c_im[l], s5_d[l]),
        )
        x1, h2, idx, prob, rank, cnt = _mixer(xt, mod, p, alpha=alpha, batch=batch, n_experts=n_experts)

        r = EXPERT_ROWS
        n_tiles = -(-(tokens * TOP_K + n_experts * (r - 1)) // r)
        counts = cnt[0, :n_experts].astype(jnp.int32)
        padded = (counts + r - 1) // r * r
        pad_end = jnp.cumsum(padded)
        pad_start = pad_end - padded
        idx4, rank4 = idx[:, :TOP_K], rank[:, :TOP_K]
        dest = pad_start[idx4] + rank4
        tok = jnp.broadcast_to(jnp.arange(tokens, dtype=jnp.int32)[:, None], dest.shape)
        row_tok = jnp.zeros((n_tiles * r,), jnp.int32).at[dest.reshape(-1)].set(tok.reshape(-1))
        tile_e = jnp.minimum(jnp.searchsorted(pad_end, jnp.arange(n_tiles, dtype=jnp.int32) * r, side="right"),
                             n_experts - 1).astype(jnp.int32)
        n_used = (pad_end[-1:] // r).astype(jnp.int32)

        xb = jnp.take(h2, row_tok, axis=0)
        yb = _experts(xb, tile_e, n_used, w_gu[l].astype(_BF16), b_gu[l], w_down[l].astype(_BF16), b_down[l])
        yg = jnp.take(yb, dest.T, axis=0)
        xt = _combine(x1, yg, prob, mod, ln2_g[l].reshape(1, -1), ln2_b[l].reshape(1, -1),
                      alpha=alpha, batch=batch)
    return jnp.swapaxes(xt.reshape(seq, batch, d), 0, 1)
```

```python
import functools
import math

import jax
import jax.numpy as jnp
from jax import lax
from jax.experimental import pallas as pl
from jax.experimental.pallas import tpu as pltpu
from jax.experimental.pallas import tpu_sc as plsc

V7X_SUBLANES = 8
V7X_LANES = 128
V7X_VMEM_BYTES = 64 * 1024 * 1024

CONV_WIDTH = 4
LRU_C = 8.0
S5_GROUP = 16
S5_STATE = 64
TOP_K = 4
SWIGLU_LIMIT = 7.0
SWIGLU_ALPHA = 1.702
LN_EPS = 1e-5

S5_BLOCK_GROUPS = V7X_LANES // S5_GROUP
S5_BLOCK_STATES = S5_BLOCK_GROUPS * S5_STATE

MIXER_STEPS = 32
EXPERT_ROWS = 256
COMBINE_ROWS = 512
WEIGHT_CAST_ROWS = 64
SCAN_UNROLL = 4

_BF16 = jnp.bfloat16
_F32 = jnp.float32


def _dot(a, b):
    return jnp.dot(a, b, preferred_element_type=_F32)


def _vmem_limit(nbytes):
    return int(min(nbytes, V7X_VMEM_BYTES - 4 * 1024 * 1024))


def _layer_norm(z, gain, bias):
    mu = jnp.mean(z, axis=-1, keepdims=True)
    zc = z - mu
    var = jnp.mean(zc * zc, axis=-1, keepdims=True)
    return zc * lax.rsqrt(var + LN_EPS) * gain + bias


def _rows(v, steps):
    return jnp.tile(v, (steps, 1))


_HI_MASK = 0xFFFF0000


def _pack_rows(v):
    half = v.shape[1] // 2
    bits = lax.bitcast_convert_type(v.astype(_BF16).astype(_F32), jnp.uint32)
    packed = (bits[:, :half] >> 16) | (bits[:, half:] & jnp.uint32(_HI_MASK))
    return lax.bitcast_convert_type(packed, jnp.int32)


def _unpack_rows(w):
    bits = lax.bitcast_convert_type(w, jnp.uint32)
    lo = lax.bitcast_convert_type(bits << 16, _F32)
    hi = lax.bitcast_convert_type(bits & jnp.uint32(_HI_MASK), _F32)
    return lo, hi


def _ada_kernel(c_ref, w_ref, b_ref, o_ref):
    c = c_ref[...]
    c_act = (c * jax.nn.sigmoid(c)).astype(_BF16)
    o_ref[...] = _dot(c_act, w_ref[...].astype(_BF16)) + b_ref[...]


def _ada(c, w_ada, b_ada):
    batch, d = c.shape
    n_out = w_ada.shape[1]
    return pl.pallas_call(
        _ada_kernel,
        grid=(n_out // d,),
        in_specs=[
            pl.BlockSpec((batch, d), lambda j: (0, 0)),
            pl.BlockSpec((d, d), lambda j: (0, j)),
            pl.BlockSpec((1, d), lambda j: (0, j)),
        ],
        out_specs=pl.BlockSpec((batch, d), lambda j: (0, j)),
        out_shape=jax.ShapeDtypeStruct((batch, n_out), _F32),
        name="ada",
    )(c, w_ada, b_ada.reshape(1, n_out))


def _mixer_kernel(alpha, steps, batch, d, n_s5_blocks, n_experts, region_rows,
                  x_ref, mod_ref, w_in_ref, b_in_ref, conv_w_ref, conv_b_ref, wg_ref, bg_ref,
                  lamc_ref, w_rnn_ref, s5ar_ref, s5ai_ref, s5b_ref, s5c_ref, s5d_ref,
                  w_glu_ref, w_out_ref, ln_g_ref, ln_b_ref, w_r_ref, b_r_ref,
                  x1_ref, h2_ref, dest_ref, prob_ref, cnt_ref,
                  xc_s, a_s, u_s, bu_s, h_state, s5_state, cnt_s):
    m = steps * batch
    halo = (CONV_WIDTH - 1) * batch
    s5w = n_s5_blocks * V7X_LANES
    n_blk = d // V7X_LANES
    bs = S5_BLOCK_STATES

    @pl.when(pl.program_id(0) == 0)
    def _():
        xc_s[0:halo, :] = jnp.zeros((halo, d), _F32)
        h_state[...] = jnp.zeros_like(h_state)
        s5_state[...] = jnp.zeros_like(s5_state)
        cnt_s[...] = jnp.zeros_like(cnt_s)

    def mod(k):
        return mod_ref[:, k * d:(k + 1) * d]

    x = x_ref[...]
    h = x * _rows(1.0 + mod(1), steps) + _rows(mod(0), steps)
    hb = h.astype(_BF16)

    xc_s[halo:halo + m, :] = _dot(hb, w_in_ref[:, 0:d]) + b_in_ref[:, 0:d]
    xr = jnp.zeros((m, d), _F32) + conv_b_ref[...]
    for k in range(CONV_WIDTH):
        xr = xr + conv_w_ref[k:k + 1, :] * xc_s[k * batch:k * batch + m, :]
    xc_s[0:halo, :] = xc_s[m:m + halo, :]
    xrb = xr.astype(_BF16)
    gates = [_dot(xrb[:, j * V7X_LANES:(j + 1) * V7X_LANES], wg_ref[j]) for j in range(n_blk)]
    r_gate = jax.nn.sigmoid(jnp.concatenate([g[:, :V7X_LANES] for g in gates], axis=1) + bg_ref[:, 0:d])
    i_gate = jax.nn.sigmoid(jnp.concatenate([g[:, V7X_LANES:] for g in gates], axis=1) + bg_ref[:, d:2 * d])
    a = jnp.exp(lamc_ref[...] * r_gate)
    a_s[...] = a
    u_s[...] = jnp.sqrt(1.0 - a * a) * (i_gate * xr)

    def lru_step(t, hc):
        r0 = pl.multiple_of(t * batch, batch)
        hn = a_s[pl.ds(r0, batch), :] * hc + u_s[pl.ds(r0, batch), :]
        u_s[pl.ds(r0, batch), :] = hn
        return hn

    h_state[...] = lax.fori_loop(0, steps, lru_step, h_state[...], unroll=SCAN_UNROLL)
    y_rnn = _dot(hb, w_in_ref[:, d:2 * d]) + b_in_ref[:, d:2 * d]
    branch_a = _dot((jax.nn.gelu(y_rnn) * u_s[...]).astype(_BF16), w_rnn_ref[...])

    c0 = 2 * d
    u5 = _dot(hb, w_in_ref[:, c0:c0 + s5w]) + b_in_ref[:, c0:c0 + s5w]
    u5b = u5.astype(_BF16)
    for j in range(n_s5_blocks):
        bu_s[:, 2 * bs * j:2 * bs * (j + 1)] = _dot(u5b[:, j * V7X_LANES:(j + 1) * V7X_LANES], s5b_ref[j])

    for j in range(n_s5_blocks):
        re0, im0 = 2 * bs * j, 2 * bs * j + bs
        ar = s5ar_ref[j]
        ai = s5ai_ref[j]

        def s5_step(t, carry, re0=re0, im0=im0, ar=ar, ai=ai):
            re, im = carry
            r0 = pl.multiple_of(t * batch, batch)
            nre = ar * re - ai * im + bu_s[pl.ds(r0, batch), re0:re0 + bs]
            nim = ar * im + ai * re + bu_s[pl.ds(r0, batch), im0:im0 + bs]
            bu_s[pl.ds(r0, batch), re0:re0 + bs] = nre
            bu_s[pl.ds(r0, batch), im0:im0 + bs] = nim
            return nre, nim

        re_f, im_f = lax.fori_loop(0, steps, s5_step,
                                   (s5_state[:, re0:re0 + bs], s5_state[:, im0:im0 + bs]),
                                   unroll=SCAN_UNROLL)
        s5_state[:, re0:re0 + bs] = re_f
        s5_state[:, im0:im0 + bs] = im_f

    y5 = jnp.concatenate(
        [_dot(bu_s[:, 2 * bs * j:2 * bs * (j + 1)].astype(_BF16), s5c_ref[j]) for j in range(n_s5_blocks)],
        axis=1) + s5d_ref[...] * u5
    glu = _dot(jax.nn.gelu(y5).astype(_BF16), w_glu_ref[...])
    branch_b = glu[:, :d] * jax.nn.sigmoid(glu[:, d:])

    c1 = c0 + s5w
    g_a = jax.nn.sigmoid(_dot(hb, w_in_ref[:, c1:c1 + d]) + b_in_ref[:, c1:c1 + d])
    g_b = jax.nn.sigmoid(_dot(hb, w_in_ref[:, c1 + d:c1 + 2 * d]) + b_in_ref[:, c1 + d:c1 + 2 * d])
    mix = _dot((g_a * branch_a + g_b * branch_b).astype(_BF16), w_out_ref[...])
    x1 = _layer_norm(alpha * x + _rows(1.0 + mod(2), steps) * mix, ln_g_ref[...], ln_b_ref[...])
    x1_ref[...] = x1

    h2 = x1 * _rows(1.0 + mod(4), steps) + _rows(mod(3), steps)
    h2b = h2.astype(_BF16)
    h2_ref[...] = _pack_rows(h2)
    lane = lax.broadcasted_iota(jnp.int32, (m, V7X_LANES), 1)
    lane_f = lane.astype(_F32)
    neg_inf = jnp.float32(-jnp.inf)
    logits = jnp.where(lane < n_experts, _dot(h2b, w_r_ref[...]) + b_r_ref[...], neg_inf)
    onehot = jnp.zeros((m, V7X_LANES), _F32)
    picks, vals = [], []
    for _ in range(TOP_K):
        v = jnp.max(logits, axis=-1, keepdims=True)
        p = jnp.min(jnp.where(logits == v, lane_f, float(V7X_LANES)), axis=-1, keepdims=True)
        hit = lane_f == p
        onehot = jnp.where(hit, 1.0, onehot)
        logits = jnp.where(hit, neg_inf, logits)
        picks.append(p)
        vals.append(v)
    exps = [jnp.exp(v - vals[0]) for v in vals]
    inv_den = 1.0 / functools.reduce(lambda s, e: s + e, exps)
    row = lax.broadcasted_iota(jnp.int32, (m, m), 0)
    col = lax.broadcasted_iota(jnp.int32, (m, m), 1)
    earlier = jnp.where(col < row, 1.0, 0.0).astype(_BF16)
    before = _dot(earlier, onehot.astype(_BF16)) + cnt_s[0:1, :]
    prob_out = jnp.zeros((m, V7X_LANES), _F32)
    dest_out = jnp.zeros((m, V7X_LANES), _F32)
    for k in range(TOP_K):
        rank_k = jnp.sum(jnp.where(lane_f == picks[k], before, 0.0), axis=-1, keepdims=True)
        prob_out = jnp.where(lane == k, exps[k] * inv_den, prob_out)
        dest_out = jnp.where(lane == k, picks[k] * float(region_rows) + rank_k, dest_out)
    prob_ref[...] = prob_out
    dest_t = dest_out.T[0:V7X_SUBLANES, :].astype(jnp.int32)
    for j in range(m // V7X_LANES):
        dest_ref[j] = dest_t[:, j * V7X_LANES:(j + 1) * V7X_LANES]
    cnt_new = cnt_s[...] + jnp.sum(onehot, axis=0, keepdims=True)
    cnt_s[...] = cnt_new
    cnt_ref[...] = cnt_new


def _mixer(xt, mod, p, *, alpha, batch, n_experts):
    tokens, d = xt.shape
    steps = MIXER_STEPS
    m = steps * batch
    n_s5_blocks = p["s5b"].shape[0]
    s5_lanes = n_s5_blocks * 2 * S5_BLOCK_STATES
    halo = (CONV_WIDTH - 1) * batch

    def const(a):
        nd = a.ndim
        return pl.BlockSpec(a.shape, lambda i, nd=nd: (0,) * nd, pipeline_mode=pl.Buffered(1))

    weights = [p["w_in"], p["b_in"], p["conv_w"], p["conv_b"], p["wg"], p["bg"], p["lamc"], p["w_rnn"],
               p["s5ar"], p["s5ai"], p["s5b"], p["s5c"], p["s5d"], p["w_glu"], p["w_out"],
               p["ln1_g"], p["ln1_b"], p["w_r"], p["b_r"]]
    row_spec = lambda width: pl.BlockSpec((m, width), lambda i: (i, 0))
    chunks = m // V7X_LANES
    out_shape = (
        jax.ShapeDtypeStruct((tokens, d), _F32),
        jax.ShapeDtypeStruct((tokens, d // 2), jnp.int32),
        jax.ShapeDtypeStruct((tokens // V7X_LANES, V7X_SUBLANES, V7X_LANES), jnp.int32),
        jax.ShapeDtypeStruct((tokens, V7X_LANES), _F32),
        jax.ShapeDtypeStruct((V7X_SUBLANES, V7X_LANES), _F32),
    )
    scratch = [
        pltpu.VMEM((m + halo, d), _F32),
        pltpu.VMEM((m, d), _F32),
        pltpu.VMEM((m, d), _F32),
        pltpu.VMEM((m, s5_lanes), _F32),
        pltpu.VMEM((batch, d), _F32),
        pltpu.VMEM((batch, s5_lanes), _F32),
        pltpu.VMEM((V7X_SUBLANES, V7X_LANES), _F32),
    ]
    weight_bytes = sum(w.size * w.dtype.itemsize for w in weights)
    act_bytes = m * d * 4
    vmem = weight_bytes + 2 * (2 * act_bytes + act_bytes // 2) + (3 + 4) * act_bytes + 16 * act_bytes
    kern = functools.partial(_mixer_kernel, alpha, steps, batch, d, n_s5_blocks, n_experts, tokens)
    return pl.pallas_call(
        kern,
        grid=(tokens // m,),
        in_specs=[row_spec(d), const(mod)] + [const(w) for w in weights],
        out_specs=(row_spec(d), row_spec(d // 2),
                   pl.BlockSpec((chunks, V7X_SUBLANES, V7X_LANES), lambda i: (i, 0, 0)),
                   row_spec(V7X_LANES),
                   pl.BlockSpec((V7X_SUBLANES, V7X_LANES), lambda i: (0, 0))),
        out_shape=out_shape,
        scratch_shapes=scratch,
        compiler_params=pltpu.CompilerParams(dimension_semantics=("arbitrary",),
                                             vmem_limit_bytes=_vmem_limit(vmem)),
        name="mixer",
    )(xt, mod, *weights)


def _expert_kernel(d_ff, tile_e_ref, tile_blk_ref, n_used_ref,
                   x_ref, wgu_ref, bgu_ref, wd_ref, bd_ref, y_ref, wgu_s, wd_s):
    i = pl.program_id(0)
    d = wgu_ref.shape[1]

    @pl.when((i == 0) | (tile_e_ref[i] != tile_e_ref[jnp.maximum(i - 1, 0)]))
    def _():
        def cast(c, carry):
            r0 = pl.multiple_of(c * WEIGHT_CAST_ROWS, WEIGHT_CAST_ROWS)
            wgu_s[pl.ds(r0, WEIGHT_CAST_ROWS), :] = wgu_ref[0, pl.ds(r0, WEIGHT_CAST_ROWS), :].astype(_BF16)
            wd_s[pl.ds(r0, WEIGHT_CAST_ROWS), :] = wd_ref[0, pl.ds(r0, WEIGHT_CAST_ROWS), :].astype(_BF16)
            return carry

        lax.fori_loop(0, d // WEIGHT_CAST_ROWS, cast, 0)

    @pl.when(i < n_used_ref[0])
    def _():
        lo, hi = _unpack_rows(x_ref[...])
        x = jnp.concatenate([lo, hi], axis=1).astype(_BF16)
        gu = _dot(x, wgu_s[...]) + bgu_ref[0]
        gate = jnp.minimum(gu[:, :d_ff], SWIGLU_LIMIT)
        up = jnp.clip(gu[:, d_ff:], -SWIGLU_LIMIT, SWIGLU_LIMIT)
        act = gate * jax.nn.sigmoid(SWIGLU_ALPHA * gate) * (up + 1.0)
        y_ref[...] = _pack_rows(_dot(act.astype(_BF16), wd_s[...]) + bd_ref[0])

    @pl.when(i >= n_used_ref[0])
    def _():
        y_ref[...] = jnp.zeros_like(y_ref)


def _experts(xb, tile_e, tile_blk, n_used, w_gu, b_gu, w_down, b_down, *, n_tiles):
    n_rows, half = xb.shape
    n_experts, d, two_ff = w_gu.shape
    d_ff = two_ff // 2
    assert d_ff == d, "the weight cast loop walks w_gu and w_down rows together"
    r = EXPERT_ROWS
    vmem = 2 * (d * two_ff + d_ff * d) * 4 + (d * two_ff + d_ff * d) * 2 + 8 * r * half * 4 + 6 * r * two_ff * 4
    grid_spec = pltpu.PrefetchScalarGridSpec(
        num_scalar_prefetch=3,
        grid=(n_tiles,),
        in_specs=[
            pl.BlockSpec((r, half), lambda i, te, tb, nu: (tb[i], 0)),
            pl.BlockSpec((1, d, two_ff), lambda i, te, tb, nu: (te[i], 0, 0)),
            pl.BlockSpec((1, 1, two_ff), lambda i, te, tb, nu: (te[i], 0, 0)),
            pl.BlockSpec((1, d_ff, d), lambda i, te, tb, nu: (te[i], 0, 0)),
            pl.BlockSpec((1, 1, d), lambda i, te, tb, nu: (te[i], 0, 0)),
        ],
        out_specs=pl.BlockSpec((r, half), lambda i, te, tb, nu: (tb[i], 0)),
        scratch_shapes=[pltpu.VMEM((d, two_ff), _BF16), pltpu.VMEM((d_ff, d), _BF16)],
    )
    return pl.pallas_call(
        functools.partial(_expert_kernel, d_ff),
        grid_spec=grid_spec,
        out_shape=jax.ShapeDtypeStruct((n_rows, half), jnp.int32),
        compiler_params=pltpu.CompilerParams(dimension_semantics=("arbitrary",),
                                             vmem_limit_bytes=_vmem_limit(vmem)),
        name="experts",
    )(tile_e, tile_blk, n_used, xb, w_gu, b_gu.reshape(n_experts, 1, two_ff), w_down,
      b_down.reshape(n_experts, 1, d))


def _sc_workers():
    info = plsc.get_sparse_core_info()
    return info.num_cores, info.num_subcores


def _dispatch(h2w, dest_c, n_rows):
    tokens, width = h2w.shape
    n_chunks, _, chunk = dest_c.shape
    nc, ns = _sc_workers()
    per_w = n_chunks // (nc * ns)
    assert per_w * nc * ns == n_chunks

    @functools.partial(
        pl.kernel, mesh=plsc.VectorSubcoreMesh(core_axis_name="c", subcore_axis_name="s"),
        out_type=jax.ShapeDtypeStruct((n_rows, width), h2w.dtype),
        scratch_types=[pltpu.VMEM(dest_c.shape[1:], jnp.int32), pltpu.VMEM((chunk, width), h2w.dtype)],
    )
    def scatter_rows(h_hbm, d_hbm, o_hbm, idx_v, rows_v):
        wid = lax.axis_index("s") * nc + lax.axis_index("c")

        @pl.loop(0, per_w)
        def _(j):
            blk = wid * per_w + j
            pltpu.sync_copy(d_hbm.at[blk], idx_v)
            pltpu.sync_copy(h_hbm.at[pl.ds(pl.multiple_of(blk * chunk, chunk), chunk)], rows_v)
            for k in range(TOP_K):
                pltpu.sync_copy(rows_v, o_hbm.at[idx_v.at[k]])

    return scatter_rows(h2w, dest_c)


def _collect(yb, dest_c):
    _, width = yb.shape
    n_chunks, _, chunk = dest_c.shape
    nc, ns = _sc_workers()
    per_w = n_chunks // (nc * ns)
    assert per_w * nc * ns == n_chunks

    @functools.partial(
        pl.kernel, mesh=plsc.VectorSubcoreMesh(core_axis_name="c", subcore_axis_name="s"),
        out_type=jax.ShapeDtypeStruct((TOP_K, n_chunks * chunk, width), yb.dtype),
        scratch_types=[pltpu.VMEM(dest_c.shape[1:], jnp.int32), pltpu.VMEM((chunk, width), yb.dtype)],
    )
    def gather_rows(y_hbm, d_hbm, o_hbm, idx_v, rows_v):
        wid = lax.axis_index("s") * nc + lax.axis_index("c")

        @pl.loop(0, per_w)
        def _(j):
            blk = wid * per_w + j
            pltpu.sync_copy(d_hbm.at[blk], idx_v)
            for k in range(TOP_K):
                pltpu.sync_copy(y_hbm.at[idx_v.at[k]], rows_v)
                pltpu.sync_copy(rows_v, o_hbm.at[k, pl.ds(pl.multiple_of(blk * chunk, chunk), chunk)])

    return gather_rows(yb, dest_c)


def _combine_kernel(alpha, steps, x1_ref, yg_ref, prob_ref, mod_ref, ln_g_ref, ln_b_ref, o_ref):
    d = x1_ref.shape[1]
    ffn_lo = jnp.zeros((x1_ref.shape[0], d // 2), _F32)
    ffn_hi = jnp.zeros((x1_ref.shape[0], d // 2), _F32)
    for k in range(TOP_K):
        lo, hi = _unpack_rows(yg_ref[k])
        ffn_lo = ffn_lo + prob_ref[:, k:k + 1] * lo
        ffn_hi = ffn_hi + prob_ref[:, k:k + 1] * hi
    ffn = jnp.concatenate([ffn_lo, ffn_hi], axis=1)
    gate = _rows(1.0 + mod_ref[:, 5 * d:6 * d], steps)
    o_ref[...] = _layer_norm(alpha * x1_ref[...] + gate * ffn, ln_g_ref[...], ln_b_ref[...])


def _combine(x1, yg, prob, mod, ln_g, ln_b, *, alpha, batch):
    tokens, d = x1.shape
    rows = COMBINE_ROWS
    const = lambda a: pl.BlockSpec(a.shape, lambda i: (0, 0))
    return pl.pallas_call(
        functools.partial(_combine_kernel, alpha, rows // batch),
        grid=(tokens // rows,),
        in_specs=[
            pl.BlockSpec((rows, d), lambda i: (i, 0)),
            pl.BlockSpec((TOP_K, rows, d // 2), lambda i: (0, i, 0)),
            pl.BlockSpec((rows, V7X_LANES), lambda i: (i, 0)),
            const(mod), const(ln_g), const(ln_b),
        ],
        out_specs=pl.BlockSpec((rows, d), lambda i: (i, 0)),
        out_shape=jax.ShapeDtypeStruct((tokens, d), _F32),
        compiler_params=pltpu.CompilerParams(dimension_semantics=("parallel",)),
        name="combine",
    )(x1, yg, prob, mod, ln_g, ln_b)


def _block_diag(blocks):
    n, a, b = blocks.shape
    eye = jnp.eye(n, dtype=blocks.dtype)
    return (eye[:, None, :, None] * blocks[:, :, None, :]).reshape(n * a, n * b)


def _s5_params(lam_re, lam_im, log_dt, b_re, b_im, c_re, c_im, d_skip):
    groups = lam_re.shape[0]
    nb = groups // S5_BLOCK_GROUPS
    dt = jnp.exp(log_dt)[:, None]
    mag = jnp.exp(lam_re * dt)
    ab_re, ab_im = mag * jnp.cos(lam_im * dt), mag * jnp.sin(lam_im * dt)
    den = lam_re * lam_re + lam_im * lam_im
    q_re = ((ab_re - 1.0) * lam_re + ab_im * lam_im) / den
    q_im = (ab_im * lam_re - (ab_re - 1.0) * lam_im) / den
    bb_re = q_re[..., None] * b_re - q_im[..., None] * b_im
    bb_im = q_re[..., None] * b_im + q_im[..., None] * b_re

    def per_block(a):
        return a.reshape(nb, S5_BLOCK_GROUPS, *a.shape[1:])

    bmat = jnp.stack([
        jnp.concatenate([_block_diag(jnp.swapaxes(per_block(bb_re)[j], 1, 2)),
                         _block_diag(jnp.swapaxes(per_block(bb_im)[j], 1, 2))], axis=1)
        for j in range(nb)])
    cmat = jnp.stack([
        jnp.concatenate([_block_diag(jnp.swapaxes(per_block(c_re)[j], 1, 2)),
                         -_block_diag(jnp.swapaxes(per_block(c_im)[j], 1, 2))], axis=0)
        for j in range(nb)])
    tile = lambda a: jnp.broadcast_to(a.reshape(nb, 1, S5_BLOCK_STATES), (nb, V7X_SUBLANES, S5_BLOCK_STATES))
    return dict(s5ar=tile(ab_re), s5ai=tile(ab_im), s5b=bmat.astype(_BF16), s5c=cmat.astype(_BF16),
                s5d=d_skip.reshape(1, -1))


def kernel(x, c, w_ada, b_ada, w_in, b_in, conv_w, conv_b, w_rg_a, b_rg_a, w_rg_x, b_rg_x, lru_lambda, w_rnn_out, s5_lambda_re, s5_lambda_im, s5_log_dt, s5_b_re, s5_b_im, s5_c_re, s5_c_im, s5_d, w_glu, w_out, ln1_g, ln1_b, w_router, b_router, w_gu, b_gu, w_down, b_down, ln2_g, ln2_b):
    batch, seq, d = x.shape
    depth = w_ada.shape[0]
    n_experts = w_router.shape[-1]
    tokens = batch * seq
    alpha = (2.0 * depth) ** 0.25
    assert batch == V7X_SUBLANES and d % V7X_LANES == 0 and n_experts <= V7X_LANES
    assert seq % MIXER_STEPS == 0 and tokens % COMBINE_ROWS == 0 and tokens % EXPERT_ROWS == 0
    assert (MIXER_STEPS * batch) % V7X_LANES == 0

    xt = jnp.swapaxes(x, 0, 1).reshape(tokens, d)
    c_cur = c
    for l in range(depth):
        mod = _ada(c_cur, w_ada[l], b_ada[l])
        row = lambda v: v.reshape(1, -1)
        p = dict(
            w_in=w_in[l].astype(_BF16), b_in=row(b_in[l]), conv_w=conv_w[l], conv_b=row(conv_b[l]),
            wg=jnp.concatenate([w_rg_a[l], w_rg_x[l]], axis=-1).astype(_BF16),
            bg=row(jnp.concatenate([b_rg_a[l], b_rg_x[l]])),
            lamc=row(-LRU_C * jax.nn.softplus(-lru_lambda[l])),
            w_rnn=w_rnn_out[l].astype(_BF16), w_glu=w_glu[l].astype(_BF16), w_out=w_out[l].astype(_BF16),
            ln1_g=row(ln1_g[l]), ln1_b=row(ln1_b[l]),
            w_r=jnp.pad(w_router[l], ((0, 0), (0, V7X_LANES - n_experts))).astype(_BF16),
            b_r=row(jnp.pad(b_router[l], (0, V7X_LANES - n_experts))),
            **_s5_params(s5_lambda_re[l], s5_lambda_im[l], s5_log_dt[l], s5_b_re[l], s5_b_im[l],
                         s5_c_re[l], s5_c_im[l], s5_d[l]),
        )
        x1, h2w, dest_c, prob, cnt = _mixer(xt, mod, p, alpha=alpha, batch=batch, n_experts=n_experts)

        r = EXPERT_ROWS
        blocks_per_region = tokens // r
        spare_blk = n_experts * blocks_per_region
        n_tiles = -(-(tokens * TOP_K + n_experts * (r - 1)) // r)
        counts = cnt[0, :n_experts].astype(jnp.int32)
        tiles_e = (counts + r - 1) // r
        tile_end = jnp.cumsum(tiles_e)
        tile_start = tile_end - tiles_e
        n_used = tile_end[-1:]
        t_ids = jnp.arange(n_tiles, dtype=jnp.int32)
        tile_e = jnp.minimum(jnp.sum((tile_end[None, :] <= t_ids[:, None]).astype(jnp.int32), axis=1),
                             n_experts - 1)
        tile_blk = jnp.where(t_ids < n_used, tile_e * blocks_per_region + t_ids - tile_start[tile_e], spare_blk)

        xb = _dispatch(h2w, dest_c, (spare_blk + 1) * r)
        yb = _experts(xb, tile_e, tile_blk, n_used, w_gu[l], b_gu[l], w_down[l], b_down[l], n_tiles=n_tiles)
        yg = _collect(yb, dest_c)
        xt = _combine(x1, yg, prob, mod, ln2_g[l].reshape(1, -1), ln2_b[l].reshape(1, -1),
                      alpha=alpha, batch=batch)
    return jnp.swapaxes(xt.reshape(seq, batch, d), 0, 1)
```

```python
import functools
import math

import jax
import jax.numpy as jnp
from jax import lax
from jax.experimental import pallas as pl
from jax.experimental.pallas import tpu as pltpu
from jax.experimental.pallas import tpu_sc as plsc

V7X_SUBLANES = 8
V7X_LANES = 128
V7X_VMEM_BYTES = 64 * 1024 * 1024

CONV_WIDTH = 4
LRU_C = 8.0
S5_GROUP = 16
S5_STATE = 64
TOP_K = 4
SWIGLU_LIMIT = 7.0
SWIGLU_ALPHA = 1.702
LN_EPS = 1e-5

S5_BLOCK_GROUPS = V7X_LANES // S5_GROUP
S5_BLOCK_STATES = S5_BLOCK_GROUPS * S5_STATE

MIXER_STEPS = 32
EXPERT_ROWS = 512
COMBINE_ROWS = 512
WEIGHT_CAST_ROWS = 64
SCAN_UNROLL = 4

_BF16 = jnp.bfloat16
_F32 = jnp.float32


def _dot(a, b):
    return jnp.dot(a, b, preferred_element_type=_F32)


def _vmem_limit(nbytes):
    return int(min(nbytes, V7X_VMEM_BYTES - 4 * 1024 * 1024))


def _layer_norm(z, gain, bias):
    mu = jnp.mean(z, axis=-1, keepdims=True)
    zc = z - mu
    var = jnp.mean(zc * zc, axis=-1, keepdims=True)
    return zc * lax.rsqrt(var + LN_EPS) * gain + bias


def _rows(v, steps):
    return jnp.tile(v, (steps, 1))


_HI_MASK = 0xFFFF0000


def _pack_rows(v):
    half = v.shape[1] // 2
    bits = lax.bitcast_convert_type(v.astype(_BF16).astype(_F32), jnp.uint32)
    packed = (bits[:, :half] >> 16) | (bits[:, half:] & jnp.uint32(_HI_MASK))
    return lax.bitcast_convert_type(packed, jnp.int32)


def _unpack_rows(w):
    bits = lax.bitcast_convert_type(w, jnp.uint32)
    lo = lax.bitcast_convert_type(bits << 16, _F32)
    hi = lax.bitcast_convert_type(bits & jnp.uint32(_HI_MASK), _F32)
    return lo, hi


def _ada_kernel(c_ref, w_ref, b_ref, o_ref):
    c = c_ref[...]
    c_act = (c * jax.nn.sigmoid(c)).astype(_BF16)
    o_ref[...] = _dot(c_act, w_ref[...].astype(_BF16)) + b_ref[...]


def _ada(c, w_ada, b_ada):
    batch, d = c.shape
    n_out = w_ada.shape[1]
    return pl.pallas_call(
        _ada_kernel,
        grid=(n_out // d,),
        in_specs=[
            pl.BlockSpec((batch, d), lambda j: (0, 0)),
            pl.BlockSpec((d, d), lambda j: (0, j)),
            pl.BlockSpec((1, d), lambda j: (0, j)),
        ],
        out_specs=pl.BlockSpec((batch, d), lambda j: (0, j)),
        out_shape=jax.ShapeDtypeStruct((batch, n_out), _F32),
        name="ada",
    )(c, w_ada, b_ada.reshape(1, n_out))


def _mixer_kernel(alpha, steps, batch, d, n_s5_blocks, n_experts, region_rows,
                  x_ref, mod_ref, w_in_ref, b_in_ref, conv_w_ref, conv_b_ref, wg_ref, bg_ref,
                  lamc_ref, w_rnn_ref, s5ar_ref, s5ai_ref, s5b_ref, s5c_ref, s5d_ref,
                  w_glu_ref, w_out_ref, ln_g_ref, ln_b_ref, w_r_ref, b_r_ref,
                  x1_ref, h2_ref, dest_ref, prob_ref, cnt_ref,
                  xt_s, xc_s, a_s, u_s, bu_s, h_state, s5_state, cnt_s):
    m = steps * batch
    halo = (CONV_WIDTH - 1) * batch
    s5w = n_s5_blocks * V7X_LANES
    n_blk = d // V7X_LANES
    bs = S5_BLOCK_STATES

    @pl.when(pl.program_id(0) == 0)
    def _():
        xc_s[0:halo, :] = jnp.zeros((halo, d), _F32)
        h_state[...] = jnp.zeros_like(h_state)
        s5_state[...] = jnp.zeros_like(s5_state)
        cnt_s[...] = jnp.zeros_like(cnt_s)

    def mod(k):
        return mod_ref[:, k * d:(k + 1) * d]

    for b in range(batch):
        for j in range(n_blk):
            xt_s[j, pl.ds(b, steps, stride=batch), :] = x_ref[b, :, j * V7X_LANES:(j + 1) * V7X_LANES]
    x = jnp.concatenate([xt_s[j] for j in range(n_blk)], axis=1)
    h = x * _rows(1.0 + mod(1), steps) + _rows(mod(0), steps)
    hb = h.astype(_BF16)

    xc_s[halo:halo + m, :] = _dot(hb, w_in_ref[:, 0:d]) + b_in_ref[:, 0:d]
    xr = jnp.zeros((m, d), _F32) + conv_b_ref[...]
    for k in range(CONV_WIDTH):
        xr = xr + conv_w_ref[k:k + 1, :] * xc_s[k * batch:k * batch + m, :]
    xc_s[0:halo, :] = xc_s[m:m + halo, :]
    xrb = xr.astype(_BF16)
    gates = [_dot(xrb[:, j * V7X_LANES:(j + 1) * V7X_LANES], wg_ref[j]) for j in range(n_blk)]
    r_gate = jax.nn.sigmoid(jnp.concatenate([g[:, :V7X_LANES] for g in gates], axis=1) + bg_ref[:, 0:d])
    i_gate = jax.nn.sigmoid(jnp.concatenate([g[:, V7X_LANES:] for g in gates], axis=1) + bg_ref[:, d:2 * d])
    a = jnp.exp(lamc_ref[...] * r_gate)
    a_s[...] = a
    u_s[...] = jnp.sqrt(1.0 - a * a) * (i_gate * xr)

    def lru_step(t, hc):
        r0 = pl.multiple_of(t * batch, batch)
        hn = a_s[pl.ds(r0, batch), :] * hc + u_s[pl.ds(r0, batch), :]
        u_s[pl.ds(r0, batch), :] = hn
        return hn

    h_state[...] = lax.fori_loop(0, steps, lru_step, h_state[...], unroll=SCAN_UNROLL)
    y_rnn = _dot(hb, w_in_ref[:, d:2 * d]) + b_in_ref[:, d:2 * d]
    branch_a = _dot((jax.nn.gelu(y_rnn) * u_s[...]).astype(_BF16), w_rnn_ref[...])

    c0 = 2 * d
    u5 = _dot(hb, w_in_ref[:, c0:c0 + s5w]) + b_in_ref[:, c0:c0 + s5w]
    u5b = u5.astype(_BF16)
    for j in range(n_s5_blocks):
        bu_s[:, 2 * bs * j:2 * bs * (j + 1)] = _dot(u5b[:, j * V7X_LANES:(j + 1) * V7X_LANES], s5b_ref[j])

    for j in range(n_s5_blocks):
        re0, im0 = 2 * bs * j, 2 * bs * j + bs
        ar = s5ar_ref[j]
        ai = s5ai_ref[j]

        def s5_step(t, carry, re0=re0, im0=im0, ar=ar, ai=ai):
            re, im = carry
            r0 = pl.multiple_of(t * batch, batch)
            nre = ar * re - ai * im + bu_s[pl.ds(r0, batch), re0:re0 + bs]
            nim = ar * im + ai * re + bu_s[pl.ds(r0, batch), im0:im0 + bs]
            bu_s[pl.ds(r0, batch), re0:re0 + bs] = nre
            bu_s[pl.ds(r0, batch), im0:im0 + bs] = nim
            return nre, nim

        re_f, im_f = lax.fori_loop(0, steps, s5_step,
                                   (s5_state[:, re0:re0 + bs], s5_state[:, im0:im0 + bs]),
                                   unroll=SCAN_UNROLL)
        s5_state[:, re0:re0 + bs] = re_f
        s5_state[:, im0:im0 + bs] = im_f

    y5 = jnp.concatenate(
        [_dot(bu_s[:, 2 * bs * j:2 * bs * (j + 1)].astype(_BF16), s5c_ref[j]) for j in range(n_s5_blocks)],
        axis=1) + s5d_ref[...] * u5
    glu = _dot(jax.nn.gelu(y5).astype(_BF16), w_glu_ref[...])
    branch_b = glu[:, :d] * jax.nn.sigmoid(glu[:, d:])

    c1 = c0 + s5w
    g_a = jax.nn.sigmoid(_dot(hb, w_in_ref[:, c1:c1 + d]) + b_in_ref[:, c1:c1 + d])
    g_b = jax.nn.sigmoid(_dot(hb, w_in_ref[:, c1 + d:c1 + 2 * d]) + b_in_ref[:, c1 + d:c1 + 2 * d])
    mix = _dot((g_a * branch_a + g_b * branch_b).astype(_BF16), w_out_ref[...])
    x1 = _layer_norm(alpha * x + _rows(1.0 + mod(2), steps) * mix, ln_g_ref[...], ln_b_ref[...])
    x1_ref[...] = x1

    h2 = x1 * _rows(1.0 + mod(4), steps) + _rows(mod(3), steps)
    h2b = h2.astype(_BF16)
    h2_ref[...] = _pack_rows(h2)
    lane = lax.broadcasted_iota(jnp.int32, (m, V7X_LANES), 1)
    lane_f = lane.astype(_F32)
    neg_inf = jnp.float32(-jnp.inf)
    logits = jnp.where(lane < n_experts, _dot(h2b, w_r_ref[...]) + b_r_ref[...], neg_inf)
    onehot = jnp.zeros((m, V7X_LANES), _F32)
    picks, vals = [], []
    for _ in range(TOP_K):
        v = jnp.max(logits, axis=-1, keepdims=True)
        p = jnp.min(jnp.where(logits == v, lane_f, float(V7X_LANES)), axis=-1, keepdims=True)
        hit = lane_f == p
        onehot = jnp.where(hit, 1.0, onehot)
        logits = jnp.where(hit, neg_inf, logits)
        picks.append(p)
        vals.append(v)
    exps = [jnp.exp(v - vals[0]) for v in vals]
    inv_den = 1.0 / functools.reduce(lambda s, e: s + e, exps)
    row = lax.broadcasted_iota(jnp.int32, (m, m), 0)
    col = lax.broadcasted_iota(jnp.int32, (m, m), 1)
    earlier = jnp.where(col < row, 1.0, 0.0).astype(_BF16)
    before = _dot(earlier, onehot.astype(_BF16)) + cnt_s[0:1, :]
    prob_out = jnp.zeros((m, V7X_LANES), _F32)
    dest_out = jnp.zeros((m, V7X_LANES), _F32)
    for k in range(TOP_K):
        rank_k = jnp.sum(jnp.where(lane_f == picks[k], before, 0.0), axis=-1, keepdims=True)
        prob_out = jnp.where(lane == k, exps[k] * inv_den, prob_out)
        dest_out = jnp.where(lane == k, picks[k] * float(region_rows) + rank_k, dest_out)
    prob_ref[...] = prob_out
    dest_t = dest_out.T[0:V7X_SUBLANES, :].astype(jnp.int32)
    for j in range(m // V7X_LANES):
        dest_ref[j] = dest_t[:, j * V7X_LANES:(j + 1) * V7X_LANES]
    cnt_new = cnt_s[...] + jnp.sum(onehot, axis=0, keepdims=True)
    cnt_s[...] = cnt_new
    cnt_ref[...] = cnt_new


def _mixer(x, mod, p, *, alpha, n_experts):
    batch, seq, d = x.shape
    tokens = batch * seq
    steps = MIXER_STEPS
    m = steps * batch
    n_s5_blocks = p["s5b"].shape[0]
    s5_lanes = n_s5_blocks * 2 * S5_BLOCK_STATES
    halo = (CONV_WIDTH - 1) * batch

    def const(a):
        nd = a.ndim
        return pl.BlockSpec(a.shape, lambda i, nd=nd: (0,) * nd, pipeline_mode=pl.Buffered(1))

    weights = [p["w_in"], p["b_in"], p["conv_w"], p["conv_b"], p["wg"], p["bg"], p["lamc"], p["w_rnn"],
               p["s5ar"], p["s5ai"], p["s5b"], p["s5c"], p["s5d"], p["w_glu"], p["w_out"],
               p["ln1_g"], p["ln1_b"], p["w_r"], p["b_r"]]
    row_spec = lambda width: pl.BlockSpec((m, width), lambda i: (i, 0))
    chunks = m // V7X_LANES
    out_shape = (
        jax.ShapeDtypeStruct((tokens, d), _F32),
        jax.ShapeDtypeStruct((tokens, d // 2), jnp.int32),
        jax.ShapeDtypeStruct((tokens // V7X_LANES, V7X_SUBLANES, V7X_LANES), jnp.int32),
        jax.ShapeDtypeStruct((tokens, V7X_LANES), _F32),
        jax.ShapeDtypeStruct((V7X_SUBLANES, V7X_LANES), _F32),
    )
    scratch = [
        pltpu.VMEM((d // V7X_LANES, m, V7X_LANES), _F32),
        pltpu.VMEM((m + halo, d), _F32),
        pltpu.VMEM((m, d), _F32),
        pltpu.VMEM((m, d), _F32),
        pltpu.VMEM((m, s5_lanes), _F32),
        pltpu.VMEM((batch, d), _F32),
        pltpu.VMEM((batch, s5_lanes), _F32),
        pltpu.VMEM((V7X_SUBLANES, V7X_LANES), _F32),
    ]
    weight_bytes = sum(w.size * w.dtype.itemsize for w in weights)
    act_bytes = m * d * 4
    vmem = weight_bytes + 2 * (2 * act_bytes + act_bytes // 2) + (3 + 4) * act_bytes + 16 * act_bytes
    kern = functools.partial(_mixer_kernel, alpha, steps, batch, d, n_s5_blocks, n_experts, tokens)
    return pl.pallas_call(
        kern,
        grid=(tokens // m,),
        in_specs=[pl.BlockSpec((batch, steps, d), lambda i: (0, i, 0)), const(mod)] + [const(w) for w in weights],
        out_specs=(row_spec(d), row_spec(d // 2),
                   pl.BlockSpec((chunks, V7X_SUBLANES, V7X_LANES), lambda i: (i, 0, 0)),
                   row_spec(V7X_LANES),
                   pl.BlockSpec((V7X_SUBLANES, V7X_LANES), lambda i: (0, 0))),
        out_shape=out_shape,
        scratch_shapes=scratch,
        compiler_params=pltpu.CompilerParams(dimension_semantics=("arbitrary",),
                                             vmem_limit_bytes=_vmem_limit(vmem)),
        name="mixer",
    )(x, mod, *weights)


def _expert_kernel(d_ff, tile_e_ref, tile_blk_ref, n_used_ref,
                   x_ref, wgu_ref, bgu_ref, wd_ref, bd_ref, y_ref, wgu_s, wd_s):
    i = pl.program_id(0)
    d = wgu_ref.shape[1]

    @pl.when((i == 0) | (tile_e_ref[i] != tile_e_ref[jnp.maximum(i - 1, 0)]))
    def _():
        def cast(c, carry):
            r0 = pl.multiple_of(c * WEIGHT_CAST_ROWS, WEIGHT_CAST_ROWS)
            wgu_s[pl.ds(r0, WEIGHT_CAST_ROWS), :] = wgu_ref[0, pl.ds(r0, WEIGHT_CAST_ROWS), :].astype(_BF16)
            wd_s[pl.ds(r0, WEIGHT_CAST_ROWS), :] = wd_ref[0, pl.ds(r0, WEIGHT_CAST_ROWS), :].astype(_BF16)
            return carry

        lax.fori_loop(0, d // WEIGHT_CAST_ROWS, cast, 0)

    @pl.when(i < n_used_ref[0])
    def _():
        lo, hi = _unpack_rows(x_ref[...])
        x = jnp.concatenate([lo, hi], axis=1).astype(_BF16)
        gu = _dot(x, wgu_s[...]) + bgu_ref[0]
        gate = jnp.minimum(gu[:, :d_ff], SWIGLU_LIMIT)
        up = jnp.clip(gu[:, d_ff:], -SWIGLU_LIMIT, SWIGLU_LIMIT)
        act = gate * jax.nn.sigmoid(SWIGLU_ALPHA * gate) * (up + 1.0)
        y_ref[...] = _pack_rows(_dot(act.astype(_BF16), wd_s[...]) + bd_ref[0])

    @pl.when(i >= n_used_ref[0])
    def _():
        y_ref[...] = jnp.zeros_like(y_ref)


def _experts(xb, tile_e, tile_blk, n_used, w_gu, b_gu, w_down, b_down, *, n_tiles):
    n_rows, half = xb.shape
    n_experts, d, two_ff = w_gu.shape
    d_ff = two_ff // 2
    assert d_ff == d, "the weight cast loop walks w_gu and w_down rows together"
    r = EXPERT_ROWS
    vmem = 2 * (d * two_ff + d_ff * d) * 4 + (d * two_ff + d_ff * d) * 2 + 8 * r * half * 4 + 6 * r * two_ff * 4
    grid_spec = pltpu.PrefetchScalarGridSpec(
        num_scalar_prefetch=3,
        grid=(n_tiles,),
        in_specs=[
            pl.BlockSpec((r, half), lambda i, te, tb, nu: (tb[i], 0)),
            pl.BlockSpec((1, d, two_ff), lambda i, te, tb, nu: (te[i], 0, 0)),
            pl.BlockSpec((1, 1, two_ff), lambda i, te, tb, nu: (te[i], 0, 0)),
            pl.BlockSpec((1, d_ff, d), lambda i, te, tb, nu: (te[i], 0, 0)),
            pl.BlockSpec((1, 1, d), lambda i, te, tb, nu: (te[i], 0, 0)),
        ],
        out_specs=pl.BlockSpec((r, half), lambda i, te, tb, nu: (tb[i], 0)),
        scratch_shapes=[pltpu.VMEM((d, two_ff), _BF16), pltpu.VMEM((d_ff, d), _BF16)],
    )
    return pl.pallas_call(
        functools.partial(_expert_kernel, d_ff),
        grid_spec=grid_spec,
        out_shape=jax.ShapeDtypeStruct((n_rows, half), jnp.int32),
        compiler_params=pltpu.CompilerParams(dimension_semantics=("arbitrary",),
                                             vmem_limit_bytes=_vmem_limit(vmem)),
        name="experts",
    )(tile_e, tile_blk, n_used, xb, w_gu, b_gu.reshape(n_experts, 1, two_ff), w_down,
      b_down.reshape(n_experts, 1, d))


def _sc_workers():
    info = plsc.get_sparse_core_info()
    return info.num_cores, info.num_subcores


def _dispatch(h2w, dest_c, n_rows):
    tokens, width = h2w.shape
    n_chunks, _, chunk = dest_c.shape
    nc, ns = _sc_workers()
    per_w = n_chunks // (nc * ns)
    assert per_w * nc * ns == n_chunks

    @functools.partial(
        pl.kernel, mesh=plsc.VectorSubcoreMesh(core_axis_name="c", subcore_axis_name="s"),
        out_type=jax.ShapeDtypeStruct((n_rows, width), h2w.dtype),
        scratch_types=[pltpu.VMEM(dest_c.shape[1:], jnp.int32), pltpu.VMEM((chunk, width), h2w.dtype)],
    )
    def scatter_rows(h_hbm, d_hbm, o_hbm, idx_v, rows_v):
        wid = lax.axis_index("s") * nc + lax.axis_index("c")

        @pl.loop(0, per_w)
        def _(j):
            blk = wid * per_w + j
            pltpu.sync_copy(d_hbm.at[blk], idx_v)
            pltpu.sync_copy(h_hbm.at[pl.ds(pl.multiple_of(blk * chunk, chunk), chunk)], rows_v)
            for k in range(TOP_K):
                pltpu.sync_copy(rows_v, o_hbm.at[idx_v.at[k]])

    return scatter_rows(h2w, dest_c)


def _collect(yb, dest_c):
    _, width = yb.shape
    n_chunks, _, chunk = dest_c.shape
    nc, ns = _sc_workers()
    per_w = n_chunks // (nc * ns)
    assert per_w * nc * ns == n_chunks

    @functools.partial(
        pl.kernel, mesh=plsc.VectorSubcoreMesh(core_axis_name="c", subcore_axis_name="s"),
        out_type=jax.ShapeDtypeStruct((TOP_K, n_chunks * chunk, width), yb.dtype),
        scratch_types=[pltpu.VMEM(dest_c.shape[1:], jnp.int32), pltpu.VMEM((chunk, width), yb.dtype)],
    )
    def gather_rows(y_hbm, d_hbm, o_hbm, idx_v, rows_v):
        wid = lax.axis_index("s") * nc + lax.axis_index("c")

        @pl.loop(0, per_w)
        def _(j):
            blk = wid * per_w + j
            pltpu.sync_copy(d_hbm.at[blk], idx_v)
            for k in range(TOP_K):
                pltpu.sync_copy(y_hbm.at[idx_v.at[k]], rows_v)
                pltpu.sync_copy(rows_v, o_hbm.at[k, pl.ds(pl.multiple_of(blk * chunk, chunk), chunk)])

    return gather_rows(yb, dest_c)


def _combine_kernel(alpha, steps, x1_ref, yg_ref, prob_ref, mod_ref, ln_g_ref, ln_b_ref, o_ref, ot_s):
    d = x1_ref.shape[1]
    batch = o_ref.shape[0]
    ffn_lo = jnp.zeros((x1_ref.shape[0], d // 2), _F32)
    ffn_hi = jnp.zeros((x1_ref.shape[0], d // 2), _F32)
    for k in range(TOP_K):
        lo, hi = _unpack_rows(yg_ref[k])
        ffn_lo = ffn_lo + prob_ref[:, k:k + 1] * lo
        ffn_hi = ffn_hi + prob_ref[:, k:k + 1] * hi
    ffn = jnp.concatenate([ffn_lo, ffn_hi], axis=1)
    gate = _rows(1.0 + mod_ref[:, 5 * d:6 * d], steps)
    out = _layer_norm(alpha * x1_ref[...] + gate * ffn, ln_g_ref[...], ln_b_ref[...])
    n_blk = d // V7X_LANES
    for j in range(n_blk):
        ot_s[j] = out[:, j * V7X_LANES:(j + 1) * V7X_LANES]
    for b in range(batch):
        for j in range(n_blk):
            o_ref[b, :, j * V7X_LANES:(j + 1) * V7X_LANES] = ot_s[j, pl.ds(b, steps, stride=batch), :]


def _combine(x1, yg, prob, mod, ln_g, ln_b, *, alpha, batch):
    tokens, d = x1.shape
    rows = COMBINE_ROWS
    steps = rows // batch
    const = lambda a: pl.BlockSpec(a.shape, lambda i: (0, 0))
    return pl.pallas_call(
        functools.partial(_combine_kernel, alpha, steps),
        grid=(tokens // rows,),
        in_specs=[
            pl.BlockSpec((rows, d), lambda i: (i, 0)),
            pl.BlockSpec((TOP_K, rows, d // 2), lambda i: (0, i, 0)),
            pl.BlockSpec((rows, V7X_LANES), lambda i: (i, 0)),
            const(mod), const(ln_g), const(ln_b),
        ],
        out_specs=pl.BlockSpec((batch, steps, d), lambda i: (0, i, 0)),
        out_shape=jax.ShapeDtypeStruct((batch, tokens // batch, d), _F32),
        scratch_shapes=[pltpu.VMEM((d // V7X_LANES, rows, V7X_LANES), _F32)],
        compiler_params=pltpu.CompilerParams(dimension_semantics=("parallel",)),
        name="combine",
    )(x1, yg, prob, mod, ln_g, ln_b)


def _block_diag(blocks):
    n, a, b = blocks.shape
    eye = jnp.eye(n, dtype=blocks.dtype)
    return (eye[:, None, :, None] * blocks[:, :, None, :]).reshape(n * a, n * b)


def _s5_params(lam_re, lam_im, log_dt, b_re, b_im, c_re, c_im, d_skip):
    groups = lam_re.shape[0]
    nb = groups // S5_BLOCK_GROUPS
    dt = jnp.exp(log_dt)[:, None]
    mag = jnp.exp(lam_re * dt)
    ab_re, ab_im = mag * jnp.cos(lam_im * dt), mag * jnp.sin(lam_im * dt)
    den = lam_re * lam_re + lam_im * lam_im
    q_re = ((ab_re - 1.0) * lam_re + ab_im * lam_im) / den
    q_im = (ab_im * lam_re - (ab_re - 1.0) * lam_im) / den
    bb_re = q_re[..., None] * b_re - q_im[..., None] * b_im
    bb_im = q_re[..., None] * b_im + q_im[..., None] * b_re

    def per_block(a):
        return a.reshape(nb, S5_BLOCK_GROUPS, *a.shape[1:])

    bmat = jnp.stack([
        jnp.concatenate([_block_diag(jnp.swapaxes(per_block(bb_re)[j], 1, 2)),
                         _block_diag(jnp.swapaxes(per_block(bb_im)[j], 1, 2))], axis=1)
        for j in range(nb)])
    cmat = jnp.stack([
        jnp.concatenate([_block_diag(jnp.swapaxes(per_block(c_re)[j], 1, 2)),
                         -_block_diag(jnp.swapaxes(per_block(c_im)[j], 1, 2))], axis=0)
        for j in range(nb)])
    tile = lambda a: jnp.broadcast_to(a.reshape(nb, 1, S5_BLOCK_STATES), (nb, V7X_SUBLANES, S5_BLOCK_STATES))
    return dict(s5ar=tile(ab_re), s5ai=tile(ab_im), s5b=bmat.astype(_BF16), s5c=cmat.astype(_BF16),
                s5d=d_skip.reshape(1, -1))


def kernel(x, c, w_ada, b_ada, w_in, b_in, conv_w, conv_b, w_rg_a, b_rg_a, w_rg_x, b_rg_x, lru_lambda, w_rnn_out, s5_lambda_re, s5_lambda_im, s5_log_dt, s5_b_re, s5_b_im, s5_c_re, s5_c_im, s5_d, w_glu, w_out, ln1_g, ln1_b, w_router, b_router, w_gu, b_gu, w_down, b_down, ln2_g, ln2_b):
    batch, seq, d = x.shape
    depth = w_ada.shape[0]
    n_experts = w_router.shape[-1]
    tokens = batch * seq
    alpha = (2.0 * depth) ** 0.25
    assert batch == V7X_SUBLANES and d % V7X_LANES == 0 and n_experts <= V7X_LANES
    assert seq % MIXER_STEPS == 0 and tokens % COMBINE_ROWS == 0 and tokens % EXPERT_ROWS == 0
    assert (MIXER_STEPS * batch) % V7X_LANES == 0

    for l in range(depth):
        mod = _ada(c, w_ada[l], b_ada[l])
        row = lambda v: v.reshape(1, -1)
        p = dict(
            w_in=w_in[l].astype(_BF16), b_in=row(b_in[l]), conv_w=conv_w[l], conv_b=row(conv_b[l]),
            wg=jnp.concatenate([w_rg_a[l], w_rg_x[l]], axis=-1).astype(_BF16),
            bg=row(jnp.concatenate([b_rg_a[l], b_rg_x[l]])),
            lamc=row(-LRU_C * jax.nn.softplus(-lru_lambda[l])),
            w_rnn=w_rnn_out[l].astype(_BF16), w_glu=w_glu[l].astype(_BF16), w_out=w_out[l].astype(_BF16),
            ln1_g=row(ln1_g[l]), ln1_b=row(ln1_b[l]),
            w_r=jnp.pad(w_router[l], ((0, 0), (0, V7X_LANES - n_experts))).astype(_BF16),
            b_r=row(jnp.pad(b_router[l], (0, V7X_LANES - n_experts))),
            **_s5_params(s5_lambda_re[l], s5_lambda_im[l], s5_log_dt[l], s5_b_re[l], s5_b_im[l],
                         s5_c_re[l], s5_c_im[l], s5_d[l]),
        )
        x1, h2w, dest_c, prob, cnt = _mixer(x, mod, p, alpha=alpha, n_experts=n_experts)

        r = EXPERT_ROWS
        blocks_per_region = tokens // r
        spare_blk = n_experts * blocks_per_region
        n_tiles = -(-(tokens * TOP_K + n_experts * (r - 1)) // r)
        counts = cnt[0, :n_experts].astype(jnp.int32)
        tiles_e = (counts + r - 1) // r
        tile_end = jnp.cumsum(tiles_e)
        tile_start = tile_end - tiles_e
        n_used = tile_end[-1:]
        t_ids = jnp.arange(n_tiles, dtype=jnp.int32)
        tile_e = jnp.minimum(jnp.sum((tile_end[None, :] <= t_ids[:, None]).astype(jnp.int32), axis=1),
                             n_experts - 1)
        tile_blk = jnp.where(t_ids < n_used, tile_e * blocks_per_region + t_ids - tile_start[tile_e], spare_blk)

        xb = _dispatch(h2w, dest_c, (spare_blk + 1) * r)
        yb = _experts(xb, tile_e, tile_blk, n_used, w_gu[l], b_gu[l], w_down[l], b_down[l], n_tiles=n_tiles)
        yg = _collect(yb, dest_c)
        x = _combine(x1, yg, prob, mod, ln2_g[l].reshape(1, -1), ln2_b[l].reshape(1, -1),
                     alpha=alpha, batch=batch)
    return x
```

```python
import functools

import jax
import jax.numpy as jnp
from jax import lax
from jax.experimental import pallas as pl
from jax.experimental.pallas import tpu as pltpu
from jax.experimental.pallas import tpu_sc as plsc

V7X_SUBLANES = 8
V7X_LANES = 128
V7X_VMEM_BYTES = 64 * 1024 * 1024

CONV_WIDTH = 4
LRU_C = 8.0
S5_GROUP = 16
S5_STATE = 64
TOP_K = 4
SWIGLU_LIMIT = 7.0
SWIGLU_ALPHA = 1.702
LN_EPS = 1e-5

S5_BLOCK_GROUPS = V7X_LANES // S5_GROUP
S5_BLOCK_STATES = S5_BLOCK_GROUPS * S5_STATE

MIXER_STEPS = 32
EXPERT_ROWS = 512
COMBINE_ROWS = 512
WEIGHT_CAST_ROWS = 64
SCAN_UNROLL = 4

_BF16 = jnp.bfloat16
_F32 = jnp.float32


def _dot(a, b):
    return jnp.dot(a, b, preferred_element_type=_F32)


def _vmem_limit(nbytes):
    return int(min(nbytes, V7X_VMEM_BYTES - 4 * 1024 * 1024))


def _layer_norm(z, gain, bias):
    mu = jnp.mean(z, axis=-1, keepdims=True)
    zc = z - mu
    var = jnp.mean(zc * zc, axis=-1, keepdims=True)
    return zc * lax.rsqrt(var + LN_EPS) * gain + bias


def _rows(v, steps):
    return jnp.tile(v, (steps, 1))


_HI_MASK = 0xFFFF0000


def _pack_rows(v):
    half = v.shape[1] // 2
    bits = lax.bitcast_convert_type(v.astype(_BF16).astype(_F32), jnp.uint32)
    packed = (bits[:, :half] >> 16) | (bits[:, half:] & jnp.uint32(_HI_MASK))
    return lax.bitcast_convert_type(packed, jnp.int32)


def _unpack_rows(w):
    bits = lax.bitcast_convert_type(w, jnp.uint32)
    lo = lax.bitcast_convert_type(bits << 16, _F32)
    hi = lax.bitcast_convert_type(bits & jnp.uint32(_HI_MASK), _F32)
    return lo, hi


def _ada_kernel(c_ref, w_ref, b_ref, o_ref):
    c = c_ref[...]
    c_act = (c * jax.nn.sigmoid(c)).astype(_BF16)
    o_ref[...] = _dot(c_act, w_ref[...].astype(_BF16)) + b_ref[...]


def _ada(c, w_ada, b_ada):
    batch, d = c.shape
    n_out = w_ada.shape[1]
    return pl.pallas_call(
        _ada_kernel,
        grid=(n_out // d,),
        in_specs=[
            pl.BlockSpec((batch, d), lambda j: (0, 0)),
            pl.BlockSpec((d, d), lambda j: (0, j)),
            pl.BlockSpec((1, d), lambda j: (0, j)),
        ],
        out_specs=pl.BlockSpec((batch, d), lambda j: (0, j)),
        out_shape=jax.ShapeDtypeStruct((batch, n_out), _F32),
        name="ada",
    )(c, w_ada, b_ada.reshape(1, n_out))


def _time_major(src_ref, slab_s, steps, batch):
    n_blk = slab_s.shape[0]
    for b in range(batch):
        for j in range(n_blk):
            slab_s[j, pl.ds(b, steps, stride=batch), :] = src_ref[b, :, j * V7X_LANES:(j + 1) * V7X_LANES]
    return jnp.concatenate([slab_s[j] for j in range(n_blk)], axis=1)


def _mixer_kernel(alpha, steps, batch, d, n_s5_blocks, n_experts, region_rows,
                  x_ref, xp_ref, mod_ref, w_in_ref, b_in_ref, conv_w_ref, conv_b_ref, wg_ref, bg_ref,
                  lamc_ref, w_rnn_ref, s5ar_ref, s5ai_ref, s5b_ref, s5c_ref, s5d_ref,
                  w_glu_ref, w_out_ref, ln_g_ref, ln_b_ref, w_r_ref, b_r_ref,
                  x1_ref, h2_ref, dest_ref, prob_ref, cnt_ref,
                  xt_s, xp_s, xc_s, a_s, u_s, bu_s, u5_s, ya_s, ga_s, gb_s, h_state, s5_state, cnt_s):
    m = steps * batch
    halo = (CONV_WIDTH - 1) * batch
    s5w = n_s5_blocks * V7X_LANES
    n_blk = d // V7X_LANES
    bs = S5_BLOCK_STATES
    step = pl.program_id(0)

    @pl.when(step == 0)
    def _():
        xc_s[0:halo, :] = jnp.zeros((halo, d), _F32)
        h_state[...] = jnp.zeros_like(h_state)
        s5_state[...] = jnp.zeros_like(s5_state)
        cnt_s[...] = jnp.zeros_like(cnt_s)
        for ref in (u_s, bu_s, u5_s, ya_s, ga_s, gb_s):
            ref[...] = jnp.zeros_like(ref)

    def mod(k):
        return mod_ref[:, k * d:(k + 1) * d]

    x = _time_major(x_ref, xt_s, steps, batch)
    hb = (x * _rows(1.0 + mod(1), steps) + _rows(mod(0), steps)).astype(_BF16)

    def in_proj(c0, width):
        return _dot(hb, w_in_ref[:, c0:c0 + width]) + b_in_ref[:, c0:c0 + width]

    c0 = 2 * d
    c1 = c0 + s5w
    branch_a = _dot((ya_s[...] * u_s[...]).astype(_BF16), w_rnn_ref[...])
    y5 = jnp.concatenate(
        [_dot(bu_s[:, 2 * bs * j:2 * bs * (j + 1)].astype(_BF16), s5c_ref[j]) for j in range(n_s5_blocks)],
        axis=1) + s5d_ref[...] * u5_s[...]
    xc_s[halo:halo + m, :] = in_proj(0, d)
    glu = _dot(jax.nn.gelu(y5).astype(_BF16), w_glu_ref[...])
    xr = jnp.zeros((m, d), _F32) + conv_b_ref[...]
    for k in range(CONV_WIDTH):
        xr = xr + conv_w_ref[k:k + 1, :] * xc_s[k * batch:k * batch + m, :]
    xc_s[0:halo, :] = xc_s[m:m + halo, :]
    xrb = xr.astype(_BF16)
    gates = [_dot(xrb[:, j * V7X_LANES:(j + 1) * V7X_LANES], wg_ref[j]) for j in range(n_blk)]
    merged = (ga_s[...] * branch_a + gb_s[...] * (glu[:, :d] * jax.nn.sigmoid(glu[:, d:]))).astype(_BF16)
    u5 = in_proj(c0, s5w)
    u5_s[...] = u5
    r_gate = jax.nn.sigmoid(jnp.concatenate([g[:, :V7X_LANES] for g in gates], axis=1) + bg_ref[:, 0:d])
    i_gate = jax.nn.sigmoid(jnp.concatenate([g[:, V7X_LANES:] for g in gates], axis=1) + bg_ref[:, d:2 * d])
    a = jnp.exp(lamc_ref[...] * r_gate)
    a_s[...] = a
    u_s[...] = jnp.sqrt(1.0 - a * a) * (i_gate * xr)
    mix = _dot(merged, w_out_ref[...])
    u5b = u5.astype(_BF16)
    for j in range(n_s5_blocks):
        bu_s[:, 2 * bs * j:2 * bs * (j + 1)] = _dot(u5b[:, j * V7X_LANES:(j + 1) * V7X_LANES], s5b_ref[j])
    x_prev = _time_major(xp_ref, xp_s, steps, batch)
    x1 = _layer_norm(alpha * x_prev + _rows(1.0 + mod(2), steps) * mix, ln_g_ref[...], ln_b_ref[...])
    x1_ref[...] = x1
    h2 = x1 * _rows(1.0 + mod(4), steps) + _rows(mod(3), steps)
    h2b = h2.astype(_BF16)
    h2_ref[...] = _pack_rows(h2)
    ya_s[...] = jax.nn.gelu(in_proj(d, d))
    lane = lax.broadcasted_iota(jnp.int32, (m, V7X_LANES), 1)
    lane_f = lane.astype(_F32)
    neg_inf = jnp.float32(-jnp.inf)
    logits = jnp.where(lane < n_experts, _dot(h2b, w_r_ref[...]) + b_r_ref[...], neg_inf)
    ga_s[...] = jax.nn.sigmoid(in_proj(c1, d))
    gb_s[...] = jax.nn.sigmoid(in_proj(c1 + d, d))
    onehot = jnp.zeros((m, V7X_LANES), _F32)
    picks, vals = [], []
    for _ in range(TOP_K):
        v = jnp.max(logits, axis=-1, keepdims=True)
        p = jnp.min(jnp.where(logits == v, lane_f, float(V7X_LANES)), axis=-1, keepdims=True)
        hit = lane_f == p
        onehot = jnp.where(hit, 1.0, onehot)
        logits = jnp.where(hit, neg_inf, logits)
        picks.append(p)
        vals.append(v)
    exps = [jnp.exp(v - vals[0]) for v in vals]
    inv_den = 1.0 / functools.reduce(lambda s, e: s + e, exps)
    row = lax.broadcasted_iota(jnp.int32, (m, m), 0)
    col = lax.broadcasted_iota(jnp.int32, (m, m), 1)
    earlier = jnp.where(col < row, 1.0, 0.0).astype(_BF16)
    before = _dot(earlier, onehot.astype(_BF16)) + cnt_s[0:1, :]
    prob_out = jnp.zeros((m, V7X_LANES), _F32)
    dest_out = jnp.zeros((m, V7X_LANES), _F32)
    for k in range(TOP_K):
        rank_k = jnp.sum(jnp.where(lane_f == picks[k], before, 0.0), axis=-1, keepdims=True)
        prob_out = jnp.where(lane == k, exps[k] * inv_den, prob_out)
        dest_out = jnp.where(lane == k, picks[k] * float(region_rows) + rank_k, dest_out)
    prob_ref[...] = prob_out
    dest_t = dest_out.T[0:V7X_SUBLANES, :].astype(jnp.int32)
    for j in range(m // V7X_LANES):
        dest_ref[j] = dest_t[:, j * V7X_LANES:(j + 1) * V7X_LANES]
    has_prev = jnp.where(step > 0, 1.0, 0.0)
    cnt_new = cnt_s[...] + has_prev * jnp.sum(onehot, axis=0, keepdims=True)
    cnt_s[...] = cnt_new
    cnt_ref[...] = cnt_new

    def scan_step(t, carry):
        hc, s5c = carry
        r0 = pl.multiple_of(t * batch, batch)
        hn = a_s[pl.ds(r0, batch), :] * hc + u_s[pl.ds(r0, batch), :]
        u_s[pl.ds(r0, batch), :] = hn
        new = []
        for j in range(n_s5_blocks):
            re0, im0 = 2 * bs * j, 2 * bs * j + bs
            re, im = s5c[j]
            ar, ai = s5ar_ref[j], s5ai_ref[j]
            nre = ar * re - ai * im + bu_s[pl.ds(r0, batch), re0:re0 + bs]
            nim = ar * im + ai * re + bu_s[pl.ds(r0, batch), im0:im0 + bs]
            bu_s[pl.ds(r0, batch), re0:re0 + bs] = nre
            bu_s[pl.ds(r0, batch), im0:im0 + bs] = nim
            new.append((nre, nim))
        return hn, tuple(new)

    s5_init = tuple((s5_state[:, 2 * bs * j:2 * bs * j + bs], s5_state[:, 2 * bs * j + bs:2 * bs * (j + 1)])
                    for j in range(n_s5_blocks))
    h_fin, s5_fin = lax.fori_loop(0, steps, scan_step, (h_state[...], s5_init), unroll=SCAN_UNROLL)
    h_state[...] = h_fin
    for j in range(n_s5_blocks):
        s5_state[:, 2 * bs * j:2 * bs * j + bs] = s5_fin[j][0]
        s5_state[:, 2 * bs * j + bs:2 * bs * (j + 1)] = s5_fin[j][1]


def _mixer(x, mod, p, *, alpha, n_experts):
    batch, seq, d = x.shape
    tokens = batch * seq
    steps = MIXER_STEPS
    m = steps * batch
    n_chunks = seq // steps
    n_s5_blocks = p["s5b"].shape[0]
    s5_lanes = n_s5_blocks * 2 * S5_BLOCK_STATES
    halo = (CONV_WIDTH - 1) * batch

    def const(a):
        nd = a.ndim
        return pl.BlockSpec(a.shape, lambda i, nd=nd: (0,) * nd, pipeline_mode=pl.Buffered(1))

    weights = [p["w_in"], p["b_in"], p["conv_w"], p["conv_b"], p["wg"], p["bg"], p["lamc"], p["w_rnn"],
               p["s5ar"], p["s5ai"], p["s5b"], p["s5c"], p["s5d"], p["w_glu"], p["w_out"],
               p["ln1_g"], p["ln1_b"], p["w_r"], p["b_r"]]
    prev = lambda i: jnp.maximum(i - 1, 0)
    row_spec = lambda width: pl.BlockSpec((m, width), lambda i: (prev(i), 0))
    chunks = m // V7X_LANES
    out_shape = (
        jax.ShapeDtypeStruct((tokens, d), _F32),
        jax.ShapeDtypeStruct((tokens, d // 2), jnp.int32),
        jax.ShapeDtypeStruct((tokens // V7X_LANES, V7X_SUBLANES, V7X_LANES), jnp.int32),
        jax.ShapeDtypeStruct((tokens, V7X_LANES), _F32),
        jax.ShapeDtypeStruct((V7X_SUBLANES, V7X_LANES), _F32),
    )
    slab = pltpu.VMEM((d // V7X_LANES, m, V7X_LANES), _F32)
    act = pltpu.VMEM((m, d), _F32)
    scratch = [
        slab, slab,
        pltpu.VMEM((m + halo, d), _F32),
        act, act,
        pltpu.VMEM((m, s5_lanes), _F32),
        pltpu.VMEM((m, n_s5_blocks * V7X_LANES), _F32),
        act, act, act,
        pltpu.VMEM((batch, d), _F32),
        pltpu.VMEM((batch, s5_lanes), _F32),
        pltpu.VMEM((V7X_SUBLANES, V7X_LANES), _F32),
    ]
    weight_bytes = sum(w.size * w.dtype.itemsize for w in weights)
    act_bytes = m * d * 4
    vmem = weight_bytes + 32 * act_bytes
    kern = functools.partial(_mixer_kernel, alpha, steps, batch, d, n_s5_blocks, n_experts, tokens)
    x_block = lambda index: pl.BlockSpec((batch, steps, d), index)
    return pl.pallas_call(
        kern,
        grid=(n_chunks + 1,),
        in_specs=[x_block(lambda i: (0, jnp.minimum(i, n_chunks - 1), 0)), x_block(lambda i: (0, prev(i), 0)),
                  const(mod)] + [const(w) for w in weights],
        out_specs=(row_spec(d), row_spec(d // 2),
                   pl.BlockSpec((chunks, V7X_SUBLANES, V7X_LANES), lambda i: (prev(i), 0, 0)),
                   row_spec(V7X_LANES),
                   pl.BlockSpec((V7X_SUBLANES, V7X_LANES), lambda i: (0, 0))),
        out_shape=out_shape,
        scratch_shapes=scratch,
        compiler_params=pltpu.CompilerParams(dimension_semantics=("arbitrary",),
                                             vmem_limit_bytes=_vmem_limit(vmem)),
        name="mixer",
    )(x, x, mod, *weights)


def _expert_kernel(d_ff, tile_e_ref, tile_blk_ref, n_used_ref,
                   x_ref, wgu_ref, bgu_ref, wd_ref, bd_ref, y_ref, wgu_s, wd_s):
    i = pl.program_id(0)
    d = wgu_ref.shape[1]

    @pl.when((i == 0) | (tile_e_ref[i] != tile_e_ref[jnp.maximum(i - 1, 0)]))
    def _():
        def cast(c, carry):
            r0 = pl.multiple_of(c * WEIGHT_CAST_ROWS, WEIGHT_CAST_ROWS)
            wgu_s[pl.ds(r0, WEIGHT_CAST_ROWS), :] = wgu_ref[0, pl.ds(r0, WEIGHT_CAST_ROWS), :].astype(_BF16)
            wd_s[pl.ds(r0, WEIGHT_CAST_ROWS), :] = wd_ref[0, pl.ds(r0, WEIGHT_CAST_ROWS), :].astype(_BF16)
            return carry

        lax.fori_loop(0, d // WEIGHT_CAST_ROWS, cast, 0)

    @pl.when(i < n_used_ref[0])
    def _():
        lo, hi = _unpack_rows(x_ref[...])
        x = jnp.concatenate([lo, hi], axis=1).astype(_BF16)
        gu = _dot(x, wgu_s[...]) + bgu_ref[0]
        gate = jnp.minimum(gu[:, :d_ff], SWIGLU_LIMIT)
        up = jnp.clip(gu[:, d_ff:], -SWIGLU_LIMIT, SWIGLU_LIMIT)
        act = gate * jax.nn.sigmoid(SWIGLU_ALPHA * gate) * (up + 1.0)
        y_ref[...] = _pack_rows(_dot(act.astype(_BF16), wd_s[...]) + bd_ref[0])

    @pl.when(i >= n_used_ref[0])
    def _():
        y_ref[...] = jnp.zeros_like(y_ref)


def _experts(xb, tile_e, tile_blk, n_used, w_gu, b_gu, w_down, b_down, *, n_tiles):
    n_rows, half = xb.shape
    n_experts, d, two_ff = w_gu.shape
    d_ff = two_ff // 2
    assert d_ff == d, "the weight cast loop walks w_gu and w_down rows together"
    r = EXPERT_ROWS
    vmem = 2 * (d * two_ff + d_ff * d) * 4 + (d * two_ff + d_ff * d) * 2 + 8 * r * half * 4 + 6 * r * two_ff * 4
    grid_spec = pltpu.PrefetchScalarGridSpec(
        num_scalar_prefetch=3,
        grid=(n_tiles,),
        in_specs=[
            pl.BlockSpec((r, half), lambda i, te, tb, nu: (tb[i], 0)),
            pl.BlockSpec((1, d, two_ff), lambda i, te, tb, nu: (te[i], 0, 0)),
            pl.BlockSpec((1, 1, two_ff), lambda i, te, tb, nu: (te[i], 0, 0)),
            pl.BlockSpec((1, d_ff, d), lambda i, te, tb, nu: (te[i], 0, 0)),
            pl.BlockSpec((1, 1, d), lambda i, te, tb, nu: (te[i], 0, 0)),
        ],
        out_specs=pl.BlockSpec((r, half), lambda i, te, tb, nu: (tb[i], 0)),
        scratch_shapes=[pltpu.VMEM((d, two_ff), _BF16), pltpu.VMEM((d_ff, d), _BF16)],
    )
    return pl.pallas_call(
        functools.partial(_expert_kernel, d_ff),
        grid_spec=grid_spec,
        out_shape=jax.ShapeDtypeStruct((n_rows, half), jnp.int32),
        compiler_params=pltpu.CompilerParams(dimension_semantics=("arbitrary",),
                                             vmem_limit_bytes=_vmem_limit(vmem)),
        name="experts",
    )(tile_e, tile_blk, n_used, xb, w_gu, b_gu.reshape(n_experts, 1, two_ff), w_down,
      b_down.reshape(n_experts, 1, d))


def _sc_workers():
    info = plsc.get_sparse_core_info()
    return info.num_cores, info.num_subcores


def _dispatch(h2w, dest_c, n_rows):
    tokens, width = h2w.shape
    n_chunks, _, chunk = dest_c.shape
    nc, ns = _sc_workers()
    per_w = n_chunks // (nc * ns)
    assert per_w * nc * ns == n_chunks

    @functools.partial(
        pl.kernel, mesh=plsc.VectorSubcoreMesh(core_axis_name="c", subcore_axis_name="s"),
        out_type=jax.ShapeDtypeStruct((n_rows, width), h2w.dtype),
        scratch_types=[pltpu.VMEM(dest_c.shape[1:], jnp.int32), pltpu.VMEM((chunk, width), h2w.dtype)],
    )
    def scatter_rows(h_hbm, d_hbm, o_hbm, idx_v, rows_v):
        wid = lax.axis_index("s") * nc + lax.axis_index("c")

        @pl.loop(0, per_w)
        def _(j):
            blk = wid * per_w + j
            pltpu.sync_copy(d_hbm.at[blk], idx_v)
            pltpu.sync_copy(h_hbm.at[pl.ds(pl.multiple_of(blk * chunk, chunk), chunk)], rows_v)
            for k in range(TOP_K):
                pltpu.sync_copy(rows_v, o_hbm.at[idx_v.at[k]])

    return scatter_rows(h2w, dest_c)


def _collect(yb, dest_c):
    _, width = yb.shape
    n_chunks, _, chunk = dest_c.shape
    nc, ns = _sc_workers()
    per_w = n_chunks // (nc * ns)
    assert per_w * nc * ns == n_chunks

    @functools.partial(
        pl.kernel, mesh=plsc.VectorSubcoreMesh(core_axis_name="c", subcore_axis_name="s"),
        out_type=jax.ShapeDtypeStruct((TOP_K, n_chunks * chunk, width), yb.dtype),
        scratch_types=[pltpu.VMEM(dest_c.shape[1:], jnp.int32), pltpu.VMEM((chunk, width), yb.dtype)],
    )
    def gather_rows(y_hbm, d_hbm, o_hbm, idx_v, rows_v):
        wid = lax.axis_index("s") * nc + lax.axis_index("c")

        @pl.loop(0, per_w)
        def _(j):
            blk = wid * per_w + j
            pltpu.sync_copy(d_hbm.at[blk], idx_v)
            for k in range(TOP_K):
                pltpu.sync_copy(y_hbm.at[idx_v.at[k]], rows_v)
                pltpu.sync_copy(rows_v, o_hbm.at[k, pl.ds(pl.multiple_of(blk * chunk, chunk), chunk)])

    return gather_rows(yb, dest_c)


def _combine_kernel(alpha, steps, x1_ref, yg_ref, prob_ref, mod_ref, ln_g_ref, ln_b_ref, o_ref, ot_s):
    d = x1_ref.shape[1]
    batch = o_ref.shape[0]
    ffn_lo = jnp.zeros((x1_ref.shape[0], d // 2), _F32)
    ffn_hi = jnp.zeros((x1_ref.shape[0], d // 2), _F32)
    for k in range(TOP_K):
        lo, hi = _unpack_rows(yg_ref[k])
        ffn_lo = ffn_lo + prob_ref[:, k:k + 1] * lo
        ffn_hi = ffn_hi + prob_ref[:, k:k + 1] * hi
    ffn = jnp.concatenate([ffn_lo, ffn_hi], axis=1)
    gate = _rows(1.0 + mod_ref[:, 5 * d:6 * d], steps)
    out = _layer_norm(alpha * x1_ref[...] + gate * ffn, ln_g_ref[...], ln_b_ref[...])
    n_blk = d // V7X_LANES
    for j in range(n_blk):
        ot_s[j] = out[:, j * V7X_LANES:(j + 1) * V7X_LANES]
    for b in range(batch):
        for j in range(n_blk):
            o_ref[b, :, j * V7X_LANES:(j + 1) * V7X_LANES] = ot_s[j, pl.ds(b, steps, stride=batch), :]


def _combine(x1, yg, prob, mod, ln_g, ln_b, *, alpha, batch):
    tokens, d = x1.shape
    rows = COMBINE_ROWS
    steps = rows // batch
    const = lambda a: pl.BlockSpec(a.shape, lambda i: (0, 0))
    return pl.pallas_call(
        functools.partial(_combine_kernel, alpha, steps),
        grid=(tokens // rows,),
        in_specs=[
            pl.BlockSpec((rows, d), lambda i: (i, 0)),
            pl.BlockSpec((TOP_K, rows, d // 2), lambda i: (0, i, 0)),
            pl.BlockSpec((rows, V7X_LANES), lambda i: (i, 0)),
            const(mod), const(ln_g), const(ln_b),
        ],
        out_specs=pl.BlockSpec((batch, steps, d), lambda i: (0, i, 0)),
        out_shape=jax.ShapeDtypeStruct((batch, tokens // batch, d), _F32),
        scratch_shapes=[pltpu.VMEM((d // V7X_LANES, rows, V7X_LANES), _F32)],
        compiler_params=pltpu.CompilerParams(dimension_semantics=("parallel",)),
        name="combine",
    )(x1, yg, prob, mod, ln_g, ln_b)


def _block_diag(blocks):
    n, a, b = blocks.shape
    eye = jnp.eye(n, dtype=blocks.dtype)
    return (eye[:, None, :, None] * blocks[:, :, None, :]).reshape(n * a, n * b)


def _s5_params(lam_re, lam_im, log_dt, b_re, b_im, c_re, c_im, d_skip):
    groups = lam_re.shape[0]
    nb = groups // S5_BLOCK_GROUPS
    dt = jnp.exp(log_dt)[:, None]
    mag = jnp.exp(lam_re * dt)
    ab_re, ab_im = mag * jnp.cos(lam_im * dt), mag * jnp.sin(lam_im * dt)
    den = lam_re * lam_re + lam_im * lam_im
    q_re = ((ab_re - 1.0) * lam_re + ab_im * lam_im) / den
    q_im = (ab_im * lam_re - (ab_re - 1.0) * lam_im) / den
    bb_re = q_re[..., None] * b_re - q_im[..., None] * b_im
    bb_im = q_re[..., None] * b_im + q_im[..., None] * b_re

    def per_block(a):
        return a.reshape(nb, S5_BLOCK_GROUPS, *a.shape[1:])

    bmat = jnp.stack([
        jnp.concatenate([_block_diag(jnp.swapaxes(per_block(bb_re)[j], 1, 2)),
                         _block_diag(jnp.swapaxes(per_block(bb_im)[j], 1, 2))], axis=1)
        for j in range(nb)])
    cmat = jnp.stack([
        jnp.concatenate([_block_diag(jnp.swapaxes(per_block(c_re)[j], 1, 2)),
                         -_block_diag(jnp.swapaxes(per_block(c_im)[j], 1, 2))], axis=0)
        for j in range(nb)])
    tile = lambda a: jnp.broadcast_to(a.reshape(nb, 1, S5_BLOCK_STATES), (nb, V7X_SUBLANES, S5_BLOCK_STATES))
    return dict(s5ar=tile(ab_re), s5ai=tile(ab_im), s5b=bmat.astype(_BF16), s5c=cmat.astype(_BF16),
                s5d=d_skip.reshape(1, -1))


def kernel(x, c, w_ada, b_ada, w_in, b_in, conv_w, conv_b, w_rg_a, b_rg_a, w_rg_x, b_rg_x, lru_lambda, w_rnn_out, s5_lambda_re, s5_lambda_im, s5_log_dt, s5_b_re, s5_b_im, s5_c_re, s5_c_im, s5_d, w_glu, w_out, ln1_g, ln1_b, w_router, b_router, w_gu, b_gu, w_down, b_down, ln2_g, ln2_b):
    batch, seq, d = x.shape
    depth = w_ada.shape[0]
    n_experts = w_router.shape[-1]
    tokens = batch * seq
    alpha = (2.0 * depth) ** 0.25
    assert batch == V7X_SUBLANES and d % V7X_LANES == 0 and n_experts <= V7X_LANES
    assert seq % MIXER_STEPS == 0 and tokens % COMBINE_ROWS == 0 and tokens % EXPERT_ROWS == 0
    assert (MIXER_STEPS * batch) % V7X_LANES == 0

    for l in range(depth):
        mod = _ada(c, w_ada[l], b_ada[l])
        row = lambda v: v.reshape(1, -1)
        p = dict(
            w_in=w_in[l].astype(_BF16), b_in=row(b_in[l]), conv_w=conv_w[l], conv_b=row(conv_b[l]),
            wg=jnp.concatenate([w_rg_a[l], w_rg_x[l]], axis=-1).astype(_BF16),
            bg=row(jnp.concatenate([b_rg_a[l], b_rg_x[l]])),
            lamc=row(-LRU_C * jax.nn.softplus(-lru_lambda[l])),
            w_rnn=w_rnn_out[l].astype(_BF16), w_glu=w_glu[l].astype(_BF16), w_out=w_out[l].astype(_BF16),
            ln1_g=row(ln1_g[l]), ln1_b=row(ln1_b[l]),
            w_r=jnp.pad(w_router[l], ((0, 0), (0, V7X_LANES - n_experts))).astype(_BF16),
            b_r=row(jnp.pad(b_router[l], (0, V7X_LANES - n_experts))),
            **_s5_params(s5_lambda_re[l], s5_lambda_im[l], s5_log_dt[l], s5_b_re[l], s5_b_im[l],
                         s5_c_re[l], s5_c_im[l], s5_d[l]),
        )
        x1, h2w, dest_c, prob, cnt = _mixer(x, mod, p, alpha=alpha, n_experts=n_experts)

        r = EXPERT_ROWS
        blocks_per_region = tokens // r
        spare_blk = n_experts * blocks_per_region
        n_tiles = -(-(tokens * TOP_K + n_experts * (r - 1)) // r)
        counts = cnt[0, :n_experts].astype(jnp.int32)
        tiles_e = (counts + r - 1) // r
        tile_end = jnp.cumsum(tiles_e)
        tile_start = tile_end - tiles_e
        n_used = tile_end[-1:]
        t_ids = jnp.arange(n_tiles, dtype=jnp.int32)
        tile_e = jnp.minimum(jnp.sum((tile_end[None, :] <= t_ids[:, None]).astype(jnp.int32), axis=1),
                             n_experts - 1)
        tile_blk = jnp.where(t_ids < n_used, tile_e * blocks_per_region + t_ids - tile_start[tile_e], spare_blk)

        xb = _dispatch(h2w, dest_c, (spare_blk + 1) * r)
        yb = _experts(xb, tile_e, tile_blk, n_used, w_gu[l], b_gu[l], w_down[l], b_down[l], n_tiles=n_tiles)
        yg = _collect(yb, dest_c)
        x = _combine(x1, yg, prob, mod, ln2_g[l].reshape(1, -1), ln2_b[l].reshape(1, -1),
                     alpha=alpha, batch=batch)
    return x
```

```python
import functools

import jax
import jax.numpy as jnp
from jax import lax
from jax.experimental import pallas as pl
from jax.experimental.pallas import tpu as pltpu
from jax.experimental.pallas import tpu_sc as plsc

V7X_SUBLANES = 8
V7X_LANES = 128
V7X_VMEM_BYTES = 64 * 1024 * 1024

CONV_WIDTH = 4
LRU_C = 8.0
S5_GROUP = 16
S5_STATE = 64
TOP_K = 4
SWIGLU_LIMIT = 7.0
SWIGLU_ALPHA = 1.702
LN_EPS = 1e-5

S5_BLOCK_GROUPS = V7X_LANES // S5_GROUP
S5_BLOCK_STATES = S5_BLOCK_GROUPS * S5_STATE

MIXER_STEPS = 32
EXPERT_ROWS = 512
COMBINE_ROWS = 512
WEIGHT_CAST_ROWS = 64
SCAN_UNROLL = 4

_BF16 = jnp.bfloat16
_F32 = jnp.float32


def _dot(a, b):
    return jnp.dot(a, b, preferred_element_type=_F32)


def _sigmoid(v):
    return 0.5 * jnp.tanh(0.5 * v) + 0.5


def _vmem_limit(nbytes):
    return int(min(nbytes, V7X_VMEM_BYTES - 4 * 1024 * 1024))


def _layer_norm(z, gain, bias):
    mu = jnp.mean(z, axis=-1, keepdims=True)
    zc = z - mu
    var = jnp.mean(zc * zc, axis=-1, keepdims=True)
    return zc * lax.rsqrt(var + LN_EPS) * gain + bias


def _rows(v, steps):
    return jnp.tile(v, (steps, 1))


_HI_MASK = 0xFFFF0000


def _pack_rows(v):
    half = v.shape[1] // 2
    bits = lax.bitcast_convert_type(v.astype(_BF16).astype(_F32), jnp.uint32)
    packed = (bits[:, :half] >> 16) | (bits[:, half:] & jnp.uint32(_HI_MASK))
    return lax.bitcast_convert_type(packed, jnp.int32)


def _unpack_rows(w):
    bits = lax.bitcast_convert_type(w, jnp.uint32)
    lo = lax.bitcast_convert_type(bits << 16, _F32)
    hi = lax.bitcast_convert_type(bits & jnp.uint32(_HI_MASK), _F32)
    return lo, hi


def _ada_kernel(c_ref, w_ref, b_ref, o_ref):
    c = c_ref[...]
    c_act = (c * _sigmoid(c)).astype(_BF16)
    o_ref[...] = _dot(c_act, w_ref[...].astype(_BF16)) + b_ref[...]


def _ada(c, w_ada, b_ada):
    batch, d = c.shape
    n_out = w_ada.shape[1]
    return pl.pallas_call(
        _ada_kernel,
        grid=(n_out // d,),
        in_specs=[
            pl.BlockSpec((batch, d), lambda j: (0, 0)),
            pl.BlockSpec((d, d), lambda j: (0, j)),
            pl.BlockSpec((1, d), lambda j: (0, j)),
        ],
        out_specs=pl.BlockSpec((batch, d), lambda j: (0, j)),
        out_shape=jax.ShapeDtypeStruct((batch, n_out), _F32),
        name="ada",
    )(c, w_ada, b_ada.reshape(1, n_out))


def _time_major(src_ref, slab_s, steps, batch):
    n_blk = slab_s.shape[0]
    for b in range(batch):
        for j in range(n_blk):
            slab_s[j, pl.ds(b, steps, stride=batch), :] = src_ref[b, :, j * V7X_LANES:(j + 1) * V7X_LANES]
    return jnp.concatenate([slab_s[j] for j in range(n_blk)], axis=1)


def _mixer_kernel(alpha, steps, batch, d, n_s5_blocks, n_experts, region_rows,
                  x_ref, xp_ref, mod_ref, w_in_ref, b_in_ref, conv_w_ref, conv_b_ref, wg_ref, bg_ref,
                  lamc_ref, w_rnn_ref, s5ar_ref, s5ai_ref, s5b_ref, s5c_ref, s5d_ref,
                  w_glu_ref, w_out_ref, ln_g_ref, ln_b_ref, w_r_ref, b_r_ref,
                  x1_ref, h2_ref, dest_ref, prob_ref, cnt_ref,
                  xt_s, xp_s, xc_s, a_s, u_s, bu_s, u5_s, ya_s, ga_s, gb_s, h_state, s5_state, cnt_s):
    m = steps * batch
    halo = (CONV_WIDTH - 1) * batch
    s5w = n_s5_blocks * V7X_LANES
    n_blk = d // V7X_LANES
    bs = S5_BLOCK_STATES
    step = pl.program_id(0)

    @pl.when(step == 0)
    def _():
        xc_s[0:halo, :] = jnp.zeros((halo, d), _F32)
        h_state[...] = jnp.zeros_like(h_state)
        s5_state[...] = jnp.zeros_like(s5_state)
        cnt_s[...] = jnp.zeros_like(cnt_s)
        for ref in (u_s, bu_s, u5_s, ya_s, ga_s, gb_s):
            ref[...] = jnp.zeros_like(ref)

    def mod(k):
        return mod_ref[:, k * d:(k + 1) * d]

    x = _time_major(x_ref, xt_s, steps, batch)
    hb = (x * _rows(1.0 + mod(1), steps) + _rows(mod(0), steps)).astype(_BF16)

    def in_proj(c0, width):
        return _dot(hb, w_in_ref[:, c0:c0 + width]) + b_in_ref[:, c0:c0 + width]

    c0 = 2 * d
    c1 = c0 + s5w
    branch_a = _dot((ya_s[...] * u_s[...]).astype(_BF16), w_rnn_ref[...])
    y5 = jnp.concatenate(
        [_dot(bu_s[:, 2 * bs * j:2 * bs * (j + 1)].astype(_BF16), s5c_ref[j]) for j in range(n_s5_blocks)],
        axis=1) + s5d_ref[...] * u5_s[...]
    xc_s[halo:halo + m, :] = in_proj(0, d)
    glu = _dot(jax.nn.gelu(y5).astype(_BF16), w_glu_ref[...])
    xr = jnp.zeros((m, d), _F32) + conv_b_ref[...]
    for k in range(CONV_WIDTH):
        xr = xr + conv_w_ref[k:k + 1, :] * xc_s[k * batch:k * batch + m, :]
    xc_s[0:halo, :] = xc_s[m:m + halo, :]
    xrb = xr.astype(_BF16)
    gates = [_dot(xrb[:, j * V7X_LANES:(j + 1) * V7X_LANES], wg_ref[j]) for j in range(n_blk)]
    merged = (ga_s[...] * branch_a + gb_s[...] * (glu[:, :d] * _sigmoid(glu[:, d:]))).astype(_BF16)
    u5 = in_proj(c0, s5w)
    u5_s[...] = u5
    r_gate = _sigmoid(jnp.concatenate([g[:, :V7X_LANES] for g in gates], axis=1) + bg_ref[:, 0:d])
    i_gate = _sigmoid(jnp.concatenate([g[:, V7X_LANES:] for g in gates], axis=1) + bg_ref[:, d:2 * d])
    a = jnp.exp(lamc_ref[...] * r_gate)
    a_s[...] = a
    z = 1.0 - a * a
    u_s[...] = jnp.where(z > 0.0, z * lax.rsqrt(z), 0.0) * (i_gate * xr)
    mix = _dot(merged, w_out_ref[...])
    u5b = u5.astype(_BF16)
    for j in range(n_s5_blocks):
        bu_s[:, 2 * bs * j:2 * bs * (j + 1)] = _dot(u5b[:, j * V7X_LANES:(j + 1) * V7X_LANES], s5b_ref[j])
    hc = h_state[...]
    s5c = [(s5_state[:, 2 * bs * j:2 * bs * j + bs], s5_state[:, 2 * bs * j + bs:2 * bs * (j + 1)])
           for j in range(n_s5_blocks)]
    for t in range(steps):
        r0 = t * batch
        hc = a_s[r0:r0 + batch, :] * hc + u_s[r0:r0 + batch, :]
        u_s[r0:r0 + batch, :] = hc
        for j in range(n_s5_blocks):
            re0, im0 = 2 * bs * j, 2 * bs * j + bs
            re, im = s5c[j]
            ar, ai = s5ar_ref[j], s5ai_ref[j]
            nre = ar * re - ai * im + bu_s[r0:r0 + batch, re0:re0 + bs]
            nim = ar * im + ai * re + bu_s[r0:r0 + batch, im0:im0 + bs]
            bu_s[r0:r0 + batch, re0:re0 + bs] = nre
            bu_s[r0:r0 + batch, im0:im0 + bs] = nim
            s5c[j] = (nre, nim)
    h_state[...] = hc
    for j in range(n_s5_blocks):
        s5_state[:, 2 * bs * j:2 * bs * j + bs] = s5c[j][0]
        s5_state[:, 2 * bs * j + bs:2 * bs * (j + 1)] = s5c[j][1]
    x_prev = _time_major(xp_ref, xp_s, steps, batch)
    x1 = _layer_norm(alpha * x_prev + _rows(1.0 + mod(2), steps) * mix, ln_g_ref[...], ln_b_ref[...])
    x1_ref[...] = x1
    h2 = x1 * _rows(1.0 + mod(4), steps) + _rows(mod(3), steps)
    h2b = h2.astype(_BF16)
    h2_ref[...] = _pack_rows(h2)
    ya_s[...] = jax.nn.gelu(in_proj(d, d))
    lane = lax.broadcasted_iota(jnp.int32, (m, V7X_LANES), 1)
    lane_f = lane.astype(_F32)
    neg_inf = jnp.float32(-jnp.inf)
    logits = jnp.where(lane < n_experts, _dot(h2b, w_r_ref[...]) + b_r_ref[...], neg_inf)
    ga_s[...] = _sigmoid(in_proj(c1, d))
    gb_s[...] = _sigmoid(in_proj(c1 + d, d))
    onehot = jnp.zeros((m, V7X_LANES), _F32)
    picks, vals = [], []
    for _ in range(TOP_K):
        v = jnp.max(logits, axis=-1, keepdims=True)
        p = jnp.min(jnp.where(logits == v, lane_f, float(V7X_LANES)), axis=-1, keepdims=True)
        hit = lane_f == p
        onehot = jnp.where(hit, 1.0, onehot)
        logits = jnp.where(hit, neg_inf, logits)
        picks.append(p)
        vals.append(v)
    exps = [jnp.exp(v - vals[0]) for v in vals]
    inv_den = 1.0 / functools.reduce(lambda s, e: s + e, exps)
    row = lax.broadcasted_iota(jnp.int32, (m, m), 0)
    col = lax.broadcasted_iota(jnp.int32, (m, m), 1)
    earlier = jnp.where(col < row, 1.0, 0.0).astype(_BF16)
    before = _dot(earlier, onehot.astype(_BF16)) + cnt_s[0:1, :]
    prob_out = jnp.zeros((m, V7X_LANES), _F32)
    dest_out = jnp.zeros((m, V7X_LANES), _F32)
    for k in range(TOP_K):
        rank_k = jnp.sum(jnp.where(lane_f == picks[k], before, 0.0), axis=-1, keepdims=True)
        prob_out = jnp.where(lane == k, exps[k] * inv_den, prob_out)
        dest_out = jnp.where(lane == k, picks[k] * float(region_rows) + rank_k, dest_out)
    prob_ref[...] = prob_out
    dest_t = dest_out.T[0:V7X_SUBLANES, :].astype(jnp.int32)
    for j in range(m // V7X_LANES):
        dest_ref[j] = dest_t[:, j * V7X_LANES:(j + 1) * V7X_LANES]
    has_prev = jnp.where(step > 0, 1.0, 0.0)
    cnt_new = cnt_s[...] + has_prev * jnp.sum(onehot, axis=0, keepdims=True)
    cnt_s[...] = cnt_new
    cnt_ref[...] = cnt_new


def _mixer(x, mod, p, *, alpha, n_experts):
    batch, seq, d = x.shape
    tokens = batch * seq
    steps = MIXER_STEPS
    m = steps * batch
    n_chunks = seq // steps
    n_s5_blocks = p["s5b"].shape[0]
    s5_lanes = n_s5_blocks * 2 * S5_BLOCK_STATES
    halo = (CONV_WIDTH - 1) * batch

    def const(a):
        nd = a.ndim
        return pl.BlockSpec(a.shape, lambda i, nd=nd: (0,) * nd, pipeline_mode=pl.Buffered(1))

    weights = [p["w_in"], p["b_in"], p["conv_w"], p["conv_b"], p["wg"], p["bg"], p["lamc"], p["w_rnn"],
               p["s5ar"], p["s5ai"], p["s5b"], p["s5c"], p["s5d"], p["w_glu"], p["w_out"],
               p["ln1_g"], p["ln1_b"], p["w_r"], p["b_r"]]
    prev = lambda i: jnp.maximum(i - 1, 0)
    row_spec = lambda width: pl.BlockSpec((m, width), lambda i: (prev(i), 0))
    chunks = m // V7X_LANES
    out_shape = (
        jax.ShapeDtypeStruct((tokens, d), _F32),
        jax.ShapeDtypeStruct((tokens, d // 2), jnp.int32),
        jax.ShapeDtypeStruct((tokens // V7X_LANES, V7X_SUBLANES, V7X_LANES), jnp.int32),
        jax.ShapeDtypeStruct((tokens, V7X_LANES), _F32),
        jax.ShapeDtypeStruct((V7X_SUBLANES, V7X_LANES), _F32),
    )
    slab = pltpu.VMEM((d // V7X_LANES, m, V7X_LANES), _F32)
    act = pltpu.VMEM((m, d), _F32)
    scratch = [
        slab, slab,
        pltpu.VMEM((m + halo, d), _F32),
        act, act,
        pltpu.VMEM((m, s5_lanes), _F32),
        pltpu.VMEM((m, n_s5_blocks * V7X_LANES), _F32),
        act, act, act,
        pltpu.VMEM((batch, d), _F32),
        pltpu.VMEM((batch, s5_lanes), _F32),
        pltpu.VMEM((V7X_SUBLANES, V7X_LANES), _F32),
    ]
    weight_bytes = sum(w.size * w.dtype.itemsize for w in weights)
    act_bytes = m * d * 4
    vmem = weight_bytes + 32 * act_bytes
    kern = functools.partial(_mixer_kernel, alpha, steps, batch, d, n_s5_blocks, n_experts, tokens)
    x_block = lambda index: pl.BlockSpec((batch, steps, d), index)
    return pl.pallas_call(
        kern,
        grid=(n_chunks + 1,),
        in_specs=[x_block(lambda i: (0, jnp.minimum(i, n_chunks - 1), 0)), x_block(lambda i: (0, prev(i), 0)),
                  const(mod)] + [const(w) for w in weights],
        out_specs=(row_spec(d), row_spec(d // 2),
                   pl.BlockSpec((chunks, V7X_SUBLANES, V7X_LANES), lambda i: (prev(i), 0, 0)),
                   row_spec(V7X_LANES),
                   pl.BlockSpec((V7X_SUBLANES, V7X_LANES), lambda i: (0, 0))),
        out_shape=out_shape,
        scratch_shapes=scratch,
        compiler_params=pltpu.CompilerParams(dimension_semantics=("arbitrary",),
                                             vmem_limit_bytes=_vmem_limit(vmem)),
        name="mixer",
    )(x, x, mod, *weights)


def _expert_kernel(d_ff, tile_e_ref, tile_blk_ref, n_used_ref,
                   x_ref, wgu_ref, bgu_ref, wd_ref, bd_ref, y_ref, wgu_s, wd_s):
    i = pl.program_id(0)
    d = wgu_ref.shape[1]

    @pl.when((i == 0) | (tile_e_ref[i] != tile_e_ref[jnp.maximum(i - 1, 0)]))
    def _():
        def cast(c, carry):
            r0 = pl.multiple_of(c * WEIGHT_CAST_ROWS, WEIGHT_CAST_ROWS)
            wgu_s[pl.ds(r0, WEIGHT_CAST_ROWS), :] = wgu_ref[0, pl.ds(r0, WEIGHT_CAST_ROWS), :].astype(_BF16)
            wd_s[pl.ds(r0, WEIGHT_CAST_ROWS), :] = wd_ref[0, pl.ds(r0, WEIGHT_CAST_ROWS), :].astype(_BF16)
            return carry

        lax.fori_loop(0, d // WEIGHT_CAST_ROWS, cast, 0)

    @pl.when(i < n_used_ref[0])
    def _():
        lo, hi = _unpack_rows(x_ref[...])
        x = jnp.concatenate([lo, hi], axis=1).astype(_BF16)
        gu = _dot(x, wgu_s[...]) + bgu_ref[0]
        gate = jnp.minimum(gu[:, :d_ff], SWIGLU_LIMIT)
        up = jnp.clip(gu[:, d_ff:], -SWIGLU_LIMIT, SWIGLU_LIMIT)
        act = gate * _sigmoid(SWIGLU_ALPHA * gate) * (up + 1.0)
        y_ref[...] = _pack_rows(_dot(act.astype(_BF16), wd_s[...]) + bd_ref[0])

    @pl.when(i >= n_used_ref[0])
    def _():
        y_ref[...] = jnp.zeros_like(y_ref)


def _experts(xb, tile_e, tile_blk, n_used, w_gu, b_gu, w_down, b_down, *, n_tiles):
    n_rows, half = xb.shape
    n_experts, d, two_ff = w_gu.shape
    d_ff = two_ff // 2
    assert d_ff == d, "the weight cast loop walks w_gu and w_down rows together"
    r = EXPERT_ROWS
    vmem = 2 * (d * two_ff + d_ff * d) * 4 + (d * two_ff + d_ff * d) * 2 + 8 * r * half * 4 + 6 * r * two_ff * 4
    grid_spec = pltpu.PrefetchScalarGridSpec(
        num_scalar_prefetch=3,
        grid=(n_tiles,),
        in_specs=[
            pl.BlockSpec((r, half), lambda i, te, tb, nu: (tb[i], 0)),
            pl.BlockSpec((1, d, two_ff), lambda i, te, tb, nu: (te[i], 0, 0)),
            pl.BlockSpec((1, 1, two_ff), lambda i, te, tb, nu: (te[i], 0, 0)),
            pl.BlockSpec((1, d_ff, d), lambda i, te, tb, nu: (te[i], 0, 0)),
            pl.BlockSpec((1, 1, d), lambda i, te, tb, nu: (te[i], 0, 0)),
        ],
        out_specs=pl.BlockSpec((r, half), lambda i, te, tb, nu: (tb[i], 0)),
        scratch_shapes=[pltpu.VMEM((d, two_ff), _BF16), pltpu.VMEM((d_ff, d), _BF16)],
    )
    return pl.pallas_call(
        functools.partial(_expert_kernel, d_ff),
        grid_spec=grid_spec,
        out_shape=jax.ShapeDtypeStruct((n_rows, half), jnp.int32),
        compiler_params=pltpu.CompilerParams(dimension_semantics=("arbitrary",),
                                             vmem_limit_bytes=_vmem_limit(vmem)),
        name="experts",
    )(tile_e, tile_blk, n_used, xb, w_gu, b_gu.reshape(n_experts, 1, two_ff), w_down,
      b_down.reshape(n_experts, 1, d))


def _sc_workers():
    info = plsc.get_sparse_core_info()
    return info.num_cores, info.num_subcores


def _dispatch(h2w, dest_c, n_rows):
    tokens, width = h2w.shape
    n_chunks, _, chunk = dest_c.shape
    nc, ns = _sc_workers()
    per_w = n_chunks // (nc * ns)
    assert per_w * nc * ns == n_chunks

    @functools.partial(
        pl.kernel, mesh=plsc.VectorSubcoreMesh(core_axis_name="c", subcore_axis_name="s"),
        out_type=jax.ShapeDtypeStruct((n_rows, width), h2w.dtype),
        scratch_types=[pltpu.VMEM(dest_c.shape[1:], jnp.int32), pltpu.VMEM((chunk, width), h2w.dtype)],
    )
    def scatter_rows(h_hbm, d_hbm, o_hbm, idx_v, rows_v):
        wid = lax.axis_index("s") * nc + lax.axis_index("c")

        @pl.loop(0, per_w)
        def _(j):
            blk = wid * per_w + j
            pltpu.sync_copy(d_hbm.at[blk], idx_v)
            pltpu.sync_copy(h_hbm.at[pl.ds(pl.multiple_of(blk * chunk, chunk), chunk)], rows_v)
            for k in range(TOP_K):
                pltpu.sync_copy(rows_v, o_hbm.at[idx_v.at[k]])

    return scatter_rows(h2w, dest_c)


def _collect(yb, dest_c):
    _, width = yb.shape
    n_chunks, _, chunk = dest_c.shape
    nc, ns = _sc_workers()
    per_w = n_chunks // (nc * ns)
    assert per_w * nc * ns == n_chunks

    @functools.partial(
        pl.kernel, mesh=plsc.VectorSubcoreMesh(core_axis_name="c", subcore_axis_name="s"),
        out_type=jax.ShapeDtypeStruct((TOP_K, n_chunks * chunk, width), yb.dtype),
        scratch_types=[pltpu.VMEM(dest_c.shape[1:], jnp.int32), pltpu.VMEM((chunk, width), yb.dtype)],
    )
    def gather_rows(y_hbm, d_hbm, o_hbm, idx_v, rows_v):
        wid = lax.axis_index("s") * nc + lax.axis_index("c")

        @pl.loop(0, per_w)
        def _(j):
            blk = wid * per_w + j
            pltpu.sync_copy(d_hbm.at[blk], idx_v)
            for k in range(TOP_K):
                pltpu.sync_copy(y_hbm.at[idx_v.at[k]], rows_v)
                pltpu.sync_copy(rows_v, o_hbm.at[k, pl.ds(pl.multiple_of(blk * chunk, chunk), chunk)])

    return gather_rows(yb, dest_c)


def _combine_kernel(alpha, steps, x1_ref, yg_ref, prob_ref, mod_ref, ln_g_ref, ln_b_ref, o_ref, ot_s):
    d = x1_ref.shape[1]
    batch = o_ref.shape[0]
    ffn_lo = jnp.zeros((x1_ref.shape[0], d // 2), _F32)
    ffn_hi = jnp.zeros((x1_ref.shape[0], d // 2), _F32)
    for k in range(TOP_K):
        lo, hi = _unpack_rows(yg_ref[k])
        ffn_lo = ffn_lo + prob_ref[:, k:k + 1] * lo
        ffn_hi = ffn_hi + prob_ref[:, k:k + 1] * hi
    ffn = jnp.concatenate([ffn_lo, ffn_hi], axis=1)
    gate = _rows(1.0 + mod_ref[:, 5 * d:6 * d], steps)
    out = _layer_norm(alpha * x1_ref[...] + gate * ffn, ln_g_ref[...], ln_b_ref[...])
    n_blk = d // V7X_LANES
    for j in range(n_blk):
        ot_s[j] = out[:, j * V7X_LANES:(j + 1) * V7X_LANES]
    for b in range(batch):
        for j in range(n_blk):
            o_ref[b, :, j * V7X_LANES:(j + 1) * V7X_LANES] = ot_s[j, pl.ds(b, steps, stride=batch), :]


def _combine(x1, yg, prob, mod, ln_g, ln_b, *, alpha, batch):
    tokens, d = x1.shape
    rows = COMBINE_ROWS
    steps = rows // batch
    const = lambda a: pl.BlockSpec(a.shape, lambda i: (0, 0))
    return pl.pallas_call(
        functools.partial(_combine_kernel, alpha, steps),
        grid=(tokens // rows,),
        in_specs=[
            pl.BlockSpec((rows, d), lambda i: (i, 0)),
            pl.BlockSpec((TOP_K, rows, d // 2), lambda i: (0, i, 0)),
            pl.BlockSpec((rows, V7X_LANES), lambda i: (i, 0)),
            const(mod), const(ln_g), const(ln_b),
        ],
        out_specs=pl.BlockSpec((batch, steps, d), lambda i: (0, i, 0)),
        out_shape=jax.ShapeDtypeStruct((batch, tokens // batch, d), _F32),
        scratch_shapes=[pltpu.VMEM((d // V7X_LANES, rows, V7X_LANES), _F32)],
        compiler_params=pltpu.CompilerParams(dimension_semantics=("parallel",)),
        name="combine",
    )(x1, yg, prob, mod, ln_g, ln_b)


def _block_diag(blocks):
    n, a, b = blocks.shape
    eye = jnp.eye(n, dtype=blocks.dtype)
    return (eye[:, None, :, None] * blocks[:, :, None, :]).reshape(n * a, n * b)


def _s5_params(lam_re, lam_im, log_dt, b_re, b_im, c_re, c_im, d_skip):
    groups = lam_re.shape[0]
    nb = groups // S5_BLOCK_GROUPS
    dt = jnp.exp(log_dt)[:, None]
    mag = jnp.exp(lam_re * dt)
    ab_re, ab_im = mag * jnp.cos(lam_im * dt), mag * jnp.sin(lam_im * dt)
    den = lam_re * lam_re + lam_im * lam_im
    q_re = ((ab_re - 1.0) * lam_re + ab_im * lam_im) / den
    q_im = (ab_im * lam_re - (ab_re - 1.0) * lam_im) / den
    bb_re = q_re[..., None] * b_re - q_im[..., None] * b_im
    bb_im = q_re[..., None] * b_im + q_im[..., None] * b_re

    def per_block(a):
        return a.reshape(nb, S5_BLOCK_GROUPS, *a.shape[1:])

    bmat = jnp.stack([
        jnp.concatenate([_block_diag(jnp.swapaxes(per_block(bb_re)[j], 1, 2)),
                         _block_diag(jnp.swapaxes(per_block(bb_im)[j], 1, 2))], axis=1)
        for j in range(nb)])
    cmat = jnp.stack([
        jnp.concatenate([_block_diag(jnp.swapaxes(per_block(c_re)[j], 1, 2)),
                         -_block_diag(jnp.swapaxes(per_block(c_im)[j], 1, 2))], axis=0)
        for j in range(nb)])
    tile = lambda a: jnp.broadcast_to(a.reshape(nb, 1, S5_BLOCK_STATES), (nb, V7X_SUBLANES, S5_BLOCK_STATES))
    return dict(s5ar=tile(ab_re), s5ai=tile(ab_im), s5b=bmat.astype(_BF16), s5c=cmat.astype(_BF16),
                s5d=d_skip.reshape(1, -1))


def kernel(x, c, w_ada, b_ada, w_in, b_in, conv_w, conv_b, w_rg_a, b_rg_a, w_rg_x, b_rg_x, lru_lambda, w_rnn_out, s5_lambda_re, s5_lambda_im, s5_log_dt, s5_b_re, s5_b_im, s5_c_re, s5_c_im, s5_d, w_glu, w_out, ln1_g, ln1_b, w_router, b_router, w_gu, b_gu, w_down, b_down, ln2_g, ln2_b):
    batch, seq, d = x.shape
    depth = w_ada.shape[0]
    n_experts = w_router.shape[-1]
    tokens = batch * seq
    alpha = (2.0 * depth) ** 0.25
    assert batch == V7X_SUBLANES and d % V7X_LANES == 0 and n_experts <= V7X_LANES
    assert seq % MIXER_STEPS == 0 and tokens % COMBINE_ROWS == 0 and tokens % EXPERT_ROWS == 0
    assert (MIXER_STEPS * batch) % V7X_LANES == 0

    for l in range(depth):
        mod = _ada(c, w_ada[l], b_ada[l])
        row = lambda v: v.reshape(1, -1)
        p = dict(
            w_in=w_in[l].astype(_BF16), b_in=row(b_in[l]), conv_w=conv_w[l], conv_b=row(conv_b[l]),
            wg=jnp.concatenate([w_rg_a[l], w_rg_x[l]], axis=-1).astype(_BF16),
            bg=row(jnp.concatenate([b_rg_a[l], b_rg_x[l]])),
            lamc=row(-LRU_C * jax.nn.softplus(-lru_lambda[l])),
            w_rnn=w_rnn_out[l].astype(_BF16), w_glu=w_glu[l].astype(_BF16), w_out=w_out[l].astype(_BF16),
            ln1_g=row(ln1_g[l]), ln1_b=row(ln1_b[l]),
            w_r=jnp.pad(w_router[l], ((0, 0), (0, V7X_LANES - n_experts))).astype(_BF16),
            b_r=row(jnp.pad(b_router[l], (0, V7X_LANES - n_experts))),
            **_s5_params(s5_lambda_re[l], s5_lambda_im[l], s5_log_dt[l], s5_b_re[l], s5_b_im[l],
                         s5_c_re[l], s5_c_im[l], s5_d[l]),
        )
        x1, h2w, dest_c, prob, cnt = _mixer(x, mod, p, alpha=alpha, n_experts=n_experts)

        r = EXPERT_ROWS
        blocks_per_region = tokens // r
        spare_blk = n_experts * blocks_per_region
        n_tiles = -(-(tokens * TOP_K + n_experts * (r - 1)) // r)
        counts = cnt[0, :n_experts].astype(jnp.int32)
        tiles_e = (counts + r - 1) // r
        tile_end = jnp.cumsum(tiles_e)
        tile_start = tile_end - tiles_e
        n_used = tile_end[-1:]
        t_ids = jnp.arange(n_tiles, dtype=jnp.int32)
        tile_e = jnp.minimum(jnp.sum((tile_end[None, :] <= t_ids[:, None]).astype(jnp.int32), axis=1),
                             n_experts - 1)
        tile_blk = jnp.where(t_ids < n_used, tile_e * blocks_per_region + t_ids - tile_start[tile_e], spare_blk)

        xb = _dispatch(h2w, dest_c, (spare_blk + 1) * r)
        yb = _experts(xb, tile_e, tile_blk, n_used, w_gu[l], b_gu[l], w_down[l], b_down[l], n_tiles=n_tiles)
        yg = _collect(yb, dest_c)
        x = _combine(x1, yg, prob, mod, ln2_g[l].reshape(1, -1), ln2_b[l].reshape(1, -1),
                     alpha=alpha, batch=batch)
    return x
```

```python
import functools

import jax
import jax.numpy as jnp
from jax import lax
from jax.experimental import pallas as pl
from jax.experimental.pallas import tpu as pltpu
from jax.experimental.pallas import tpu_sc as plsc

V7X_SUBLANES = 8
V7X_LANES = 128
V7X_VMEM_BYTES = 64 * 1024 * 1024

CONV_WIDTH = 4
LRU_C = 8.0
S5_GROUP = 16
S5_STATE = 64
TOP_K = 4
SWIGLU_LIMIT = 7.0
SWIGLU_ALPHA = 1.702
LN_EPS = 1e-5

S5_BLOCK_GROUPS = V7X_LANES // S5_GROUP
S5_BLOCK_STATES = S5_BLOCK_GROUPS * S5_STATE

(_VEC_B_X, _VEC_B_Y, _VEC_B_U5, _VEC_B_GA, _VEC_B_GB, _VEC_CONV_W) = range(6)
(_VEC_CONV_B, _VEC_B_RG_A, _VEC_B_RG_X, _VEC_LAMC, _VEC_LN_G, _VEC_LN_B, _VEC_S5_D, _VEC_B_ROUTER) = range(
    _VEC_CONV_W + CONV_WIDTH, _VEC_CONV_W + CONV_WIDTH + 8)

MIXER_STEPS = 32
EXPERT_ROWS = 512
COMBINE_ROWS = 1024
WEIGHT_CAST_ROWS = 64
SCAN_UNROLL = 4

_BF16 = jnp.bfloat16
_F32 = jnp.float32


def _dot(a, b):
    return jnp.dot(a, b, preferred_element_type=_F32)


def _sigmoid(v):
    return 0.5 * jnp.tanh(0.5 * v) + 0.5


def _vmem_limit(nbytes):
    return int(min(nbytes, V7X_VMEM_BYTES - 4 * 1024 * 1024))


def _layer_norm(z, gain, bias):
    mu = jnp.mean(z, axis=-1, keepdims=True)
    zc = z - mu
    var = jnp.mean(zc * zc, axis=-1, keepdims=True)
    return zc * lax.rsqrt(var + LN_EPS) * gain + bias


def _rows(v, steps):
    return jnp.tile(v, (steps, 1))


_HI_MASK = 0xFFFF0000


def _pack_rows(v):
    half = v.shape[1] // 2
    bits = lax.bitcast_convert_type(v.astype(_BF16).astype(_F32), jnp.uint32)
    packed = (bits[:, :half] >> 16) | (bits[:, half:] & jnp.uint32(_HI_MASK))
    return lax.bitcast_convert_type(packed, jnp.int32)


def _unpack_rows(w):
    bits = lax.bitcast_convert_type(w, jnp.uint32)
    lo = lax.bitcast_convert_type(bits << 16, _F32)
    hi = lax.bitcast_convert_type(bits & jnp.uint32(_HI_MASK), _F32)
    return lo, hi


def _ada_kernel(c_ref, w_ref, b_ref, o_ref):
    c = c_ref[...]
    c_act = (c * _sigmoid(c)).astype(_BF16)
    o_ref[...] = _dot(c_act, w_ref[...].astype(_BF16)) + b_ref[...]


def _ada(c, w_ada, b_ada):
    batch, d = c.shape
    n_out = w_ada.shape[1]
    return pl.pallas_call(
        _ada_kernel,
        grid=(n_out // d,),
        in_specs=[
            pl.BlockSpec((batch, d), lambda j: (0, 0)),
            pl.BlockSpec((d, d), lambda j: (0, j)),
            pl.BlockSpec((1, d), lambda j: (0, j)),
        ],
        out_specs=pl.BlockSpec((batch, d), lambda j: (0, j)),
        out_shape=jax.ShapeDtypeStruct((batch, n_out), _F32),
        name="ada",
    )(c, w_ada, b_ada.reshape(1, n_out))


def _time_major(src_ref, slab_s, steps, batch):
    n_blk = slab_s.shape[0]
    for b in range(batch):
        for j in range(n_blk):
            slab_s[j, pl.ds(b, steps, stride=batch), :] = src_ref[b, :, j * V7X_LANES:(j + 1) * V7X_LANES]
    return jnp.concatenate([slab_s[j] for j in range(n_blk)], axis=1)


def _mixer_kernel(alpha, steps, batch, d, n_s5_blocks, n_experts, region_rows,
                  x_ref, xp_ref, mod_ref, vecs_ref, w_in_ref, wg_ref, w_rnn_ref, s5a_ref, s5b_ref, s5c_ref,
                  w_glu_ref, w_out_ref, w_r_ref,
                  x1_ref, h2_ref, dest_ref, prob_ref, cnt_ref,
                  xt_s, xp_s, xc_s, a_s, u_s, bu_s, u5_s, ya_s, ga_s, gb_s, h_state, s5_state, cnt_s):
    m = steps * batch
    halo = (CONV_WIDTH - 1) * batch
    s5w = n_s5_blocks * V7X_LANES
    n_blk = d // V7X_LANES
    bs = S5_BLOCK_STATES
    step = pl.program_id(0)

    @pl.when(step == 0)
    def _():
        xc_s[0:halo, :] = jnp.zeros((halo, d), _F32)
        h_state[...] = jnp.zeros_like(h_state)
        s5_state[...] = jnp.zeros_like(s5_state)
        cnt_s[...] = jnp.zeros_like(cnt_s)
        for ref in (u_s, bu_s, u5_s, ya_s, ga_s, gb_s):
            ref[...] = jnp.zeros_like(ref)

    def mod(k):
        return mod_ref[:, k * d:(k + 1) * d]

    x = _time_major(x_ref, xt_s, steps, batch)
    hb = (x * _rows(1.0 + mod(1), steps) + _rows(mod(0), steps)).astype(_BF16)

    def vec(k, width=d):
        return vecs_ref[k:k + 1, 0:width]

    def in_proj(c0, width, bias_row):
        return _dot(hb, w_in_ref[:, c0:c0 + width]) + vec(bias_row, width)

    c0 = 2 * d
    c1 = c0 + s5w
    branch_a = _dot((ya_s[...] * u_s[...]).astype(_BF16), w_rnn_ref[...])
    y5 = jnp.concatenate(
        [_dot(bu_s[:, 2 * bs * j:2 * bs * (j + 1)].astype(_BF16), s5c_ref[j]) for j in range(n_s5_blocks)],
        axis=1) + vec(_VEC_S5_D, s5w) * u5_s[...]
    xc_s[halo:halo + m, :] = in_proj(0, d, _VEC_B_X)
    glu = _dot(jax.nn.gelu(y5).astype(_BF16), w_glu_ref[...])
    xr = jnp.zeros((m, d), _F32) + vec(_VEC_CONV_B)
    for k in range(CONV_WIDTH):
        xr = xr + vec(_VEC_CONV_W + k) * xc_s[k * batch:k * batch + m, :]
    xc_s[0:halo, :] = xc_s[m:m + halo, :]
    xrb = xr.astype(_BF16)
    gates = [_dot(xrb[:, j * V7X_LANES:(j + 1) * V7X_LANES], wg_ref[j]) for j in range(n_blk)]
    merged = (ga_s[...] * branch_a + gb_s[...] * (glu[:, :d] * _sigmoid(glu[:, d:]))).astype(_BF16)
    u5 = in_proj(c0, s5w, _VEC_B_U5)
    u5_s[...] = u5
    r_gate = _sigmoid(jnp.concatenate([g[:, :V7X_LANES] for g in gates], axis=1) + vec(_VEC_B_RG_A))
    i_gate = _sigmoid(jnp.concatenate([g[:, V7X_LANES:] for g in gates], axis=1) + vec(_VEC_B_RG_X))
    a = jnp.exp(vec(_VEC_LAMC) * r_gate)
    a_s[...] = a
    z = 1.0 - a * a
    u_s[...] = jnp.where(z > 0.0, z * lax.rsqrt(z), 0.0) * (i_gate * xr)
    mix = _dot(merged, w_out_ref[...])
    u5b = u5.astype(_BF16)
    for j in range(n_s5_blocks):
        bu_s[:, 2 * bs * j:2 * bs * (j + 1)] = _dot(u5b[:, j * V7X_LANES:(j + 1) * V7X_LANES], s5b_ref[j])
    hc = h_state[...]
    s5c = [(s5_state[:, 2 * bs * j:2 * bs * j + bs], s5_state[:, 2 * bs * j + bs:2 * bs * (j + 1)])
           for j in range(n_s5_blocks)]
    for t in range(steps):
        r0 = t * batch
        hc = a_s[r0:r0 + batch, :] * hc + u_s[r0:r0 + batch, :]
        u_s[r0:r0 + batch, :] = hc
        for j in range(n_s5_blocks):
            re0, im0 = 2 * bs * j, 2 * bs * j + bs
            re, im = s5c[j]
            ar, ai = s5a_ref[0, j], s5a_ref[1, j]
            nre = ar * re - ai * im + bu_s[r0:r0 + batch, re0:re0 + bs]
            nim = ar * im + ai * re + bu_s[r0:r0 + batch, im0:im0 + bs]
            bu_s[r0:r0 + batch, re0:re0 + bs] = nre
            bu_s[r0:r0 + batch, im0:im0 + bs] = nim
            s5c[j] = (nre, nim)
    h_state[...] = hc
    for j in range(n_s5_blocks):
        s5_state[:, 2 * bs * j:2 * bs * j + bs] = s5c[j][0]
        s5_state[:, 2 * bs * j + bs:2 * bs * (j + 1)] = s5c[j][1]
    x_prev = _time_major(xp_ref, xp_s, steps, batch)
    x1 = _layer_norm(alpha * x_prev + _rows(1.0 + mod(2), steps) * mix, vec(_VEC_LN_G), vec(_VEC_LN_B))
    x1_ref[...] = x1
    h2 = x1 * _rows(1.0 + mod(4), steps) + _rows(mod(3), steps)
    h2b = h2.astype(_BF16)
    h2_ref[...] = _pack_rows(h2)
    ya_s[...] = jax.nn.gelu(in_proj(d, d, _VEC_B_Y))
    lane = lax.broadcasted_iota(jnp.int32, (m, V7X_LANES), 1)
    lane_f = lane.astype(_F32)
    neg_inf = jnp.float32(-jnp.inf)
    logits = jnp.where(lane < n_experts, _dot(h2b, w_r_ref[...]) + vec(_VEC_B_ROUTER, V7X_LANES), neg_inf)
    ga_s[...] = _sigmoid(in_proj(c1, d, _VEC_B_GA))
    gb_s[...] = _sigmoid(in_proj(c1 + d, d, _VEC_B_GB))
    onehot = jnp.zeros((m, V7X_LANES), _F32)
    picks, vals = [], []
    for _ in range(TOP_K):
        v = jnp.max(logits, axis=-1, keepdims=True)
        p = jnp.min(jnp.where(logits == v, lane_f, float(V7X_LANES)), axis=-1, keepdims=True)
        hit = lane_f == p
        onehot = jnp.where(hit, 1.0, onehot)
        logits = jnp.where(hit, neg_inf, logits)
        picks.append(p)
        vals.append(v)
    exps = [jnp.exp(v - vals[0]) for v in vals]
    inv_den = 1.0 / functools.reduce(lambda s, e: s + e, exps)
    row = lax.broadcasted_iota(jnp.int32, (m, m), 0)
    col = lax.broadcasted_iota(jnp.int32, (m, m), 1)
    earlier = jnp.where(col < row, 1.0, 0.0).astype(_BF16)
    before = _dot(earlier, onehot.astype(_BF16)) + cnt_s[0:1, :]
    prob_out = jnp.zeros((m, V7X_LANES), _F32)
    dest_out = jnp.zeros((m, V7X_LANES), _F32)
    for k in range(TOP_K):
        rank_k = jnp.sum(jnp.where(lane_f == picks[k], before, 0.0), axis=-1, keepdims=True)
        prob_out = jnp.where(lane == k, exps[k] * inv_den, prob_out)
        dest_out = jnp.where(lane == k, picks[k] * float(region_rows) + rank_k, dest_out)
    prob_ref[...] = prob_out
    dest_t = dest_out.T[0:V7X_SUBLANES, :].astype(jnp.int32)
    for j in range(m // V7X_LANES):
        dest_ref[j] = dest_t[:, j * V7X_LANES:(j + 1) * V7X_LANES]
    has_prev = jnp.where(step > 0, 1.0, 0.0)
    cnt_new = cnt_s[...] + has_prev * jnp.sum(onehot, axis=0, keepdims=True)
    cnt_s[...] = cnt_new
    cnt_ref[...] = cnt_new


def _mixer(x, mod, p, *, alpha, n_experts):
    batch, seq, d = x.shape
    tokens = batch * seq
    steps = MIXER_STEPS
    m = steps * batch
    n_chunks = seq // steps
    n_s5_blocks = p["s5b"].shape[0]
    s5_lanes = n_s5_blocks * 2 * S5_BLOCK_STATES
    halo = (CONV_WIDTH - 1) * batch

    def const(a):
        nd = a.ndim
        return pl.BlockSpec(a.shape, lambda i, nd=nd: (0,) * nd, pipeline_mode=pl.Buffered(1))

    weights = [p["vecs"], p["w_in"], p["wg"], p["w_rnn"], p["s5a"], p["s5b"], p["s5c"], p["w_glu"], p["w_out"],
               p["w_r"]]
    prev = lambda i: jnp.maximum(i - 1, 0)
    row_spec = lambda width: pl.BlockSpec((m, width), lambda i: (prev(i), 0))
    chunks = m // V7X_LANES
    out_shape = (
        jax.ShapeDtypeStruct((tokens, d), _F32),
        jax.ShapeDtypeStruct((tokens, d // 2), jnp.int32),
        jax.ShapeDtypeStruct((tokens // V7X_LANES, V7X_SUBLANES, V7X_LANES), jnp.int32),
        jax.ShapeDtypeStruct((tokens, V7X_LANES), _F32),
        jax.ShapeDtypeStruct((V7X_SUBLANES, V7X_LANES), _F32),
    )
    slab = pltpu.VMEM((d // V7X_LANES, m, V7X_LANES), _F32)
    act = pltpu.VMEM((m, d), _F32)
    scratch = [
        slab, slab,
        pltpu.VMEM((m + halo, d), _F32),
        act, act,
        pltpu.VMEM((m, s5_lanes), _F32),
        pltpu.VMEM((m, n_s5_blocks * V7X_LANES), _F32),
        act, act, act,
        pltpu.VMEM((batch, d), _F32),
        pltpu.VMEM((batch, s5_lanes), _F32),
        pltpu.VMEM((V7X_SUBLANES, V7X_LANES), _F32),
    ]
    weight_bytes = sum(w.size * w.dtype.itemsize for w in weights)
    act_bytes = m * d * 4
    vmem = weight_bytes + 32 * act_bytes
    kern = functools.partial(_mixer_kernel, alpha, steps, batch, d, n_s5_blocks, n_experts, tokens)
    x_block = lambda index: pl.BlockSpec((batch, steps, d), index)
    return pl.pallas_call(
        kern,
        grid=(n_chunks + 1,),
        in_specs=[x_block(lambda i: (0, jnp.minimum(i, n_chunks - 1), 0)), x_block(lambda i: (0, prev(i), 0)),
                  const(mod)] + [const(w) for w in weights],
        out_specs=(row_spec(d), row_spec(d // 2),
                   pl.BlockSpec((chunks, V7X_SUBLANES, V7X_LANES), lambda i: (prev(i), 0, 0)),
                   row_spec(V7X_LANES),
                   pl.BlockSpec((V7X_SUBLANES, V7X_LANES), lambda i: (0, 0))),
        out_shape=out_shape,
        scratch_shapes=scratch,
        compiler_params=pltpu.CompilerParams(dimension_semantics=("arbitrary",),
                                             vmem_limit_bytes=_vmem_limit(vmem)),
        name="mixer",
    )(x, x, mod, *weights)


def _expert_kernel(d_ff, tile_e_ref, tile_blk_ref, n_used_ref,
                   x_ref, wgu_ref, bgu_ref, wd_ref, bd_ref, y_ref, wgu_s, wd_s):
    i = pl.program_id(0)
    d = wgu_ref.shape[1]

    @pl.when((i == 0) | (tile_e_ref[i] != tile_e_ref[jnp.maximum(i - 1, 0)]))
    def _():
        def cast(c, carry):
            r0 = pl.multiple_of(c * WEIGHT_CAST_ROWS, WEIGHT_CAST_ROWS)
            wgu_s[pl.ds(r0, WEIGHT_CAST_ROWS), :] = wgu_ref[0, pl.ds(r0, WEIGHT_CAST_ROWS), :].astype(_BF16)
            wd_s[pl.ds(r0, WEIGHT_CAST_ROWS), :] = wd_ref[0, pl.ds(r0, WEIGHT_CAST_ROWS), :].astype(_BF16)
            return carry

        lax.fori_loop(0, d // WEIGHT_CAST_ROWS, cast, 0)

    @pl.when(i < n_used_ref[0])
    def _():
        lo, hi = _unpack_rows(x_ref[...])
        x = jnp.concatenate([lo, hi], axis=1).astype(_BF16)
        gu = _dot(x, wgu_s[...]) + bgu_ref[0]
        gate = jnp.minimum(gu[:, :d_ff], SWIGLU_LIMIT)
        up = jnp.clip(gu[:, d_ff:], -SWIGLU_LIMIT, SWIGLU_LIMIT)
        act = gate * _sigmoid(SWIGLU_ALPHA * gate) * (up + 1.0)
        y_ref[...] = _pack_rows(_dot(act.astype(_BF16), wd_s[...]) + bd_ref[0])

    @pl.when(i >= n_used_ref[0])
    def _():
        y_ref[...] = jnp.zeros_like(y_ref)


def _experts(xb, tile_e, tile_blk, n_used, w_gu, b_gu, w_down, b_down, *, n_tiles):
    n_rows, half = xb.shape
    n_experts, d, two_ff = w_gu.shape
    d_ff = two_ff // 2
    assert d_ff == d, "the weight cast loop walks w_gu and w_down rows together"
    r = EXPERT_ROWS
    vmem = 2 * (d * two_ff + d_ff * d) * 4 + (d * two_ff + d_ff * d) * 2 + 8 * r * half * 4 + 6 * r * two_ff * 4
    grid_spec = pltpu.PrefetchScalarGridSpec(
        num_scalar_prefetch=3,
        grid=(n_tiles,),
        in_specs=[
            pl.BlockSpec((r, half), lambda i, te, tb, nu: (tb[i], 0)),
            pl.BlockSpec((1, d, two_ff), lambda i, te, tb, nu: (te[i], 0, 0)),
            pl.BlockSpec((1, 1, two_ff), lambda i, te, tb, nu: (te[i], 0, 0)),
            pl.BlockSpec((1, d_ff, d), lambda i, te, tb, nu: (te[i], 0, 0)),
            pl.BlockSpec((1, 1, d), lambda i, te, tb, nu: (te[i], 0, 0)),
        ],
        out_specs=pl.BlockSpec((r, half), lambda i, te, tb, nu: (tb[i], 0)),
        scratch_shapes=[pltpu.VMEM((d, two_ff), _BF16), pltpu.VMEM((d_ff, d), _BF16)],
    )
    return pl.pallas_call(
        functools.partial(_expert_kernel, d_ff),
        grid_spec=grid_spec,
        out_shape=jax.ShapeDtypeStruct((n_rows, half), jnp.int32),
        compiler_params=pltpu.CompilerParams(dimension_semantics=("arbitrary",),
                                             vmem_limit_bytes=_vmem_limit(vmem)),
        name="experts",
    )(tile_e, tile_blk, n_used, xb, w_gu, b_gu.reshape(n_experts, 1, two_ff), w_down,
      b_down.reshape(n_experts, 1, d))


def _sc_workers():
    info = plsc.get_sparse_core_info()
    return info.num_cores, info.num_subcores


def _dispatch(h2w, dest_c, n_rows):
    tokens, width = h2w.shape
    n_chunks, _, chunk = dest_c.shape
    nc, ns = _sc_workers()
    per_w = n_chunks // (nc * ns)
    assert per_w * nc * ns == n_chunks

    @functools.partial(
        pl.kernel, mesh=plsc.VectorSubcoreMesh(core_axis_name="c", subcore_axis_name="s"),
        out_type=jax.ShapeDtypeStruct((n_rows, width), h2w.dtype),
        scratch_types=[pltpu.VMEM(dest_c.shape[1:], jnp.int32), pltpu.VMEM((chunk, width), h2w.dtype)],
    )
    def scatter_rows(h_hbm, d_hbm, o_hbm, idx_v, rows_v):
        wid = lax.axis_index("s") * nc + lax.axis_index("c")

        @pl.loop(0, per_w)
        def _(j):
            blk = wid * per_w + j
            pltpu.sync_copy(d_hbm.at[blk], idx_v)
            pltpu.sync_copy(h_hbm.at[pl.ds(pl.multiple_of(blk * chunk, chunk), chunk)], rows_v)
            for k in range(TOP_K):
                pltpu.sync_copy(rows_v, o_hbm.at[idx_v.at[k]])

    return scatter_rows(h2w, dest_c)


def _collect(yb, dest_c):
    _, width = yb.shape
    n_chunks, _, chunk = dest_c.shape
    nc, ns = _sc_workers()
    per_w = n_chunks // (nc * ns)
    assert per_w * nc * ns == n_chunks

    @functools.partial(
        pl.kernel, mesh=plsc.VectorSubcoreMesh(core_axis_name="c", subcore_axis_name="s"),
        out_type=jax.ShapeDtypeStruct((TOP_K, n_chunks * chunk, width), yb.dtype),
        scratch_types=[pltpu.VMEM(dest_c.shape[1:], jnp.int32), pltpu.VMEM((chunk, width), yb.dtype)],
    )
    def gather_rows(y_hbm, d_hbm, o_hbm, idx_v, rows_v):
        wid = lax.axis_index("s") * nc + lax.axis_index("c")

        @pl.loop(0, per_w)
        def _(j):
            blk = wid * per_w + j
            pltpu.sync_copy(d_hbm.at[blk], idx_v)
            for k in range(TOP_K):
                pltpu.sync_copy(y_hbm.at[idx_v.at[k]], rows_v)
                pltpu.sync_copy(rows_v, o_hbm.at[k, pl.ds(pl.multiple_of(blk * chunk, chunk), chunk)])

    return gather_rows(yb, dest_c)


def _combine_kernel(alpha, steps, x1_ref, yg_ref, prob_ref, mod_ref, ln_g_ref, ln_b_ref, o_ref, ot_s):
    d = x1_ref.shape[1]
    batch = o_ref.shape[0]
    ffn_lo = jnp.zeros((x1_ref.shape[0], d // 2), _F32)
    ffn_hi = jnp.zeros((x1_ref.shape[0], d // 2), _F32)
    for k in range(TOP_K):
        lo, hi = _unpack_rows(yg_ref[k])
        ffn_lo = ffn_lo + prob_ref[:, k:k + 1] * lo
        ffn_hi = ffn_hi + prob_ref[:, k:k + 1] * hi
    ffn = jnp.concatenate([ffn_lo, ffn_hi], axis=1)
    gate = _rows(1.0 + mod_ref[:, 5 * d:6 * d], steps)
    out = _layer_norm(alpha * x1_ref[...] + gate * ffn, ln_g_ref[...], ln_b_ref[...])
    n_blk = d // V7X_LANES
    for j in range(n_blk):
        ot_s[j] = out[:, j * V7X_LANES:(j + 1) * V7X_LANES]
    for b in range(batch):
        for j in range(n_blk):
            o_ref[b, :, j * V7X_LANES:(j + 1) * V7X_LANES] = ot_s[j, pl.ds(b, steps, stride=batch), :]


def _combine(x1, yg, prob, mod, ln_g, ln_b, *, alpha, batch):
    tokens, d = x1.shape
    rows = COMBINE_ROWS
    steps = rows // batch
    const = lambda a: pl.BlockSpec(a.shape, lambda i: (0, 0))
    return pl.pallas_call(
        functools.partial(_combine_kernel, alpha, steps),
        grid=(tokens // rows,),
        in_specs=[
            pl.BlockSpec((rows, d), lambda i: (i, 0)),
            pl.BlockSpec((TOP_K, rows, d // 2), lambda i: (0, i, 0)),
            pl.BlockSpec((rows, V7X_LANES), lambda i: (i, 0)),
            const(mod), const(ln_g), const(ln_b),
        ],
        out_specs=pl.BlockSpec((batch, steps, d), lambda i: (0, i, 0)),
        out_shape=jax.ShapeDtypeStruct((batch, tokens // batch, d), _F32),
        scratch_shapes=[pltpu.VMEM((d // V7X_LANES, rows, V7X_LANES), _F32)],
        compiler_params=pltpu.CompilerParams(dimension_semantics=("parallel",)),
        name="combine",
    )(x1, yg, prob, mod, ln_g, ln_b)


def _block_diag(blocks):
    nb, n, a, b = blocks.shape
    eye = jnp.eye(n, dtype=blocks.dtype)
    return (eye[None, :, None, :, None] * blocks[:, :, :, None, :]).reshape(nb, n * a, n * b)


def _s5_params(lam_re, lam_im, log_dt, b_re, b_im, c_re, c_im):
    groups = lam_re.shape[0]
    nb = groups // S5_BLOCK_GROUPS
    dt = jnp.exp(log_dt)[:, None]
    mag = jnp.exp(lam_re * dt)
    ab_re, ab_im = mag * jnp.cos(lam_im * dt), mag * jnp.sin(lam_im * dt)
    den = lam_re * lam_re + lam_im * lam_im
    q_re = ((ab_re - 1.0) * lam_re + ab_im * lam_im) / den
    q_im = (ab_im * lam_re - (ab_re - 1.0) * lam_im) / den
    bb_re = q_re[..., None] * b_re - q_im[..., None] * b_im
    bb_im = q_re[..., None] * b_im + q_im[..., None] * b_re

    def per_block(a):
        return jnp.swapaxes(a.reshape(nb, S5_BLOCK_GROUPS, *a.shape[1:]), 2, 3)

    bmat = jnp.concatenate([_block_diag(per_block(bb_re)), _block_diag(per_block(bb_im))], axis=2)
    cmat = jnp.concatenate([_block_diag(per_block(c_re)), -_block_diag(per_block(c_im))], axis=1)
    s5a = jnp.broadcast_to(jnp.stack([ab_re, ab_im]).reshape(2, nb, 1, S5_BLOCK_STATES),
                           (2, nb, V7X_SUBLANES, S5_BLOCK_STATES))
    return dict(s5a=s5a, s5b=bmat.astype(_BF16), s5c=cmat.astype(_BF16))


def _mixer_vectors(d, b_in, conv_w, conv_b, b_rg_a, b_rg_x, lru_lambda, ln_g, ln_b, s5_d, b_router):
    pad = lambda v: jnp.pad(v, (0, d - v.shape[0]))
    s5w = s5_d.size
    rows = [b_in[0:d], b_in[d:2 * d], pad(b_in[2 * d:2 * d + s5w]), b_in[2 * d + s5w:3 * d + s5w],
            b_in[3 * d + s5w:4 * d + s5w], *conv_w, conv_b, b_rg_a, b_rg_x,
            -LRU_C * jax.nn.softplus(-lru_lambda), ln_g, ln_b, pad(s5_d.reshape(-1)), pad(b_router)]
    rows += [jnp.zeros((d,), _F32)] * (-len(rows) % V7X_SUBLANES)
    return jnp.stack(rows)


def kernel(x, c, w_ada, b_ada, w_in, b_in, conv_w, conv_b, w_rg_a, b_rg_a, w_rg_x, b_rg_x, lru_lambda, w_rnn_out, s5_lambda_re, s5_lambda_im, s5_log_dt, s5_b_re, s5_b_im, s5_c_re, s5_c_im, s5_d, w_glu, w_out, ln1_g, ln1_b, w_router, b_router, w_gu, b_gu, w_down, b_down, ln2_g, ln2_b):
    batch, seq, d = x.shape
    depth = w_ada.shape[0]
    n_experts = w_router.shape[-1]
    tokens = batch * seq
    alpha = (2.0 * depth) ** 0.25
    assert batch == V7X_SUBLANES and d % V7X_LANES == 0 and n_experts <= V7X_LANES
    assert seq % MIXER_STEPS == 0 and tokens % COMBINE_ROWS == 0 and tokens % EXPERT_ROWS == 0
    assert (MIXER_STEPS * batch) % V7X_LANES == 0

    for l in range(depth):
        mod = _ada(c, w_ada[l], b_ada[l])
        p = dict(
            vecs=_mixer_vectors(d, b_in[l], conv_w[l], conv_b[l], b_rg_a[l], b_rg_x[l], lru_lambda[l],
                                ln1_g[l], ln1_b[l], s5_d[l], b_router[l]),
            w_in=w_in[l].astype(_BF16),
            wg=jnp.concatenate([w_rg_a[l], w_rg_x[l]], axis=-1).astype(_BF16),
            w_rnn=w_rnn_out[l].astype(_BF16), w_glu=w_glu[l].astype(_BF16), w_out=w_out[l].astype(_BF16),
            w_r=jnp.pad(w_router[l], ((0, 0), (0, V7X_LANES - n_experts))).astype(_BF16),
            **_s5_params(s5_lambda_re[l], s5_lambda_im[l], s5_log_dt[l], s5_b_re[l], s5_b_im[l],
                         s5_c_re[l], s5_c_im[l]),
        )
        x1, h2w, dest_c, prob, cnt = _mixer(x, mod, p, alpha=alpha, n_experts=n_experts)

        r = EXPERT_ROWS
        blocks_per_region = tokens // r
        spare_blk = n_experts * blocks_per_region
        n_tiles = -(-(tokens * TOP_K + n_experts * (r - 1)) // r)
        counts = cnt[0, :n_experts].astype(jnp.int32)
        tiles_e = (counts + r - 1) // r
        tile_end = jnp.cumsum(tiles_e)
        tile_start = tile_end - tiles_e
        n_used = tile_end[-1:]
        t_ids = jnp.arange(n_tiles, dtype=jnp.int32)
        tile_e = jnp.minimum(jnp.sum((tile_end[None, :] <= t_ids[:, None]).astype(jnp.int32), axis=1),
                             n_experts - 1)
        tile_blk = jnp.where(t_ids < n_used, tile_e * blocks_per_region + t_ids - tile_start[tile_e], spare_blk)

        xb = _dispatch(h2w, dest_c, (spare_blk + 1) * r)
        yb = _experts(xb, tile_e, tile_blk, n_used, w_gu[l], b_gu[l], w_down[l], b_down[l], n_tiles=n_tiles)
        yg = _collect(yb, dest_c)
        x = _combine(x1, yg, prob, mod, ln2_g[l].reshape(1, -1), ln2_b[l].reshape(1, -1),
                     alpha=alpha, batch=batch)
    return x
```

```python
import functools

import jax
import jax.numpy as jnp
from jax import lax
from jax.experimental import pallas as pl
from jax.experimental.pallas import tpu as pltpu
from jax.experimental.pallas import tpu_sc as plsc

V7X_SUBLANES = 8
V7X_LANES = 128
V7X_VMEM_BYTES = 64 * 1024 * 1024

CONV_WIDTH = 4
LRU_C = 8.0
S5_GROUP = 16
S5_STATE = 64
TOP_K = 4
SWIGLU_LIMIT = 7.0
SWIGLU_ALPHA = 1.702
LN_EPS = 1e-5

S5_BLOCK_GROUPS = V7X_LANES // S5_GROUP
S5_BLOCK_STATES = S5_BLOCK_GROUPS * S5_STATE

(_VEC_B_X, _VEC_B_Y, _VEC_B_U5, _VEC_B_GA, _VEC_B_GB, _VEC_CONV_W) = range(6)
(_VEC_CONV_B, _VEC_B_RG_A, _VEC_B_RG_X, _VEC_LAMC, _VEC_LN_G, _VEC_LN_B, _VEC_S5_D, _VEC_B_ROUTER) = range(
    _VEC_CONV_W + CONV_WIDTH, _VEC_CONV_W + CONV_WIDTH + 8)

MIXER_STEPS = 32
EXPERT_ROWS = 512
COMBINE_ROWS = 1024
WEIGHT_CAST_ROWS = 64
SCAN_UNROLL = 4

_BF16 = jnp.bfloat16
_F32 = jnp.float32


def _dot(a, b):
    return jnp.dot(a, b, preferred_element_type=_F32)


def _sigmoid(v):
    return 0.5 * jnp.tanh(0.5 * v) + 0.5


def _vmem_limit(nbytes):
    return int(min(nbytes, V7X_VMEM_BYTES - 4 * 1024 * 1024))


def _layer_norm(z, gain, bias):
    mu = jnp.mean(z, axis=-1, keepdims=True)
    zc = z - mu
    var = jnp.mean(zc * zc, axis=-1, keepdims=True)
    return zc * lax.rsqrt(var + LN_EPS) * gain + bias


def _rows(v, steps):
    return jnp.tile(v, (steps, 1))


_HI_MASK = 0xFFFF0000


def _pack_rows(v):
    half = v.shape[1] // 2
    bits = lax.bitcast_convert_type(v.astype(_BF16).astype(_F32), jnp.uint32)
    packed = (bits[:, :half] >> 16) | (bits[:, half:] & jnp.uint32(_HI_MASK))
    return lax.bitcast_convert_type(packed, jnp.int32)


def _unpack_rows(w):
    bits = lax.bitcast_convert_type(w, jnp.uint32)
    lo = lax.bitcast_convert_type(bits << 16, _F32)
    hi = lax.bitcast_convert_type(bits & jnp.uint32(_HI_MASK), _F32)
    return lo, hi


def _ada_kernel(c_ref, w_ref, b_ref, o_ref):
    c = c_ref[...]
    c_act = (c * _sigmoid(c)).astype(_BF16)
    o_ref[...] = _dot(c_act, w_ref[...].astype(_BF16)) + b_ref[...]


def _ada(c, w_ada, b_ada):
    batch, d = c.shape
    n_out = w_ada.shape[1]
    return pl.pallas_call(
        _ada_kernel,
        grid=(n_out // d,),
        in_specs=[
            pl.BlockSpec((batch, d), lambda j: (0, 0)),
            pl.BlockSpec((d, d), lambda j: (0, j)),
            pl.BlockSpec((1, d), lambda j: (0, j)),
        ],
        out_specs=pl.BlockSpec((batch, d), lambda j: (0, j)),
        out_shape=jax.ShapeDtypeStruct((batch, n_out), _F32),
        name="ada",
    )(c, w_ada, b_ada.reshape(1, n_out))


def _mixer_kernel(alpha, steps, batch, d, n_s5_blocks, n_experts, region_rows,
                  x_ref, mod_ref, vecs_ref, w_in_ref, wg_ref, w_rnn_ref, s5a_ref, s5b_ref, s5c_ref,
                  w_glu_ref, w_out_ref, w_r_ref,
                  x1_ref, h2_ref, dest_ref, prob_ref, cnt_ref,
                  xt_s, xc_s, a_s, u_s, bu_s, u5_s, ya_s, ga_s, gb_s, h_state, s5_state, cnt_s):
    m = steps * batch
    halo = (CONV_WIDTH - 1) * batch
    s5w = n_s5_blocks * V7X_LANES
    n_blk = d // V7X_LANES
    bs = S5_BLOCK_STATES
    step = pl.program_id(0)

    @pl.when(step == 0)
    def _():
        xc_s[0:halo, :] = jnp.zeros((halo, d), _F32)
        h_state[...] = jnp.zeros_like(h_state)
        s5_state[...] = jnp.zeros_like(s5_state)
        cnt_s[...] = jnp.zeros_like(cnt_s)
        for ref in (xt_s, u_s, bu_s, u5_s, ya_s, ga_s, gb_s):
            ref[...] = jnp.zeros_like(ref)

    def mod(k):
        return mod_ref[:, k * d:(k + 1) * d]

    slot = lax.rem(step, 2)
    for b in range(batch):
        for j in range(n_blk):
            xt_s[slot, j, pl.ds(b, steps, stride=batch), :] = x_ref[b, :, j * V7X_LANES:(j + 1) * V7X_LANES]
    x = jnp.concatenate([xt_s[slot, j] for j in range(n_blk)], axis=1)
    hb = (x * _rows(1.0 + mod(1), steps) + _rows(mod(0), steps)).astype(_BF16)

    def vec(k, width=d):
        return vecs_ref[k:k + 1, 0:width]

    def in_proj(c0, width, bias_row):
        return _dot(hb, w_in_ref[:, c0:c0 + width]) + vec(bias_row, width)

    c0 = 2 * d
    c1 = c0 + s5w
    branch_a = _dot((ya_s[...] * u_s[...]).astype(_BF16), w_rnn_ref[...])
    y5 = jnp.concatenate(
        [_dot(bu_s[:, 2 * bs * j:2 * bs * (j + 1)].astype(_BF16), s5c_ref[j]) for j in range(n_s5_blocks)],
        axis=1) + vec(_VEC_S5_D, s5w) * u5_s[...]
    xc_s[halo:halo + m, :] = in_proj(0, d, _VEC_B_X)
    glu = _dot(jax.nn.gelu(y5).astype(_BF16), w_glu_ref[...])
    xr = jnp.zeros((m, d), _F32) + vec(_VEC_CONV_B)
    for k in range(CONV_WIDTH):
        xr = xr + vec(_VEC_CONV_W + k) * xc_s[k * batch:k * batch + m, :]
    xc_s[0:halo, :] = xc_s[m:m + halo, :]
    xrb = xr.astype(_BF16)
    gates = [_dot(xrb[:, j * V7X_LANES:(j + 1) * V7X_LANES], wg_ref[j]) for j in range(n_blk)]
    merged = (ga_s[...] * branch_a + gb_s[...] * (glu[:, :d] * _sigmoid(glu[:, d:]))).astype(_BF16)
    u5 = in_proj(c0, s5w, _VEC_B_U5)
    u5_s[...] = u5
    r_gate = _sigmoid(jnp.concatenate([g[:, :V7X_LANES] for g in gates], axis=1) + vec(_VEC_B_RG_A))
    i_gate = _sigmoid(jnp.concatenate([g[:, V7X_LANES:] for g in gates], axis=1) + vec(_VEC_B_RG_X))
    a = jnp.exp(vec(_VEC_LAMC) * r_gate)
    a_s[...] = a
    z = 1.0 - a * a
    u_s[...] = jnp.where(z > 0.0, z * lax.rsqrt(z), 0.0) * (i_gate * xr)
    mix = _dot(merged, w_out_ref[...])
    u5b = u5.astype(_BF16)
    for j in range(n_s5_blocks):
        bu_s[:, 2 * bs * j:2 * bs * (j + 1)] = _dot(u5b[:, j * V7X_LANES:(j + 1) * V7X_LANES], s5b_ref[j])
    hc = h_state[...]
    s5c = [(s5_state[:, 2 * bs * j:2 * bs * j + bs], s5_state[:, 2 * bs * j + bs:2 * bs * (j + 1)])
           for j in range(n_s5_blocks)]
    for t in range(steps):
        r0 = t * batch
        hc = a_s[r0:r0 + batch, :] * hc + u_s[r0:r0 + batch, :]
        u_s[r0:r0 + batch, :] = hc
        for j in range(n_s5_blocks):
            re0, im0 = 2 * bs * j, 2 * bs * j + bs
            re, im = s5c[j]
            ar, ai = s5a_ref[0, j], s5a_ref[1, j]
            nre = ar * re - ai * im + bu_s[r0:r0 + batch, re0:re0 + bs]
            nim = ar * im + ai * re + bu_s[r0:r0 + batch, im0:im0 + bs]
            bu_s[r0:r0 + batch, re0:re0 + bs] = nre
            bu_s[r0:r0 + batch, im0:im0 + bs] = nim
            s5c[j] = (nre, nim)
    h_state[...] = hc
    for j in range(n_s5_blocks):
        s5_state[:, 2 * bs * j:2 * bs * j + bs] = s5c[j][0]
        s5_state[:, 2 * bs * j + bs:2 * bs * (j + 1)] = s5c[j][1]
    x_prev = jnp.concatenate([xt_s[1 - slot, j] for j in range(n_blk)], axis=1)
    x1 = _layer_norm(alpha * x_prev + _rows(1.0 + mod(2), steps) * mix, vec(_VEC_LN_G), vec(_VEC_LN_B))
    x1_ref[...] = x1
    h2 = x1 * _rows(1.0 + mod(4), steps) + _rows(mod(3), steps)
    h2b = h2.astype(_BF16)
    h2_ref[...] = _pack_rows(h2)
    ya_s[...] = jax.nn.gelu(in_proj(d, d, _VEC_B_Y))
    lane = lax.broadcasted_iota(jnp.int32, (m, V7X_LANES), 1)
    lane_f = lane.astype(_F32)
    neg_inf = jnp.float32(-jnp.inf)
    logits = jnp.where(lane < n_experts, _dot(h2b, w_r_ref[...]) + vec(_VEC_B_ROUTER, V7X_LANES), neg_inf)
    ga_s[...] = _sigmoid(in_proj(c1, d, _VEC_B_GA))
    gb_s[...] = _sigmoid(in_proj(c1 + d, d, _VEC_B_GB))
    onehot = jnp.zeros((m, V7X_LANES), _F32)
    picks, vals = [], []
    for _ in range(TOP_K):
        v = jnp.max(logits, axis=-1, keepdims=True)
        p = jnp.min(jnp.where(logits == v, lane_f, float(V7X_LANES)), axis=-1, keepdims=True)
        hit = lane_f == p
        onehot = jnp.where(hit, 1.0, onehot)
        logits = jnp.where(hit, neg_inf, logits)
        picks.append(p)
        vals.append(v)
    exps = [jnp.exp(v - vals[0]) for v in vals]
    inv_den = 1.0 / functools.reduce(lambda s, e: s + e, exps)
    row = lax.broadcasted_iota(jnp.int32, (m, m), 0)
    col = lax.broadcasted_iota(jnp.int32, (m, m), 1)
    earlier = jnp.where(col < row, 1.0, 0.0).astype(_BF16)
    before = _dot(earlier, onehot.astype(_BF16)) + cnt_s[0:1, :]
    prob_out = jnp.zeros((m, V7X_LANES), _F32)
    dest_out = jnp.zeros((m, V7X_LANES), _F32)
    for k in range(TOP_K):
        rank_k = jnp.sum(jnp.where(lane_f == picks[k], before, 0.0), axis=-1, keepdims=True)
        prob_out = jnp.where(lane == k, exps[k] * inv_den, prob_out)
        dest_out = jnp.where(lane == k, picks[k] * float(region_rows) + rank_k, dest_out)
    prob_ref[...] = prob_out
    dest_t = dest_out.T[0:V7X_SUBLANES, :].astype(jnp.int32)
    for j in range(m // V7X_LANES):
        dest_ref[j] = dest_t[:, j * V7X_LANES:(j + 1) * V7X_LANES]
    has_prev = jnp.where(step > 0, 1.0, 0.0)
    cnt_new = cnt_s[...] + has_prev * jnp.sum(onehot, axis=0, keepdims=True)
    cnt_s[...] = cnt_new
    cnt_ref[...] = cnt_new


def _mixer(x, mod, p, *, alpha, n_experts):
    batch, seq, d = x.shape
    tokens = batch * seq
    steps = MIXER_STEPS
    m = steps * batch
    n_chunks = seq // steps
    n_s5_blocks = p["s5b"].shape[0]
    s5_lanes = n_s5_blocks * 2 * S5_BLOCK_STATES
    halo = (CONV_WIDTH - 1) * batch

    def const(a):
        nd = a.ndim
        return pl.BlockSpec(a.shape, lambda i, nd=nd: (0,) * nd, pipeline_mode=pl.Buffered(1))

    weights = [p["vecs"], p["w_in"], p["wg"], p["w_rnn"], p["s5a"], p["s5b"], p["s5c"], p["w_glu"], p["w_out"],
               p["w_r"]]
    prev = lambda i: jnp.maximum(i - 1, 0)
    row_spec = lambda width: pl.BlockSpec((m, width), lambda i: (prev(i), 0))
    chunks = m // V7X_LANES
    out_shape = (
        jax.ShapeDtypeStruct((tokens, d), _F32),
        jax.ShapeDtypeStruct((tokens, d // 2), jnp.int32),
        jax.ShapeDtypeStruct((tokens // V7X_LANES, V7X_SUBLANES, V7X_LANES), jnp.int32),
        jax.ShapeDtypeStruct((tokens, V7X_LANES), _F32),
        jax.ShapeDtypeStruct((V7X_SUBLANES, V7X_LANES), _F32),
    )
    act = pltpu.VMEM((m, d), _F32)
    scratch = [
        pltpu.VMEM((2, d // V7X_LANES, m, V7X_LANES), _F32),
        pltpu.VMEM((m + halo, d), _F32),
        act, act,
        pltpu.VMEM((m, s5_lanes), _F32),
        pltpu.VMEM((m, n_s5_blocks * V7X_LANES), _F32),
        act, act, act,
        pltpu.VMEM((batch, d), _F32),
        pltpu.VMEM((batch, s5_lanes), _F32),
        pltpu.VMEM((V7X_SUBLANES, V7X_LANES), _F32),
    ]
    weight_bytes = sum(w.size * w.dtype.itemsize for w in weights)
    act_bytes = m * d * 4
    vmem = weight_bytes + 32 * act_bytes
    kern = functools.partial(_mixer_kernel, alpha, steps, batch, d, n_s5_blocks, n_experts, tokens)
    return pl.pallas_call(
        kern,
        grid=(n_chunks + 1,),
        in_specs=[pl.BlockSpec((batch, steps, d), lambda i: (0, jnp.minimum(i, n_chunks - 1), 0)),
                  const(mod)] + [const(w) for w in weights],
        out_specs=(row_spec(d), row_spec(d // 2),
                   pl.BlockSpec((chunks, V7X_SUBLANES, V7X_LANES), lambda i: (prev(i), 0, 0)),
                   row_spec(V7X_LANES),
                   pl.BlockSpec((V7X_SUBLANES, V7X_LANES), lambda i: (0, 0))),
        out_shape=out_shape,
        scratch_shapes=scratch,
        compiler_params=pltpu.CompilerParams(dimension_semantics=("arbitrary",),
                                             vmem_limit_bytes=_vmem_limit(vmem)),
        name="mixer",
    )(x, mod, *weights)


def _expert_kernel(d_ff, tile_e_ref, tile_blk_ref, n_used_ref,
                   x_ref, wgu_ref, bgu_ref, wd_ref, bd_ref, y_ref, wgu_s, wd_s):
    i = pl.program_id(0)
    d = wgu_ref.shape[1]

    @pl.when((i == 0) | (tile_e_ref[i] != tile_e_ref[jnp.maximum(i - 1, 0)]))
    def _():
        def cast(c, carry):
            r0 = pl.multiple_of(c * WEIGHT_CAST_ROWS, WEIGHT_CAST_ROWS)
            wgu_s[pl.ds(r0, WEIGHT_CAST_ROWS), :] = wgu_ref[0, pl.ds(r0, WEIGHT_CAST_ROWS), :].astype(_BF16)
            wd_s[pl.ds(r0, WEIGHT_CAST_ROWS), :] = wd_ref[0, pl.ds(r0, WEIGHT_CAST_ROWS), :].astype(_BF16)
            return carry

        lax.fori_loop(0, d // WEIGHT_CAST_ROWS, cast, 0)

    @pl.when(i < n_used_ref[0])
    def _():
        lo, hi = _unpack_rows(x_ref[...])
        x = jnp.concatenate([lo, hi], axis=1).astype(_BF16)
        gu = _dot(x, wgu_s[...]) + bgu_ref[0]
        gate = jnp.minimum(gu[:, :d_ff], SWIGLU_LIMIT)
        up = jnp.clip(gu[:, d_ff:], -SWIGLU_LIMIT, SWIGLU_LIMIT)
        act = gate * _sigmoid(SWIGLU_ALPHA * gate) * (up + 1.0)
        y_ref[...] = _pack_rows(_dot(act.astype(_BF16), wd_s[...]) + bd_ref[0])

    @pl.when(i >= n_used_ref[0])
    def _():
        y_ref[...] = jnp.zeros_like(y_ref)


def _experts(xb, tile_e, tile_blk, n_used, w_gu, b_gu, w_down, b_down, *, n_tiles):
    n_rows, half = xb.shape
    n_experts, d, two_ff = w_gu.shape
    d_ff = two_ff // 2
    assert d_ff == d, "the weight cast loop walks w_gu and w_down rows together"
    r = EXPERT_ROWS
    vmem = 2 * (d * two_ff + d_ff * d) * 4 + (d * two_ff + d_ff * d) * 2 + 8 * r * half * 4 + 6 * r * two_ff * 4
    grid_spec = pltpu.PrefetchScalarGridSpec(
        num_scalar_prefetch=3,
        grid=(n_tiles,),
        in_specs=[
            pl.BlockSpec((r, half), lambda i, te, tb, nu: (tb[i], 0)),
            pl.BlockSpec((1, d, two_ff), lambda i, te, tb, nu: (te[i], 0, 0)),
            pl.BlockSpec((1, 1, two_ff), lambda i, te, tb, nu: (te[i], 0, 0)),
            pl.BlockSpec((1, d_ff, d), lambda i, te, tb, nu: (te[i], 0, 0)),
            pl.BlockSpec((1, 1, d), lambda i, te, tb, nu: (te[i], 0, 0)),
        ],
        out_specs=pl.BlockSpec((r, half), lambda i, te, tb, nu: (tb[i], 0)),
        scratch_shapes=[pltpu.VMEM((d, two_ff), _BF16), pltpu.VMEM((d_ff, d), _BF16)],
    )
    return pl.pallas_call(
        functools.partial(_expert_kernel, d_ff),
        grid_spec=grid_spec,
        out_shape=jax.ShapeDtypeStruct((n_rows, half), jnp.int32),
        compiler_params=pltpu.CompilerParams(dimension_semantics=("arbitrary",),
                                             vmem_limit_bytes=_vmem_limit(vmem)),
        name="experts",
    )(tile_e, tile_blk, n_used, xb, w_gu, b_gu.reshape(n_experts, 1, two_ff), w_down,
      b_down.reshape(n_experts, 1, d))


def _sc_workers():
    info = plsc.get_sparse_core_info()
    return info.num_cores, info.num_subcores


def _dispatch(h2w, dest_c, n_rows):
    tokens, width = h2w.shape
    n_chunks, _, chunk = dest_c.shape
    nc, ns = _sc_workers()
    per_w = n_chunks // (nc * ns)
    assert per_w * nc * ns == n_chunks

    @functools.partial(
        pl.kernel, mesh=plsc.VectorSubcoreMesh(core_axis_name="c", subcore_axis_name="s"),
        out_type=jax.ShapeDtypeStruct((n_rows, width), h2w.dtype),
        scratch_types=[pltpu.VMEM(dest_c.shape[1:], jnp.int32), pltpu.VMEM((chunk, width), h2w.dtype)],
    )
    def scatter_rows(h_hbm, d_hbm, o_hbm, idx_v, rows_v):
        wid = lax.axis_index("s") * nc + lax.axis_index("c")

        @pl.loop(0, per_w)
        def _(j):
            blk = wid * per_w + j
            pltpu.sync_copy(d_hbm.at[blk], idx_v)
            pltpu.sync_copy(h_hbm.at[pl.ds(pl.multiple_of(blk * chunk, chunk), chunk)], rows_v)
            for k in range(TOP_K):
                pltpu.sync_copy(rows_v, o_hbm.at[idx_v.at[k]])

    return scatter_rows(h2w, dest_c)


def _collect(yb, dest_c):
    _, width = yb.shape
    n_chunks, _, chunk = dest_c.shape
    nc, ns = _sc_workers()
    per_w = n_chunks // (nc * ns)
    assert per_w * nc * ns == n_chunks

    @functools.partial(
        pl.kernel, mesh=plsc.VectorSubcoreMesh(core_axis_name="c", subcore_axis_name="s"),
        out_type=jax.ShapeDtypeStruct((TOP_K, n_chunks * chunk, width), yb.dtype),
        scratch_types=[pltpu.VMEM(dest_c.shape[1:], jnp.int32), pltpu.VMEM((chunk, width), yb.dtype)],
    )
    def gather_rows(y_hbm, d_hbm, o_hbm, idx_v, rows_v):
        wid = lax.axis_index("s") * nc + lax.axis_index("c")

        @pl.loop(0, per_w)
        def _(j):
            blk = wid * per_w + j
            pltpu.sync_copy(d_hbm.at[blk], idx_v)
            for k in range(TOP_K):
                pltpu.sync_copy(y_hbm.at[idx_v.at[k]], rows_v)
                pltpu.sync_copy(rows_v, o_hbm.at[k, pl.ds(pl.multiple_of(blk * chunk, chunk), chunk)])

    return gather_rows(yb, dest_c)


def _combine_kernel(alpha, steps, x1_ref, yg_ref, prob_ref, mod_ref, ln_g_ref, ln_b_ref, o_ref, ot_s):
    d = x1_ref.shape[1]
    batch = o_ref.shape[0]
    ffn_lo = jnp.zeros((x1_ref.shape[0], d // 2), _F32)
    ffn_hi = jnp.zeros((x1_ref.shape[0], d // 2), _F32)
    for k in range(TOP_K):
        lo, hi = _unpack_rows(yg_ref[k])
        ffn_lo = ffn_lo + prob_ref[:, k:k + 1] * lo
        ffn_hi = ffn_hi + prob_ref[:, k:k + 1] * hi
    ffn = jnp.concatenate([ffn_lo, ffn_hi], axis=1)
    gate = _rows(1.0 + mod_ref[:, 5 * d:6 * d], steps)
    out = _layer_norm(alpha * x1_ref[...] + gate * ffn, ln_g_ref[...], ln_b_ref[...])
    n_blk = d // V7X_LANES
    for j in range(n_blk):
        ot_s[j] = out[:, j * V7X_LANES:(j + 1) * V7X_LANES]
    for b in range(batch):
        for j in range(n_blk):
            o_ref[b, :, j * V7X_LANES:(j + 1) * V7X_LANES] = ot_s[j, pl.ds(b, steps, stride=batch), :]


def _combine(x1, yg, prob, mod, ln_g, ln_b, *, alpha, batch):
    tokens, d = x1.shape
    rows = COMBINE_ROWS
    steps = rows // batch
    const = lambda a: pl.BlockSpec(a.shape, lambda i: (0, 0))
    return pl.pallas_call(
        functools.partial(_combine_kernel, alpha, steps),
        grid=(tokens // rows,),
        in_specs=[
            pl.BlockSpec((rows, d), lambda i: (i, 0)),
            pl.BlockSpec((TOP_K, rows, d // 2), lambda i: (0, i, 0)),
            pl.BlockSpec((rows, V7X_LANES), lambda i: (i, 0)),
            const(mod), const(ln_g), const(ln_b),
        ],
        out_specs=pl.BlockSpec((batch, steps, d), lambda i: (0, i, 0)),
        out_shape=jax.ShapeDtypeStruct((batch, tokens // batch, d), _F32),
        scratch_shapes=[pltpu.VMEM((d // V7X_LANES, rows, V7X_LANES), _F32)],
        compiler_params=pltpu.CompilerParams(dimension_semantics=("parallel",)),
        name="combine",
    )(x1, yg, prob, mod, ln_g, ln_b)


def _block_diag(blocks):
    nb, n, a, b = blocks.shape
    eye = jnp.eye(n, dtype=blocks.dtype)
    return (eye[None, :, None, :, None] * blocks[:, :, :, None, :]).reshape(nb, n * a, n * b)


def _s5_params(lam_re, lam_im, log_dt, b_re, b_im, c_re, c_im):
    groups = lam_re.shape[0]
    nb = groups // S5_BLOCK_GROUPS
    dt = jnp.exp(log_dt)[:, None]
    mag = jnp.exp(lam_re * dt)
    ab_re, ab_im = mag * jnp.cos(lam_im * dt), mag * jnp.sin(lam_im * dt)
    den = lam_re * lam_re + lam_im * lam_im
    q_re = ((ab_re - 1.0) * lam_re + ab_im * lam_im) / den
    q_im = (ab_im * lam_re - (ab_re - 1.0) * lam_im) / den
    bb_re = q_re[..., None] * b_re - q_im[..., None] * b_im
    bb_im = q_re[..., None] * b_im + q_im[..., None] * b_re

    def per_block(a):
        return jnp.swapaxes(a.reshape(nb, S5_BLOCK_GROUPS, *a.shape[1:]), 2, 3)

    bmat = jnp.concatenate([_block_diag(per_block(bb_re)), _block_diag(per_block(bb_im))], axis=2)
    cmat = jnp.concatenate([_block_diag(per_block(c_re)), -_block_diag(per_block(c_im))], axis=1)
    s5a = jnp.broadcast_to(jnp.stack([ab_re, ab_im]).reshape(2, nb, 1, S5_BLOCK_STATES),
                           (2, nb, V7X_SUBLANES, S5_BLOCK_STATES))
    return dict(s5a=s5a, s5b=bmat.astype(_BF16), s5c=cmat.astype(_BF16))


def _mixer_vectors(d, b_in, conv_w, conv_b, b_rg_a, b_rg_x, lru_lambda, ln_g, ln_b, s5_d, b_router):
    pad = lambda v: jnp.pad(v, (0, d - v.shape[0]))
    s5w = s5_d.size
    rows = [b_in[0:d], b_in[d:2 * d], pad(b_in[2 * d:2 * d + s5w]), b_in[2 * d + s5w:3 * d + s5w],
            b_in[3 * d + s5w:4 * d + s5w], *conv_w, conv_b, b_rg_a, b_rg_x,
            -LRU_C * jax.nn.softplus(-lru_lambda), ln_g, ln_b, pad(s5_d.reshape(-1)), pad(b_router)]
    rows += [jnp.zeros((d,), _F32)] * (-len(rows) % V7X_SUBLANES)
    return jnp.stack(rows)


def kernel(x, c, w_ada, b_ada, w_in, b_in, conv_w, conv_b, w_rg_a, b_rg_a, w_rg_x, b_rg_x, lru_lambda, w_rnn_out, s5_lambda_re, s5_lambda_im, s5_log_dt, s5_b_re, s5_b_im, s5_c_re, s5_c_im, s5_d, w_glu, w_out, ln1_g, ln1_b, w_router, b_router, w_gu, b_gu, w_down, b_down, ln2_g, ln2_b):
    batch, seq, d = x.shape
    depth = w_ada.shape[0]
    n_experts = w_router.shape[-1]
    tokens = batch * seq
    alpha = (2.0 * depth) ** 0.25
    assert batch == V7X_SUBLANES and d % V7X_LANES == 0 and n_experts <= V7X_LANES
    assert seq % MIXER_STEPS == 0 and tokens % COMBINE_ROWS == 0 and tokens % EXPERT_ROWS == 0
    assert (MIXER_STEPS * batch) % V7X_LANES == 0

    for l in range(depth):
        mod = _ada(c, w_ada[l], b_ada[l])
        p = dict(
            vecs=_mixer_vectors(d, b_in[l], conv_w[l], conv_b[l], b_rg_a[l], b_rg_x[l], lru_lambda[l],
                                ln1_g[l], ln1_b[l], s5_d[l], b_router[l]),
            w_in=w_in[l].astype(_BF16),
            wg=jnp.concatenate([w_rg_a[l], w_rg_x[l]], axis=-1).astype(_BF16),
            w_rnn=w_rnn_out[l].astype(_BF16), w_glu=w_glu[l].astype(_BF16), w_out=w_out[l].astype(_BF16),
            w_r=jnp.pad(w_router[l], ((0, 0), (0, V7X_LANES - n_experts))).astype(_BF16),
            **_s5_params(s5_lambda_re[l], s5_lambda_im[l], s5_log_dt[l], s5_b_re[l], s5_b_im[l],
                         s5_c_re[l], s5_c_im[l]),
        )
        x1, h2w, dest_c, prob, cnt = _mixer(x, mod, p, alpha=alpha, n_experts=n_experts)

        r = EXPERT_ROWS
        blocks_per_region = tokens // r
        spare_blk = n_experts * blocks_per_region
        n_tiles = -(-(tokens * TOP_K + n_experts * (r - 1)) // r)
        counts = cnt[0, :n_experts].astype(jnp.int32)
        tiles_e = (counts + r - 1) // r
        tile_end = jnp.cumsum(tiles_e)
        n_used = tile_end[-1:]
        t_ids = jnp.arange(n_tiles, dtype=jnp.int32)
        done = (tile_end[None, :] <= t_ids[:, None]).astype(jnp.int32)
        tile_e = jnp.minimum(jnp.sum(done, axis=1), n_experts - 1)
        first_tile = jnp.sum(done * tiles_e[None, :], axis=1)
        tile_blk = jnp.where(t_ids < n_used, tile_e * blocks_per_region + t_ids - first_tile, spare_blk)

        xb = _dispatch(h2w, dest_c, (spare_blk + 1) * r)
        yb = _experts(xb, tile_e, tile_blk, n_used, w_gu[l], b_gu[l], w_down[l], b_down[l], n_tiles=n_tiles)
        yg = _collect(yb, dest_c)
        x = _combine(x1, yg, prob, mod, ln2_g[l].reshape(1, -1), ln2_b[l].reshape(1, -1),
                     alpha=alpha, batch=batch)
    return x
```

```python
import functools

import jax
import jax.numpy as jnp
from jax import lax
from jax.experimental import pallas as pl
from jax.experimental.pallas import tpu as pltpu
from jax.experimental.pallas import tpu_sc as plsc

V7X_SUBLANES = 8
V7X_LANES = 128
V7X_VMEM_BYTES = 64 * 1024 * 1024

CONV_WIDTH = 4
LRU_C = 8.0
S5_GROUP = 16
S5_STATE = 64
TOP_K = 4
SWIGLU_LIMIT = 7.0
SWIGLU_ALPHA = 1.702
LN_EPS = 1e-5

S5_BLOCK_GROUPS = V7X_LANES // S5_GROUP
S5_BLOCK_STATES = S5_BLOCK_GROUPS * S5_STATE

(_VEC_B_X, _VEC_B_Y, _VEC_B_U5, _VEC_B_GA, _VEC_B_GB, _VEC_CONV_W) = range(6)
(_VEC_CONV_B, _VEC_B_RG_A, _VEC_B_RG_X, _VEC_LAMC, _VEC_LN_G, _VEC_LN_B, _VEC_S5_D, _VEC_B_ROUTER) = range(
    _VEC_CONV_W + CONV_WIDTH, _VEC_CONV_W + CONV_WIDTH + 8)

MIXER_STEPS = 32
EXPERT_ROWS = 512
COMBINE_ROWS = 1024
WEIGHT_CAST_ROWS = 64
SCAN_UNROLL = 4

_BF16 = jnp.bfloat16
_F32 = jnp.float32


def _dot(a, b):
    return jnp.dot(a, b, preferred_element_type=_F32)


def _sigmoid(v):
    return 0.5 * jnp.tanh(0.5 * v) + 0.5


def _vmem_limit(nbytes):
    return int(min(nbytes, V7X_VMEM_BYTES - 4 * 1024 * 1024))


def _layer_norm(z, gain, bias):
    mu = jnp.mean(z, axis=-1, keepdims=True)
    zc = z - mu
    var = jnp.mean(zc * zc, axis=-1, keepdims=True)
    return zc * lax.rsqrt(var + LN_EPS) * gain + bias


def _rows(v, steps):
    return jnp.tile(v, (steps, 1))


_HI_MASK = 0xFFFF0000


def _pack_rows(v):
    half = v.shape[1] // 2
    bits = lax.bitcast_convert_type(v.astype(_BF16).astype(_F32), jnp.uint32)
    packed = (bits[:, :half] >> 16) | (bits[:, half:] & jnp.uint32(_HI_MASK))
    return lax.bitcast_convert_type(packed, jnp.int32)


def _unpack_rows(w):
    bits = lax.bitcast_convert_type(w, jnp.uint32)
    lo = lax.bitcast_convert_type(bits << 16, _F32)
    hi = lax.bitcast_convert_type(bits & jnp.uint32(_HI_MASK), _F32)
    return lo, hi


def _ada_kernel(c_ref, w_ref, b_ref, o_ref):
    c = c_ref[...]
    c_act = (c * _sigmoid(c)).astype(_BF16)
    o_ref[...] = _dot(c_act, w_ref[...].astype(_BF16)) + b_ref[...]


def _ada(c, w_ada, b_ada):
    batch, d = c.shape
    n_out = w_ada.shape[1]
    return pl.pallas_call(
        _ada_kernel,
        grid=(n_out // d,),
        in_specs=[
            pl.BlockSpec((batch, d), lambda j: (0, 0)),
            pl.BlockSpec((d, d), lambda j: (0, j)),
            pl.BlockSpec((1, d), lambda j: (0, j)),
        ],
        out_specs=pl.BlockSpec((batch, d), lambda j: (0, j)),
        out_shape=jax.ShapeDtypeStruct((batch, n_out), _F32),
        name="ada",
    )(c, w_ada, b_ada.reshape(1, n_out))


def _mixer_kernel(alpha, steps, batch, d, n_s5_blocks, n_experts, region_rows,
                  x_ref, mod_ref, vecs_ref, w_in_ref, wg_ref, w_rnn_ref, s5a_ref, s5b_ref, s5c_ref,
                  w_glu_ref, w_out_ref, w_r_ref,
                  x1_ref, h2_ref, dest_ref, prob_ref, cnt_ref,
                  xt_s, xc_s, a_s, u_s, bu_s, u5_s, ya_s, ga_s, gb_s, h_state, s5_state, cnt_s):
    m = steps * batch
    halo = (CONV_WIDTH - 1) * batch
    s5w = n_s5_blocks * V7X_LANES
    n_blk = d // V7X_LANES
    bs = S5_BLOCK_STATES
    step = pl.program_id(0)

    @pl.when(step == 0)
    def _():
        xc_s[0:halo, :] = jnp.zeros((halo, d), _F32)
        h_state[...] = jnp.zeros_like(h_state)
        s5_state[...] = jnp.zeros_like(s5_state)
        cnt_s[...] = jnp.zeros_like(cnt_s)
        for ref in (xt_s, u_s, bu_s, u5_s, ya_s, ga_s, gb_s):
            ref[...] = jnp.zeros_like(ref)

    def mod(k):
        return mod_ref[:, k * d:(k + 1) * d]

    slot = lax.rem(step, 2)
    for b in range(batch):
        for j in range(n_blk):
            xt_s[slot, j, pl.ds(b, steps, stride=batch), :] = x_ref[b, :, j * V7X_LANES:(j + 1) * V7X_LANES]
    x = jnp.concatenate([xt_s[slot, j] for j in range(n_blk)], axis=1)
    hb = (x * _rows(1.0 + mod(1), steps) + _rows(mod(0), steps)).astype(_BF16)

    def vec(k, width=d):
        return vecs_ref[k:k + 1, 0:width]

    def in_proj(c0, width, bias_row):
        return _dot(hb, w_in_ref[:, c0:c0 + width]) + vec(bias_row, width)

    c0 = 2 * d
    c1 = c0 + s5w
    branch_a = _dot((ya_s[...] * u_s[...]).astype(_BF16), w_rnn_ref[...])
    y5 = jnp.concatenate(
        [_dot(bu_s[:, 2 * bs * j:2 * bs * (j + 1)].astype(_BF16), s5c_ref[j]) for j in range(n_s5_blocks)],
        axis=1) + vec(_VEC_S5_D, s5w) * u5_s[...]
    xc_s[halo:halo + m, :] = in_proj(0, d, _VEC_B_X)
    glu = _dot(jax.nn.gelu(y5).astype(_BF16), w_glu_ref[...])
    xr = jnp.zeros((m, d), _F32) + vec(_VEC_CONV_B)
    for k in range(CONV_WIDTH):
        xr = xr + vec(_VEC_CONV_W + k) * xc_s[k * batch:k * batch + m, :]
    xc_s[0:halo, :] = xc_s[m:m + halo, :]
    xrb = xr.astype(_BF16)
    gates = [_dot(xrb[:, j * V7X_LANES:(j + 1) * V7X_LANES], wg_ref[j]) for j in range(n_blk)]
    merged = (ga_s[...] * branch_a + gb_s[...] * (glu[:, :d] * _sigmoid(glu[:, d:]))).astype(_BF16)
    u5 = in_proj(c0, s5w, _VEC_B_U5)
    u5_s[...] = u5
    r_gate = _sigmoid(jnp.concatenate([g[:, :V7X_LANES] for g in gates], axis=1) + vec(_VEC_B_RG_A))
    i_gate = _sigmoid(jnp.concatenate([g[:, V7X_LANES:] for g in gates], axis=1) + vec(_VEC_B_RG_X))
    a = jnp.exp(vec(_VEC_LAMC) * r_gate)
    a_s[...] = a
    z = 1.0 - a * a
    u_s[...] = jnp.where(z > 0.0, z * lax.rsqrt(z), 0.0) * (i_gate * xr)
    mix = _dot(merged, w_out_ref[...])
    u5b = u5.astype(_BF16)
    for j in range(n_s5_blocks):
        bu_s[:, 2 * bs * j:2 * bs * (j + 1)] = _dot(u5b[:, j * V7X_LANES:(j + 1) * V7X_LANES], s5b_ref[j])
    hc = h_state[...]
    s5c = [(s5_state[:, 2 * bs * j:2 * bs * j + bs], s5_state[:, 2 * bs * j + bs:2 * bs * (j + 1)])
           for j in range(n_s5_blocks)]
    for t in range(steps):
        r0 = t * batch
        hc = a_s[r0:r0 + batch, :] * hc + u_s[r0:r0 + batch, :]
        u_s[r0:r0 + batch, :] = hc
        for j in range(n_s5_blocks):
            re0, im0 = 2 * bs * j, 2 * bs * j + bs
            re, im = s5c[j]
            ar, ai = s5a_ref[0, j], s5a_ref[1, j]
            nre = ar * re - ai * im + bu_s[r0:r0 + batch, re0:re0 + bs]
            nim = ar * im + ai * re + bu_s[r0:r0 + batch, im0:im0 + bs]
            bu_s[r0:r0 + batch, re0:re0 + bs] = nre
            bu_s[r0:r0 + batch, im0:im0 + bs] = nim
            s5c[j] = (nre, nim)
    h_state[...] = hc
    for j in range(n_s5_blocks):
        s5_state[:, 2 * bs * j:2 * bs * j + bs] = s5c[j][0]
        s5_state[:, 2 * bs * j + bs:2 * bs * (j + 1)] = s5c[j][1]
    x_prev = jnp.concatenate([xt_s[1 - slot, j] for j in range(n_blk)], axis=1)
    x1 = _layer_norm(alpha * x_prev + _rows(1.0 + mod(2), steps) * mix, vec(_VEC_LN_G), vec(_VEC_LN_B))
    x1_ref[...] = x1
    h2 = x1 * _rows(1.0 + mod(4), steps) + _rows(mod(3), steps)
    h2b = h2.astype(_BF16)
    h2_ref[...] = _pack_rows(h2)
    ya_s[...] = jax.nn.gelu(in_proj(d, d, _VEC_B_Y))
    lane = lax.broadcasted_iota(jnp.int32, (m, V7X_LANES), 1)
    lane_f = lane.astype(_F32)
    neg_inf = jnp.float32(-jnp.inf)
    logits = jnp.where(lane < n_experts, _dot(h2b, w_r_ref[...]) + vec(_VEC_B_ROUTER, V7X_LANES), neg_inf)
    ga_s[...] = _sigmoid(in_proj(c1, d, _VEC_B_GA))
    gb_s[...] = _sigmoid(in_proj(c1 + d, d, _VEC_B_GB))
    onehot = jnp.zeros((m, V7X_LANES), _F32)
    picks, vals = [], []
    for _ in range(TOP_K):
        v = jnp.max(logits, axis=-1, keepdims=True)
        p = jnp.min(jnp.where(logits == v, lane_f, float(V7X_LANES)), axis=-1, keepdims=True)
        hit = lane_f == p
        onehot = jnp.where(hit, 1.0, onehot)
        logits = jnp.where(hit, neg_inf, logits)
        picks.append(p)
        vals.append(v)
    exps = [jnp.exp(v - vals[0]) for v in vals]
    inv_den = 1.0 / functools.reduce(lambda s, e: s + e, exps)
    row = lax.broadcasted_iota(jnp.int32, (m, m), 0)
    col = lax.broadcasted_iota(jnp.int32, (m, m), 1)
    earlier = jnp.where(col < row, 1.0, 0.0).astype(_BF16)
    before = _dot(earlier, onehot.astype(_BF16)) + cnt_s[0:1, :]
    prob_out = jnp.zeros((m, V7X_LANES), _F32)
    dest_out = jnp.zeros((m, V7X_LANES), _F32)
    for k in range(TOP_K):
        rank_k = jnp.sum(jnp.where(lane_f == picks[k], before, 0.0), axis=-1, keepdims=True)
        prob_out = jnp.where(lane == k, exps[k] * inv_den, prob_out)
        dest_out = jnp.where(lane == k, picks[k] * float(region_rows) + rank_k, dest_out)
    prob_ref[...] = prob_out
    dest_t = dest_out.T[0:V7X_SUBLANES, :].astype(jnp.int32)
    for j in range(m // V7X_LANES):
        dest_ref[j] = dest_t[:, j * V7X_LANES:(j + 1) * V7X_LANES]
    has_prev = jnp.where(step > 0, 1.0, 0.0)
    cnt_new = cnt_s[...] + has_prev * jnp.sum(onehot, axis=0, keepdims=True)
    cnt_s[...] = cnt_new
    cnt_ref[...] = cnt_new


def _mixer(x, mod, p, *, alpha, n_experts):
    batch, seq, d = x.shape
    tokens = batch * seq
    steps = MIXER_STEPS
    m = steps * batch
    n_chunks = seq // steps
    n_s5_blocks = p["s5b"].shape[0]
    s5_lanes = n_s5_blocks * 2 * S5_BLOCK_STATES
    halo = (CONV_WIDTH - 1) * batch

    def const(a):
        nd = a.ndim
        return pl.BlockSpec(a.shape, lambda i, nd=nd: (0,) * nd, pipeline_mode=pl.Buffered(1))

    weights = [p["vecs"], p["w_in"], p["wg"], p["w_rnn"], p["s5a"], p["s5b"], p["s5c"], p["w_glu"], p["w_out"],
               p["w_r"]]
    prev = lambda i: jnp.maximum(i - 1, 0)
    row_spec = lambda width: pl.BlockSpec((m, width), lambda i: (prev(i), 0))
    chunks = m // V7X_LANES
    out_shape = (
        jax.ShapeDtypeStruct((tokens, d), _F32),
        jax.ShapeDtypeStruct((tokens, d // 2), jnp.int32),
        jax.ShapeDtypeStruct((tokens // V7X_LANES, V7X_SUBLANES, V7X_LANES), jnp.int32),
        jax.ShapeDtypeStruct((tokens, V7X_LANES), _F32),
        jax.ShapeDtypeStruct((V7X_SUBLANES, V7X_LANES), _F32),
    )
    act = pltpu.VMEM((m, d), _F32)
    scratch = [
        pltpu.VMEM((2, d // V7X_LANES, m, V7X_LANES), _F32),
        pltpu.VMEM((m + halo, d), _F32),
        act, act,
        pltpu.VMEM((m, s5_lanes), _F32),
        pltpu.VMEM((m, n_s5_blocks * V7X_LANES), _F32),
        act, act, act,
        pltpu.VMEM((batch, d), _F32),
        pltpu.VMEM((batch, s5_lanes), _F32),
        pltpu.VMEM((V7X_SUBLANES, V7X_LANES), _F32),
    ]
    weight_bytes = sum(w.size * w.dtype.itemsize for w in weights)
    act_bytes = m * d * 4
    vmem = weight_bytes + 32 * act_bytes
    kern = functools.partial(_mixer_kernel, alpha, steps, batch, d, n_s5_blocks, n_experts, tokens)
    return pl.pallas_call(
        kern,
        grid=(n_chunks + 1,),
        in_specs=[pl.BlockSpec((batch, steps, d), lambda i: (0, jnp.minimum(i, n_chunks - 1), 0)),
                  const(mod)] + [const(w) for w in weights],
        out_specs=(row_spec(d), row_spec(d // 2),
                   pl.BlockSpec((chunks, V7X_SUBLANES, V7X_LANES), lambda i: (prev(i), 0, 0)),
                   row_spec(V7X_LANES),
                   pl.BlockSpec((V7X_SUBLANES, V7X_LANES), lambda i: (0, 0))),
        out_shape=out_shape,
        scratch_shapes=scratch,
        compiler_params=pltpu.CompilerParams(dimension_semantics=("arbitrary",),
                                             vmem_limit_bytes=_vmem_limit(vmem)),
        name="mixer",
    )(x, mod, *weights)


def _expert_kernel(d_ff, tile_e_ref, tile_blk_ref, tile_rows_ref,
                   x_ref, wgu_ref, bgu_ref, wd_ref, bd_ref, y_ref, wgu_s, wd_s):
    i = pl.program_id(0)
    d = wgu_ref.shape[1]
    r = x_ref.shape[0]

    @pl.when((i == 0) | (tile_e_ref[i] != tile_e_ref[jnp.maximum(i - 1, 0)]))
    def _():
        def cast(c, carry):
            r0 = pl.multiple_of(c * WEIGHT_CAST_ROWS, WEIGHT_CAST_ROWS)
            wgu_s[pl.ds(r0, WEIGHT_CAST_ROWS), :] = wgu_ref[0, pl.ds(r0, WEIGHT_CAST_ROWS), :].astype(_BF16)
            wd_s[pl.ds(r0, WEIGHT_CAST_ROWS), :] = wd_ref[0, pl.ds(r0, WEIGHT_CAST_ROWS), :].astype(_BF16)
            return carry

        lax.fori_loop(0, d // WEIGHT_CAST_ROWS, cast, 0)

    def mlp(rows):
        lo, hi = _unpack_rows(x_ref[0:rows, :])
        x = jnp.concatenate([lo, hi], axis=1).astype(_BF16)
        gu = _dot(x, wgu_s[...]) + bgu_ref[0]
        gate = jnp.minimum(gu[:, :d_ff], SWIGLU_LIMIT)
        up = jnp.clip(gu[:, d_ff:], -SWIGLU_LIMIT, SWIGLU_LIMIT)
        act = gate * _sigmoid(SWIGLU_ALPHA * gate) * (up + 1.0)
        y_ref[0:rows, :] = _pack_rows(_dot(act.astype(_BF16), wd_s[...]) + bd_ref[0])
        if rows < r:
            y_ref[rows:r, :] = jnp.zeros((r - rows, y_ref.shape[1]), y_ref.dtype)

    @pl.when(tile_rows_ref[i] == r)
    def _():
        mlp(r)

    @pl.when(tile_rows_ref[i] == r // 2)
    def _():
        mlp(r // 2)

    @pl.when(tile_rows_ref[i] == 0)
    def _():
        y_ref[...] = jnp.zeros_like(y_ref)


def _experts(xb, tile_e, tile_blk, tile_rows, w_gu, b_gu, w_down, b_down, *, n_tiles):
    n_rows, half = xb.shape
    n_experts, d, two_ff = w_gu.shape
    d_ff = two_ff // 2
    assert d_ff == d, "the weight cast loop walks w_gu and w_down rows together"
    r = EXPERT_ROWS
    vmem = 2 * (d * two_ff + d_ff * d) * 4 + (d * two_ff + d_ff * d) * 2 + 8 * r * half * 4 + 6 * r * two_ff * 4
    grid_spec = pltpu.PrefetchScalarGridSpec(
        num_scalar_prefetch=3,
        grid=(n_tiles,),
        in_specs=[
            pl.BlockSpec((r, half), lambda i, te, tb, nu: (tb[i], 0)),
            pl.BlockSpec((1, d, two_ff), lambda i, te, tb, nu: (te[i], 0, 0)),
            pl.BlockSpec((1, 1, two_ff), lambda i, te, tb, nu: (te[i], 0, 0)),
            pl.BlockSpec((1, d_ff, d), lambda i, te, tb, nu: (te[i], 0, 0)),
            pl.BlockSpec((1, 1, d), lambda i, te, tb, nu: (te[i], 0, 0)),
        ],
        out_specs=pl.BlockSpec((r, half), lambda i, te, tb, nu: (tb[i], 0)),
        scratch_shapes=[pltpu.VMEM((d, two_ff), _BF16), pltpu.VMEM((d_ff, d), _BF16)],
    )
    return pl.pallas_call(
        functools.partial(_expert_kernel, d_ff),
        grid_spec=grid_spec,
        out_shape=jax.ShapeDtypeStruct((n_rows, half), jnp.int32),
        compiler_params=pltpu.CompilerParams(dimension_semantics=("arbitrary",),
                                             vmem_limit_bytes=_vmem_limit(vmem)),
        name="experts",
    )(tile_e, tile_blk, tile_rows, xb, w_gu, b_gu.reshape(n_experts, 1, two_ff), w_down,
      b_down.reshape(n_experts, 1, d))


def _sc_workers():
    info = plsc.get_sparse_core_info()
    return info.num_cores, info.num_subcores


def _dispatch(h2w, dest_c, n_rows):
    tokens, width = h2w.shape
    n_chunks, _, chunk = dest_c.shape
    nc, ns = _sc_workers()
    per_w = n_chunks // (nc * ns)
    assert per_w * nc * ns == n_chunks

    @functools.partial(
        pl.kernel, mesh=plsc.VectorSubcoreMesh(core_axis_name="c", subcore_axis_name="s"),
        out_type=jax.ShapeDtypeStruct((n_rows, width), h2w.dtype),
        scratch_types=[pltpu.VMEM(dest_c.shape[1:], jnp.int32), pltpu.VMEM((chunk, width), h2w.dtype)],
    )
    def scatter_rows(h_hbm, d_hbm, o_hbm, idx_v, rows_v):
        wid = lax.axis_index("s") * nc + lax.axis_index("c")

        @pl.loop(0, per_w)
        def _(j):
            blk = wid * per_w + j
            pltpu.sync_copy(d_hbm.at[blk], idx_v)
            pltpu.sync_copy(h_hbm.at[pl.ds(pl.multiple_of(blk * chunk, chunk), chunk)], rows_v)
            for k in range(TOP_K):
                pltpu.sync_copy(rows_v, o_hbm.at[idx_v.at[k]])

    return scatter_rows(h2w, dest_c)


def _collect(yb, dest_c):
    _, width = yb.shape
    n_chunks, _, chunk = dest_c.shape
    nc, ns = _sc_workers()
    per_w = n_chunks // (nc * ns)
    assert per_w * nc * ns == n_chunks

    @functools.partial(
        pl.kernel, mesh=plsc.VectorSubcoreMesh(core_axis_name="c", subcore_axis_name="s"),
        out_type=jax.ShapeDtypeStruct((TOP_K, n_chunks * chunk, width), yb.dtype),
        scratch_types=[pltpu.VMEM(dest_c.shape[1:], jnp.int32), pltpu.VMEM((chunk, width), yb.dtype)],
    )
    def gather_rows(y_hbm, d_hbm, o_hbm, idx_v, rows_v):
        wid = lax.axis_index("s") * nc + lax.axis_index("c")

        @pl.loop(0, per_w)
        def _(j):
            blk = wid * per_w + j
            pltpu.sync_copy(d_hbm.at[blk], idx_v)
            for k in range(TOP_K):
                pltpu.sync_copy(y_hbm.at[idx_v.at[k]], rows_v)
                pltpu.sync_copy(rows_v, o_hbm.at[k, pl.ds(pl.multiple_of(blk * chunk, chunk), chunk)])

    return gather_rows(yb, dest_c)


def _combine_kernel(alpha, steps, x1_ref, yg_ref, prob_ref, mod_ref, ln_g_ref, ln_b_ref, o_ref, ot_s):
    d = x1_ref.shape[1]
    batch = o_ref.shape[0]
    ffn_lo = jnp.zeros((x1_ref.shape[0], d // 2), _F32)
    ffn_hi = jnp.zeros((x1_ref.shape[0], d // 2), _F32)
    for k in range(TOP_K):
        lo, hi = _unpack_rows(yg_ref[k])
        ffn_lo = ffn_lo + prob_ref[:, k:k + 1] * lo
        ffn_hi = ffn_hi + prob_ref[:, k:k + 1] * hi
    ffn = jnp.concatenate([ffn_lo, ffn_hi], axis=1)
    gate = _rows(1.0 + mod_ref[:, 5 * d:6 * d], steps)
    out = _layer_norm(alpha * x1_ref[...] + gate * ffn, ln_g_ref[...], ln_b_ref[...])
    n_blk = d // V7X_LANES
    for j in range(n_blk):
        ot_s[j] = out[:, j * V7X_LANES:(j + 1) * V7X_LANES]
    for b in range(batch):
        for j in range(n_blk):
            o_ref[b, :, j * V7X_LANES:(j + 1) * V7X_LANES] = ot_s[j, pl.ds(b, steps, stride=batch), :]


def _combine(x1, yg, prob, mod, ln_g, ln_b, *, alpha, batch):
    tokens, d = x1.shape
    rows = COMBINE_ROWS
    steps = rows // batch
    const = lambda a: pl.BlockSpec(a.shape, lambda i: (0, 0))
    return pl.pallas_call(
        functools.partial(_combine_kernel, alpha, steps),
        grid=(tokens // rows,),
        in_specs=[
            pl.BlockSpec((rows, d), lambda i: (i, 0)),
            pl.BlockSpec((TOP_K, rows, d // 2), lambda i: (0, i, 0)),
            pl.BlockSpec((rows, V7X_LANES), lambda i: (i, 0)),
            const(mod), const(ln_g), const(ln_b),
        ],
        out_specs=pl.BlockSpec((batch, steps, d), lambda i: (0, i, 0)),
        out_shape=jax.ShapeDtypeStruct((batch, tokens // batch, d), _F32),
        scratch_shapes=[pltpu.VMEM((d // V7X_LANES, rows, V7X_LANES), _F32)],
        compiler_params=pltpu.CompilerParams(dimension_semantics=("parallel",)),
        name="combine",
    )(x1, yg, prob, mod, ln_g, ln_b)


def _block_diag(blocks):
    nb, n, a, b = blocks.shape
    eye = jnp.eye(n, dtype=blocks.dtype)
    return (eye[None, :, None, :, None] * blocks[:, :, :, None, :]).reshape(nb, n * a, n * b)


def _s5_params(lam_re, lam_im, log_dt, b_re, b_im, c_re, c_im):
    groups = lam_re.shape[0]
    nb = groups // S5_BLOCK_GROUPS
    dt = jnp.exp(log_dt)[:, None]
    mag = jnp.exp(lam_re * dt)
    ab_re, ab_im = mag * jnp.cos(lam_im * dt), mag * jnp.sin(lam_im * dt)
    den = lam_re * lam_re + lam_im * lam_im
    q_re = ((ab_re - 1.0) * lam_re + ab_im * lam_im) / den
    q_im = (ab_im * lam_re - (ab_re - 1.0) * lam_im) / den
    bb_re = q_re[..., None] * b_re - q_im[..., None] * b_im
    bb_im = q_re[..., None] * b_im + q_im[..., None] * b_re

    def per_block(a):
        return jnp.swapaxes(a.reshape(nb, S5_BLOCK_GROUPS, *a.shape[1:]), 2, 3)

    bmat = jnp.concatenate([_block_diag(per_block(bb_re)), _block_diag(per_block(bb_im))], axis=2)
    cmat = jnp.concatenate([_block_diag(per_block(c_re)), -_block_diag(per_block(c_im))], axis=1)
    s5a = jnp.broadcast_to(jnp.stack([ab_re, ab_im]).reshape(2, nb, 1, S5_BLOCK_STATES),
                           (2, nb, V7X_SUBLANES, S5_BLOCK_STATES))
    return dict(s5a=s5a, s5b=bmat.astype(_BF16), s5c=cmat.astype(_BF16))


def _mixer_vectors(d, b_in, conv_w, conv_b, b_rg_a, b_rg_x, lru_lambda, ln_g, ln_b, s5_d, b_router):
    pad = lambda v: jnp.pad(v, (0, d - v.shape[0]))
    s5w = s5_d.size
    rows = [b_in[0:d], b_in[d:2 * d], pad(b_in[2 * d:2 * d + s5w]), b_in[2 * d + s5w:3 * d + s5w],
            b_in[3 * d + s5w:4 * d + s5w], *conv_w, conv_b, b_rg_a, b_rg_x,
            -LRU_C * jax.nn.softplus(-lru_lambda), ln_g, ln_b, pad(s5_d.reshape(-1)), pad(b_router)]
    rows += [jnp.zeros((d,), _F32)] * (-len(rows) % V7X_SUBLANES)
    return jnp.stack(rows)


def kernel(x, c, w_ada, b_ada, w_in, b_in, conv_w, conv_b, w_rg_a, b_rg_a, w_rg_x, b_rg_x, lru_lambda, w_rnn_out, s5_lambda_re, s5_lambda_im, s5_log_dt, s5_b_re, s5_b_im, s5_c_re, s5_c_im, s5_d, w_glu, w_out, ln1_g, ln1_b, w_router, b_router, w_gu, b_gu, w_down, b_down, ln2_g, ln2_b):
    batch, seq, d = x.shape
    depth = w_ada.shape[0]
    n_experts = w_router.shape[-1]
    tokens = batch * seq
    alpha = (2.0 * depth) ** 0.25
    assert batch == V7X_SUBLANES and d % V7X_LANES == 0 and n_experts <= V7X_LANES
    assert seq % MIXER_STEPS == 0 and tokens % COMBINE_ROWS == 0 and tokens % EXPERT_ROWS == 0
    assert (MIXER_STEPS * batch) % V7X_LANES == 0

    for l in range(depth):
        mod = _ada(c, w_ada[l], b_ada[l])
        p = dict(
            vecs=_mixer_vectors(d, b_in[l], conv_w[l], conv_b[l], b_rg_a[l], b_rg_x[l], lru_lambda[l],
                                ln1_g[l], ln1_b[l], s5_d[l], b_router[l]),
            w_in=w_in[l].astype(_BF16),
            wg=jnp.concatenate([w_rg_a[l], w_rg_x[l]], axis=-1).astype(_BF16),
            w_rnn=w_rnn_out[l].astype(_BF16), w_glu=w_glu[l].astype(_BF16), w_out=w_out[l].astype(_BF16),
            w_r=jnp.pad(w_router[l], ((0, 0), (0, V7X_LANES - n_experts))).astype(_BF16),
            **_s5_params(s5_lambda_re[l], s5_lambda_im[l], s5_log_dt[l], s5_b_re[l], s5_b_im[l],
                         s5_c_re[l], s5_c_im[l]),
        )
        x1, h2w, dest_c, prob, cnt = _mixer(x, mod, p, alpha=alpha, n_experts=n_experts)

        r = EXPERT_ROWS
        blocks_per_region = tokens // r
        spare_blk = n_experts * blocks_per_region
        n_tiles = -(-(tokens * TOP_K + n_experts * (r - 1)) // r)
        counts = cnt[0, :n_experts].astype(jnp.int32)
        tiles_e = (counts + r - 1) // r
        tile_end = jnp.cumsum(tiles_e)
        n_used = tile_end[-1:]
        t_ids = jnp.arange(n_tiles, dtype=jnp.int32)
        done = (tile_end[None, :] <= t_ids[:, None]).astype(jnp.int32)
        tile_e = jnp.minimum(jnp.sum(done, axis=1), n_experts - 1)
        first_tile = jnp.sum(done * tiles_e[None, :], axis=1)
        used = t_ids < n_used
        tile_blk = jnp.where(used, tile_e * blocks_per_region + t_ids - first_tile, spare_blk)
        own = (jnp.arange(n_experts, dtype=jnp.int32)[None, :] == tile_e[:, None]).astype(jnp.int32)
        valid = jnp.sum(own * counts[None, :], axis=1) - (t_ids - first_tile) * r
        tile_rows = jnp.where(used, jnp.where(valid <= r // 2, r // 2, r), 0)

        xb = _dispatch(h2w, dest_c, (spare_blk + 1) * r)
        yb = _experts(xb, tile_e, tile_blk, tile_rows, w_gu[l], b_gu[l], w_down[l], b_down[l], n_tiles=n_tiles)
        yg = _collect(yb, dest_c)
        x = _combine(x1, yg, prob, mod, ln2_g[l].reshape(1, -1), ln2_b[l].reshape(1, -1),
                     alpha=alpha, batch=batch)
    return x
```

```python
import functools

import jax
import jax.numpy as jnp
from jax import lax
from jax.experimental import pallas as pl
from jax.experimental.pallas import tpu as pltpu
from jax.experimental.pallas import tpu_sc as plsc

V7X_SUBLANES = 8
V7X_LANES = 128
V7X_VMEM_BYTES = 64 * 1024 * 1024

CONV_WIDTH = 4
LRU_C = 8.0
S5_GROUP = 16
S5_STATE = 64
TOP_K = 4
SWIGLU_LIMIT = 7.0
SWIGLU_ALPHA = 1.702
LN_EPS = 1e-5

S5_BLOCK_GROUPS = V7X_LANES // S5_GROUP
S5_BLOCK_STATES = S5_BLOCK_GROUPS * S5_STATE

(_VEC_B_X, _VEC_B_Y, _VEC_B_U5, _VEC_B_GA, _VEC_B_GB, _VEC_CONV_W) = range(6)
(_VEC_CONV_B, _VEC_B_RG_A, _VEC_B_RG_X, _VEC_LAMC, _VEC_LN_G, _VEC_LN_B, _VEC_S5_D, _VEC_B_ROUTER) = range(
    _VEC_CONV_W + CONV_WIDTH, _VEC_CONV_W + CONV_WIDTH + 8)

MIXER_STEPS = 32
EXPERT_ROWS = 512
COMBINE_ROWS = 1024
WEIGHT_CAST_ROWS = 64
SCAN_UNROLL = 4

_BF16 = jnp.bfloat16
_F32 = jnp.float32


def _dot(a, b):
    return jnp.dot(a, b, preferred_element_type=_F32)


def _sigmoid(v):
    return 0.5 * jnp.tanh(0.5 * v) + 0.5


def _vmem_limit(nbytes):
    return int(min(nbytes, V7X_VMEM_BYTES - 4 * 1024 * 1024))


def _layer_norm(z, gain, bias):
    mu = jnp.mean(z, axis=-1, keepdims=True)
    zc = z - mu
    var = jnp.mean(zc * zc, axis=-1, keepdims=True)
    return zc * lax.rsqrt(var + LN_EPS) * gain + bias


def _rows(v, steps):
    return jnp.tile(v, (steps, 1))


_HI_MASK = 0xFFFF0000


def _pack_rows(v):
    half = v.shape[1] // 2
    bits = lax.bitcast_convert_type(v.astype(_BF16).astype(_F32), jnp.uint32)
    packed = (bits[:, :half] >> 16) | (bits[:, half:] & jnp.uint32(_HI_MASK))
    return lax.bitcast_convert_type(packed, jnp.int32)


def _unpack_rows(w):
    bits = lax.bitcast_convert_type(w, jnp.uint32)
    lo = lax.bitcast_convert_type(bits << 16, _F32)
    hi = lax.bitcast_convert_type(bits & jnp.uint32(_HI_MASK), _F32)
    return lo, hi


def _ada_kernel(c_ref, w_ref, b_ref, o_ref):
    c = c_ref[...]
    c_act = (c * _sigmoid(c)).astype(_BF16)
    o_ref[...] = _dot(c_act, w_ref[...].astype(_BF16)) + b_ref[...]


def _ada(c, w_ada, b_ada):
    batch, d = c.shape
    n_out = w_ada.shape[1]
    return pl.pallas_call(
        _ada_kernel,
        grid=(n_out // d,),
        in_specs=[
            pl.BlockSpec((batch, d), lambda j: (0, 0)),
            pl.BlockSpec((d, d), lambda j: (0, j)),
            pl.BlockSpec((1, d), lambda j: (0, j)),
        ],
        out_specs=pl.BlockSpec((batch, d), lambda j: (0, j)),
        out_shape=jax.ShapeDtypeStruct((batch, n_out), _F32),
        name="ada",
    )(c, w_ada, b_ada.reshape(1, n_out))


def _mixer_kernel(alpha, steps, batch, d, n_s5_blocks, n_experts, region_rows,
                  x_ref, mod_ref, vecs_ref, w_in_ref, wg_ref, w_rnn_ref, s5a_ref, s5b_ref, s5c_ref,
                  w_glu_ref, w_out_ref, w_r_ref,
                  x1_ref, h2_ref, dest_ref, prob_ref, cnt_ref,
                  xt_s, xc_s, a_s, u_s, bu_s, u5_s, ya_s, ga_s, gb_s, h_state, s5_state, cnt_s):
    m = steps * batch
    halo = (CONV_WIDTH - 1) * batch
    s5w = n_s5_blocks * V7X_LANES
    n_blk = d // V7X_LANES
    bs = S5_BLOCK_STATES
    step = pl.program_id(0)

    @pl.when(step == 0)
    def _():
        xc_s[0:halo, :] = jnp.zeros((halo, d), _F32)
        h_state[...] = jnp.zeros_like(h_state)
        s5_state[...] = jnp.zeros_like(s5_state)
        cnt_s[...] = jnp.zeros_like(cnt_s)
        for ref in (xt_s, u_s, bu_s, u5_s, ya_s, ga_s, gb_s):
            ref[...] = jnp.zeros_like(ref)

    def mod(k):
        return mod_ref[:, k * d:(k + 1) * d]

    slot = lax.rem(step, 2)
    for b in range(batch):
        for j in range(n_blk):
            xt_s[slot, j, pl.ds(b, steps, stride=batch), :] = x_ref[b, :, j * V7X_LANES:(j + 1) * V7X_LANES]
    x = jnp.concatenate([xt_s[slot, j] for j in range(n_blk)], axis=1)
    hb = (x * _rows(1.0 + mod(1), steps) + _rows(mod(0), steps)).astype(_BF16)

    def vec(k, width=d):
        return vecs_ref[k:k + 1, 0:width]

    def in_proj(c0, width, bias_row):
        return _dot(hb, w_in_ref[:, c0:c0 + width]) + vec(bias_row, width)

    c0 = 2 * d
    c1 = c0 + s5w
    branch_a = _dot((ya_s[...] * u_s[...]).astype(_BF16), w_rnn_ref[...])
    y5 = jnp.concatenate(
        [_dot(bu_s[:, 2 * bs * j:2 * bs * (j + 1)].astype(_BF16), s5c_ref[j]) for j in range(n_s5_blocks)],
        axis=1) + vec(_VEC_S5_D, s5w) * u5_s[...]
    xc_s[halo:halo + m, :] = in_proj(0, d, _VEC_B_X)
    glu = _dot(jax.nn.gelu(y5).astype(_BF16), w_glu_ref[...])
    xr = jnp.zeros((m, d), _F32) + vec(_VEC_CONV_B)
    for k in range(CONV_WIDTH):
        xr = xr + vec(_VEC_CONV_W + k) * xc_s[k * batch:k * batch + m, :]
    xc_s[0:halo, :] = xc_s[m:m + halo, :]
    xrb = xr.astype(_BF16)
    gates = [_dot(xrb[:, j * V7X_LANES:(j + 1) * V7X_LANES], wg_ref[j]) for j in range(n_blk)]
    merged = (ga_s[...] * branch_a + gb_s[...] * (glu[:, :d] * _sigmoid(glu[:, d:]))).astype(_BF16)
    u5 = in_proj(c0, s5w, _VEC_B_U5)
    u5_s[...] = u5
    r_gate = _sigmoid(jnp.concatenate([g[:, :V7X_LANES] for g in gates], axis=1) + vec(_VEC_B_RG_A))
    i_gate = _sigmoid(jnp.concatenate([g[:, V7X_LANES:] for g in gates], axis=1) + vec(_VEC_B_RG_X))
    a = jnp.exp(vec(_VEC_LAMC) * r_gate)
    a_s[...] = a
    z = 1.0 - a * a
    u_s[...] = jnp.where(z > 0.0, z * lax.rsqrt(z), 0.0) * (i_gate * xr)
    mix = _dot(merged, w_out_ref[...])
    u5b = u5.astype(_BF16)
    for j in range(n_s5_blocks):
        bu_s[:, 2 * bs * j:2 * bs * (j + 1)] = _dot(u5b[:, j * V7X_LANES:(j + 1) * V7X_LANES], s5b_ref[j])
    hc = h_state[...]
    s5c = [(s5_state[:, 2 * bs * j:2 * bs * j + bs], s5_state[:, 2 * bs * j + bs:2 * bs * (j + 1)])
           for j in range(n_s5_blocks)]
    for t in range(steps):
        r0 = t * batch
        hc = a_s[r0:r0 + batch, :] * hc + u_s[r0:r0 + batch, :]
        u_s[r0:r0 + batch, :] = hc
        for j in range(n_s5_blocks):
            re0, im0 = 2 * bs * j, 2 * bs * j + bs
            re, im = s5c[j]
            ar, ai = s5a_ref[0, j], s5a_ref[1, j]
            nre = ar * re - ai * im + bu_s[r0:r0 + batch, re0:re0 + bs]
            nim = ar * im + ai * re + bu_s[r0:r0 + batch, im0:im0 + bs]
            bu_s[r0:r0 + batch, re0:re0 + bs] = nre
            bu_s[r0:r0 + batch, im0:im0 + bs] = nim
            s5c[j] = (nre, nim)
    h_state[...] = hc
    for j in range(n_s5_blocks):
        s5_state[:, 2 * bs * j:2 * bs * j + bs] = s5c[j][0]
        s5_state[:, 2 * bs * j + bs:2 * bs * (j + 1)] = s5c[j][1]
    x_prev = jnp.concatenate([xt_s[1 - slot, j] for j in range(n_blk)], axis=1)
    x1 = _layer_norm(alpha * x_prev + _rows(1.0 + mod(2), steps) * mix, vec(_VEC_LN_G), vec(_VEC_LN_B))
    x1_ref[...] = x1
    h2 = x1 * _rows(1.0 + mod(4), steps) + _rows(mod(3), steps)
    h2b = h2.astype(_BF16)
    h2_ref[...] = _pack_rows(h2)
    ya_s[...] = jax.nn.gelu(in_proj(d, d, _VEC_B_Y))
    lane = lax.broadcasted_iota(jnp.int32, (m, V7X_LANES), 1)
    lane_f = lane.astype(_F32)
    neg_inf = jnp.float32(-jnp.inf)
    logits = jnp.where(lane < n_experts, _dot(h2b, w_r_ref[...]) + vec(_VEC_B_ROUTER, V7X_LANES), neg_inf)
    ga_s[...] = _sigmoid(in_proj(c1, d, _VEC_B_GA))
    gb_s[...] = _sigmoid(in_proj(c1 + d, d, _VEC_B_GB))
    onehot = jnp.zeros((m, V7X_LANES), _F32)
    picks, vals = [], []
    for _ in range(TOP_K):
        v = jnp.max(logits, axis=-1, keepdims=True)
        p = jnp.min(jnp.where(logits == v, lane_f, float(V7X_LANES)), axis=-1, keepdims=True)
        hit = lane_f == p
        onehot = jnp.where(hit, 1.0, onehot)
        logits = jnp.where(hit, neg_inf, logits)
        picks.append(p)
        vals.append(v)
    exps = [jnp.exp(v - vals[0]) for v in vals]
    inv_den = 1.0 / functools.reduce(lambda s, e: s + e, exps)
    row = lax.broadcasted_iota(jnp.int32, (m, m), 0)
    col = lax.broadcasted_iota(jnp.int32, (m, m), 1)
    earlier = jnp.where(col < row, 1.0, 0.0).astype(_BF16)
    before = _dot(earlier, onehot.astype(_BF16)) + cnt_s[0:1, :]
    prob_out = jnp.zeros((m, V7X_LANES), _F32)
    dest_out = jnp.zeros((m, V7X_LANES), _F32)
    for k in range(TOP_K):
        rank_k = jnp.sum(jnp.where(lane_f == picks[k], before, 0.0), axis=-1, keepdims=True)
        prob_out = jnp.where(lane == k, exps[k] * inv_den, prob_out)
        dest_out = jnp.where(lane == k, picks[k] * float(region_rows) + rank_k, dest_out)
    prob_ref[...] = prob_out
    dest_t = dest_out.T[0:V7X_SUBLANES, :].astype(jnp.int32)
    for j in range(m // V7X_LANES):
        dest_ref[j] = dest_t[:, j * V7X_LANES:(j + 1) * V7X_LANES]
    has_prev = jnp.where(step > 0, 1.0, 0.0)
    cnt_new = cnt_s[...] + has_prev * jnp.sum(onehot, axis=0, keepdims=True)
    cnt_s[...] = cnt_new
    cnt_ref[...] = cnt_new


def _mixer(x, mod, p, *, alpha, n_experts):
    batch, seq, d = x.shape
    tokens = batch * seq
    steps = MIXER_STEPS
    m = steps * batch
    n_chunks = seq // steps
    n_s5_blocks = p["s5b"].shape[0]
    s5_lanes = n_s5_blocks * 2 * S5_BLOCK_STATES
    halo = (CONV_WIDTH - 1) * batch

    def const(a):
        nd = a.ndim
        return pl.BlockSpec(a.shape, lambda i, nd=nd: (0,) * nd, pipeline_mode=pl.Buffered(1))

    weights = [p["vecs"], p["w_in"], p["wg"], p["w_rnn"], p["s5a"], p["s5b"], p["s5c"], p["w_glu"], p["w_out"],
               p["w_r"]]
    prev = lambda i: jnp.maximum(i - 1, 0)
    row_spec = lambda width: pl.BlockSpec((m, width), lambda i: (prev(i), 0))
    chunks = m // V7X_LANES
    out_shape = (
        jax.ShapeDtypeStruct((tokens, d), _F32),
        jax.ShapeDtypeStruct((tokens, d // 2), jnp.int32),
        jax.ShapeDtypeStruct((tokens // V7X_LANES, V7X_SUBLANES, V7X_LANES), jnp.int32),
        jax.ShapeDtypeStruct((tokens, V7X_LANES), _F32),
        jax.ShapeDtypeStruct((V7X_SUBLANES, V7X_LANES), _F32),
    )
    act = pltpu.VMEM((m, d), _F32)
    scratch = [
        pltpu.VMEM((2, d // V7X_LANES, m, V7X_LANES), _F32),
        pltpu.VMEM((m + halo, d), _F32),
        act, act,
        pltpu.VMEM((m, s5_lanes), _F32),
        pltpu.VMEM((m, n_s5_blocks * V7X_LANES), _F32),
        act, act, act,
        pltpu.VMEM((batch, d), _F32),
        pltpu.VMEM((batch, s5_lanes), _F32),
        pltpu.VMEM((V7X_SUBLANES, V7X_LANES), _F32),
    ]
    weight_bytes = sum(w.size * w.dtype.itemsize for w in weights)
    act_bytes = m * d * 4
    vmem = weight_bytes + 32 * act_bytes
    kern = functools.partial(_mixer_kernel, alpha, steps, batch, d, n_s5_blocks, n_experts, tokens)
    return pl.pallas_call(
        kern,
        grid=(n_chunks + 1,),
        in_specs=[pl.BlockSpec((batch, steps, d), lambda i: (0, jnp.minimum(i, n_chunks - 1), 0)),
                  const(mod)] + [const(w) for w in weights],
        out_specs=(row_spec(d), row_spec(d // 2),
                   pl.BlockSpec((chunks, V7X_SUBLANES, V7X_LANES), lambda i: (prev(i), 0, 0)),
                   row_spec(V7X_LANES),
                   pl.BlockSpec((V7X_SUBLANES, V7X_LANES), lambda i: (0, 0))),
        out_shape=out_shape,
        scratch_shapes=scratch,
        compiler_params=pltpu.CompilerParams(dimension_semantics=("arbitrary",),
                                             vmem_limit_bytes=_vmem_limit(vmem)),
        name="mixer",
    )(x, mod, *weights)


def _expert_kernel(d_ff, tile_e_ref, tile_blk_ref, tile_rows_ref, tile_first_ref, tile_slot_ref, tile_next_ref,
                   x_ref, wgu_hbm, bgu_ref, wd_hbm, bd_ref, y_ref, wgu_f, wd_f, wgu_s, wd_s, sems):
    i = pl.program_id(0)
    d = wgu_f.shape[1]
    r = x_ref.shape[0]

    def weight_copies(expert, slot):
        return (pltpu.make_async_copy(wgu_hbm.at[expert], wgu_f.at[slot], sems.at[slot, 0]),
                pltpu.make_async_copy(wd_hbm.at[expert], wd_f.at[slot], sems.at[slot, 1]))

    @pl.when(tile_first_ref[i] == 1)
    def _():
        slot = tile_slot_ref[i]

        @pl.when(i == 0)
        def _():
            for cp in weight_copies(tile_e_ref[i], slot):
                cp.start()

        for cp in weight_copies(tile_e_ref[i], slot):
            cp.wait()

        @pl.when(tile_next_ref[i] >= 0)
        def _():
            for cp in weight_copies(tile_next_ref[i], 1 - slot):
                cp.start()

        def cast(c, carry):
            r0 = pl.multiple_of(c * WEIGHT_CAST_ROWS, WEIGHT_CAST_ROWS)
            wgu_s[pl.ds(r0, WEIGHT_CAST_ROWS), :] = wgu_f[slot, pl.ds(r0, WEIGHT_CAST_ROWS), :].astype(_BF16)
            wd_s[pl.ds(r0, WEIGHT_CAST_ROWS), :] = wd_f[slot, pl.ds(r0, WEIGHT_CAST_ROWS), :].astype(_BF16)
            return carry

        lax.fori_loop(0, d // WEIGHT_CAST_ROWS, cast, 0)

    def mlp(rows):
        lo, hi = _unpack_rows(x_ref[0:rows, :])
        x = jnp.concatenate([lo, hi], axis=1).astype(_BF16)
        gu = _dot(x, wgu_s[...]) + bgu_ref[0]
        gate = jnp.minimum(gu[:, :d_ff], SWIGLU_LIMIT)
        up = jnp.clip(gu[:, d_ff:], -SWIGLU_LIMIT, SWIGLU_LIMIT)
        act = gate * _sigmoid(SWIGLU_ALPHA * gate) * (up + 1.0)
        y_ref[0:rows, :] = _pack_rows(_dot(act.astype(_BF16), wd_s[...]) + bd_ref[0])
        if rows < r:
            y_ref[rows:r, :] = jnp.zeros((r - rows, y_ref.shape[1]), y_ref.dtype)

    @pl.when(tile_rows_ref[i] == r)
    def _():
        mlp(r)

    @pl.when(tile_rows_ref[i] == r // 2)
    def _():
        mlp(r // 2)

    @pl.when(tile_rows_ref[i] == 0)
    def _():
        y_ref[...] = jnp.zeros_like(y_ref)


def _experts(xb, schedule, w_gu, b_gu, w_down, b_down):
    n_rows, half = xb.shape
    n_experts, d, two_ff = w_gu.shape
    d_ff = two_ff // 2
    assert d_ff == d, "the weight cast loop walks w_gu and w_down rows together"
    r = EXPERT_ROWS
    vmem = 2 * (d * two_ff + d_ff * d) * 4 + (d * two_ff + d_ff * d) * 2 + 8 * r * half * 4 + 6 * r * two_ff * 4
    n_sched = len(schedule)
    tile = lambda i, *s: (s[1][i], 0)
    expert = lambda i, *s: (s[0][i], 0, 0)
    grid_spec = pltpu.PrefetchScalarGridSpec(
        num_scalar_prefetch=n_sched,
        grid=(schedule[0].shape[0],),
        in_specs=[
            pl.BlockSpec((r, half), tile),
            pl.BlockSpec(memory_space=pl.ANY),
            pl.BlockSpec((1, 1, two_ff), expert),
            pl.BlockSpec(memory_space=pl.ANY),
            pl.BlockSpec((1, 1, d), expert),
        ],
        out_specs=pl.BlockSpec((r, half), tile),
        scratch_shapes=[pltpu.VMEM((2, d, two_ff), _F32), pltpu.VMEM((2, d_ff, d), _F32),
                        pltpu.VMEM((d, two_ff), _BF16), pltpu.VMEM((d_ff, d), _BF16),
                        pltpu.SemaphoreType.DMA((2, 2))],
    )
    return pl.pallas_call(
        functools.partial(_expert_kernel, d_ff),
        grid_spec=grid_spec,
        out_shape=jax.ShapeDtypeStruct((n_rows, half), jnp.int32),
        compiler_params=pltpu.CompilerParams(dimension_semantics=("arbitrary",),
                                             vmem_limit_bytes=_vmem_limit(vmem)),
        name="experts",
    )(*schedule, xb, w_gu, b_gu.reshape(n_experts, 1, two_ff), w_down, b_down.reshape(n_experts, 1, d))


def _sc_workers():
    info = plsc.get_sparse_core_info()
    return info.num_cores, info.num_subcores


def _dispatch(h2w, dest_c, n_rows):
    tokens, width = h2w.shape
    n_chunks, _, chunk = dest_c.shape
    nc, ns = _sc_workers()
    per_w = n_chunks // (nc * ns)
    assert per_w * nc * ns == n_chunks

    @functools.partial(
        pl.kernel, mesh=plsc.VectorSubcoreMesh(core_axis_name="c", subcore_axis_name="s"),
        out_type=jax.ShapeDtypeStruct((n_rows, width), h2w.dtype),
        scratch_types=[pltpu.VMEM(dest_c.shape[1:], jnp.int32), pltpu.VMEM((chunk, width), h2w.dtype)],
    )
    def scatter_rows(h_hbm, d_hbm, o_hbm, idx_v, rows_v):
        wid = lax.axis_index("s") * nc + lax.axis_index("c")

        @pl.loop(0, per_w)
        def _(j):
            blk = wid * per_w + j
            pltpu.sync_copy(d_hbm.at[blk], idx_v)
            pltpu.sync_copy(h_hbm.at[pl.ds(pl.multiple_of(blk * chunk, chunk), chunk)], rows_v)
            for k in range(TOP_K):
                pltpu.sync_copy(rows_v, o_hbm.at[idx_v.at[k]])

    return scatter_rows(h2w, dest_c)


def _collect(yb, dest_c):
    _, width = yb.shape
    n_chunks, _, chunk = dest_c.shape
    nc, ns = _sc_workers()
    per_w = n_chunks // (nc * ns)
    assert per_w * nc * ns == n_chunks

    @functools.partial(
        pl.kernel, mesh=plsc.VectorSubcoreMesh(core_axis_name="c", subcore_axis_name="s"),
        out_type=jax.ShapeDtypeStruct((TOP_K, n_chunks * chunk, width), yb.dtype),
        scratch_types=[pltpu.VMEM(dest_c.shape[1:], jnp.int32), pltpu.VMEM((chunk, width), yb.dtype)],
    )
    def gather_rows(y_hbm, d_hbm, o_hbm, idx_v, rows_v):
        wid = lax.axis_index("s") * nc + lax.axis_index("c")

        @pl.loop(0, per_w)
        def _(j):
            blk = wid * per_w + j
            pltpu.sync_copy(d_hbm.at[blk], idx_v)
            for k in range(TOP_K):
                pltpu.sync_copy(y_hbm.at[idx_v.at[k]], rows_v)
                pltpu.sync_copy(rows_v, o_hbm.at[k, pl.ds(pl.multiple_of(blk * chunk, chunk), chunk)])

    return gather_rows(yb, dest_c)


def _combine_kernel(alpha, steps, x1_ref, yg_ref, prob_ref, mod_ref, ln_g_ref, ln_b_ref, o_ref, ot_s):
    d = x1_ref.shape[1]
    batch = o_ref.shape[0]
    ffn_lo = jnp.zeros((x1_ref.shape[0], d // 2), _F32)
    ffn_hi = jnp.zeros((x1_ref.shape[0], d // 2), _F32)
    for k in range(TOP_K):
        lo, hi = _unpack_rows(yg_ref[k])
        ffn_lo = ffn_lo + prob_ref[:, k:k + 1] * lo
        ffn_hi = ffn_hi + prob_ref[:, k:k + 1] * hi
    ffn = jnp.concatenate([ffn_lo, ffn_hi], axis=1)
    gate = _rows(1.0 + mod_ref[:, 5 * d:6 * d], steps)
    out = _layer_norm(alpha * x1_ref[...] + gate * ffn, ln_g_ref[...], ln_b_ref[...])
    n_blk = d // V7X_LANES
    for j in range(n_blk):
        ot_s[j] = out[:, j * V7X_LANES:(j + 1) * V7X_LANES]
    for b in range(batch):
        for j in range(n_blk):
            o_ref[b, :, j * V7X_LANES:(j + 1) * V7X_LANES] = ot_s[j, pl.ds(b, steps, stride=batch), :]


def _combine(x1, yg, prob, mod, ln_g, ln_b, *, alpha, batch):
    tokens, d = x1.shape
    rows = COMBINE_ROWS
    steps = rows // batch
    const = lambda a: pl.BlockSpec(a.shape, lambda i: (0, 0))
    return pl.pallas_call(
        functools.partial(_combine_kernel, alpha, steps),
        grid=(tokens // rows,),
        in_specs=[
            pl.BlockSpec((rows, d), lambda i: (i, 0)),
            pl.BlockSpec((TOP_K, rows, d // 2), lambda i: (0, i, 0)),
            pl.BlockSpec((rows, V7X_LANES), lambda i: (i, 0)),
            const(mod), const(ln_g), const(ln_b),
        ],
        out_specs=pl.BlockSpec((batch, steps, d), lambda i: (0, i, 0)),
        out_shape=jax.ShapeDtypeStruct((batch, tokens // batch, d), _F32),
        scratch_shapes=[pltpu.VMEM((d // V7X_LANES, rows, V7X_LANES), _F32)],
        compiler_params=pltpu.CompilerParams(dimension_semantics=("parallel",)),
        name="combine",
    )(x1, yg, prob, mod, ln_g, ln_b)


def _block_diag(blocks):
    nb, n, a, b = blocks.shape
    eye = jnp.eye(n, dtype=blocks.dtype)
    return (eye[None, :, None, :, None] * blocks[:, :, :, None, :]).reshape(nb, n * a, n * b)


def _s5_params(lam_re, lam_im, log_dt, b_re, b_im, c_re, c_im):
    groups = lam_re.shape[0]
    nb = groups // S5_BLOCK_GROUPS
    dt = jnp.exp(log_dt)[:, None]
    mag = jnp.exp(lam_re * dt)
    ab_re, ab_im = mag * jnp.cos(lam_im * dt), mag * jnp.sin(lam_im * dt)
    den = lam_re * lam_re + lam_im * lam_im
    q_re = ((ab_re - 1.0) * lam_re + ab_im * lam_im) / den
    q_im = (ab_im * lam_re - (ab_re - 1.0) * lam_im) / den
    bb_re = q_re[..., None] * b_re - q_im[..., None] * b_im
    bb_im = q_re[..., None] * b_im + q_im[..., None] * b_re

    def per_block(a):
        return jnp.swapaxes(a.reshape(nb, S5_BLOCK_GROUPS, *a.shape[1:]), 2, 3)

    bmat = jnp.concatenate([_block_diag(per_block(bb_re)), _block_diag(per_block(bb_im))], axis=2)
    cmat = jnp.concatenate([_block_diag(per_block(c_re)), -_block_diag(per_block(c_im))], axis=1)
    s5a = jnp.broadcast_to(jnp.stack([ab_re, ab_im]).reshape(2, nb, 1, S5_BLOCK_STATES),
                           (2, nb, V7X_SUBLANES, S5_BLOCK_STATES))
    return dict(s5a=s5a, s5b=bmat.astype(_BF16), s5c=cmat.astype(_BF16))


def _mixer_vectors(d, b_in, conv_w, conv_b, b_rg_a, b_rg_x, lru_lambda, ln_g, ln_b, s5_d, b_router):
    pad = lambda v: jnp.pad(v, (0, d - v.shape[0]))
    s5w = s5_d.size
    rows = [b_in[0:d], b_in[d:2 * d], pad(b_in[2 * d:2 * d + s5w]), b_in[2 * d + s5w:3 * d + s5w],
            b_in[3 * d + s5w:4 * d + s5w], *conv_w, conv_b, b_rg_a, b_rg_x,
            -LRU_C * jax.nn.softplus(-lru_lambda), ln_g, ln_b, pad(s5_d.reshape(-1)), pad(b_router)]
    rows += [jnp.zeros((d,), _F32)] * (-len(rows) % V7X_SUBLANES)
    return jnp.stack(rows)


def kernel(x, c, w_ada, b_ada, w_in, b_in, conv_w, conv_b, w_rg_a, b_rg_a, w_rg_x, b_rg_x, lru_lambda, w_rnn_out, s5_lambda_re, s5_lambda_im, s5_log_dt, s5_b_re, s5_b_im, s5_c_re, s5_c_im, s5_d, w_glu, w_out, ln1_g, ln1_b, w_router, b_router, w_gu, b_gu, w_down, b_down, ln2_g, ln2_b):
    batch, seq, d = x.shape
    depth = w_ada.shape[0]
    n_experts = w_router.shape[-1]
    tokens = batch * seq
    alpha = (2.0 * depth) ** 0.25
    assert batch == V7X_SUBLANES and d % V7X_LANES == 0 and n_experts <= V7X_LANES
    assert seq % MIXER_STEPS == 0 and tokens % COMBINE_ROWS == 0 and tokens % EXPERT_ROWS == 0
    assert (MIXER_STEPS * batch) % V7X_LANES == 0

    for l in range(depth):
        mod = _ada(c, w_ada[l], b_ada[l])
        p = dict(
            vecs=_mixer_vectors(d, b_in[l], conv_w[l], conv_b[l], b_rg_a[l], b_rg_x[l], lru_lambda[l],
                                ln1_g[l], ln1_b[l], s5_d[l], b_router[l]),
            w_in=w_in[l].astype(_BF16),
            wg=jnp.concatenate([w_rg_a[l], w_rg_x[l]], axis=-1).astype(_BF16),
            w_rnn=w_rnn_out[l].astype(_BF16), w_glu=w_glu[l].astype(_BF16), w_out=w_out[l].astype(_BF16),
            w_r=jnp.pad(w_router[l], ((0, 0), (0, V7X_LANES - n_experts))).astype(_BF16),
            **_s5_params(s5_lambda_re[l], s5_lambda_im[l], s5_log_dt[l], s5_b_re[l], s5_b_im[l],
                         s5_c_re[l], s5_c_im[l]),
        )
        x1, h2w, dest_c, prob, cnt = _mixer(x, mod, p, alpha=alpha, n_experts=n_experts)

        r = EXPERT_ROWS
        blocks_per_region = tokens // r
        spare_blk = n_experts * blocks_per_region
        n_tiles = -(-(tokens * TOP_K + n_experts * (r - 1)) // r)
        counts = cnt[0, :n_experts].astype(jnp.int32)
        tiles_e = (counts + r - 1) // r
        tile_end = jnp.cumsum(tiles_e)
        n_used = tile_end[-1:]
        t_ids = jnp.arange(n_tiles, dtype=jnp.int32)
        done = (tile_end[None, :] <= t_ids[:, None]).astype(jnp.int32)
        tile_e = jnp.minimum(jnp.sum(done, axis=1), n_experts - 1)
        first_tile = jnp.sum(done * tiles_e[None, :], axis=1)
        used = t_ids < n_used
        tile_blk = jnp.where(used, tile_e * blocks_per_region + t_ids - first_tile, spare_blk)
        e_ids = jnp.arange(n_experts, dtype=jnp.int32)[None, :]
        own = (e_ids == tile_e[:, None]).astype(jnp.int32)
        valid = jnp.sum(own * counts[None, :], axis=1) - (t_ids - first_tile) * r
        tile_rows = jnp.where(used, jnp.where(valid <= r // 2, r // 2, r), 0)
        has_tiles = (tiles_e > 0).astype(jnp.int32)[None, :]
        tile_first = (used & (t_ids == first_tile)).astype(jnp.int32)
        tile_slot = jnp.sum(done * has_tiles, axis=1) % 2
        later = jnp.where((has_tiles > 0) & (e_ids > tile_e[:, None]), e_ids, n_experts)
        tile_next = jnp.min(later, axis=1)
        tile_next = jnp.where(tile_next < n_experts, tile_next, -1)

        xb = _dispatch(h2w, dest_c, (spare_blk + 1) * r)
        yb = _experts(xb, (tile_e, tile_blk, tile_rows, tile_first, tile_slot, tile_next),
                      w_gu[l], b_gu[l], w_down[l], b_down[l])
        yg = _collect(yb, dest_c)
        x = _combine(x1, yg, prob, mod, ln2_g[l].reshape(1, -1), ln2_b[l].reshape(1, -1),
                     alpha=alpha, batch=batch)
    return x
```

```python
import functools

import jax
import jax.numpy as jnp
from jax import lax
from jax.experimental import pallas as pl
from jax.experimental.pallas import tpu as pltpu
from jax.experimental.pallas import tpu_sc as plsc

V7X_SUBLANES = 8
V7X_LANES = 128
V7X_VMEM_BYTES = 64 * 1024 * 1024

CONV_WIDTH = 4
LRU_C = 8.0
S5_GROUP = 16
S5_STATE = 64
TOP_K = 4
SWIGLU_LIMIT = 7.0
SWIGLU_ALPHA = 1.702
LN_EPS = 1e-5

S5_BLOCK_GROUPS = V7X_LANES // S5_GROUP
S5_BLOCK_STATES = S5_BLOCK_GROUPS * S5_STATE

(_VEC_B_X, _VEC_B_Y, _VEC_B_U5, _VEC_B_GA, _VEC_B_GB, _VEC_CONV_W) = range(6)
(_VEC_CONV_B, _VEC_B_RG_A, _VEC_B_RG_X, _VEC_LAMC, _VEC_LN_G, _VEC_LN_B, _VEC_S5_D, _VEC_B_ROUTER) = range(
    _VEC_CONV_W + CONV_WIDTH, _VEC_CONV_W + CONV_WIDTH + 8)

MIXER_STEPS = 32
EXPERT_ROWS = 512
COMBINE_ROWS = 1024
COLLECT_PARTS = 2
WEIGHT_CAST_ROWS = 64

_BF16 = jnp.bfloat16
_F32 = jnp.float32


def _dot(a, b):
    return jnp.dot(a, b, preferred_element_type=_F32)


def _sigmoid(v):
    return 0.5 * jnp.tanh(0.5 * v) + 0.5


def _vmem_limit(nbytes):
    return int(min(nbytes, V7X_VMEM_BYTES - 4 * 1024 * 1024))


def _layer_norm(z, gain, bias):
    mu = jnp.mean(z, axis=-1, keepdims=True)
    zc = z - mu
    var = jnp.mean(zc * zc, axis=-1, keepdims=True)
    return zc * lax.rsqrt(var + LN_EPS) * gain + bias


def _rows(v, steps):
    return jnp.tile(v, (steps, 1))


_HI_MASK = 0xFFFF0000


def _pack_rows(v):
    half = v.shape[1] // 2
    bits = lax.bitcast_convert_type(v.astype(_BF16).astype(_F32), jnp.uint32)
    packed = (bits[:, :half] >> 16) | (bits[:, half:] & jnp.uint32(_HI_MASK))
    return lax.bitcast_convert_type(packed, jnp.int32)


def _unpack_rows(w):
    bits = lax.bitcast_convert_type(w, jnp.uint32)
    lo = lax.bitcast_convert_type(bits << 16, _F32)
    hi = lax.bitcast_convert_type(bits & jnp.uint32(_HI_MASK), _F32)
    return lo, hi


def _ada_kernel(c_ref, w_ref, b_ref, o_ref):
    c = c_ref[...]
    c_act = (c * _sigmoid(c)).astype(_BF16)
    o_ref[...] = _dot(c_act, w_ref[...].astype(_BF16)) + b_ref[...]


def _ada(c, w_ada, b_ada):
    batch, d = c.shape
    n_out = w_ada.shape[1]
    return pl.pallas_call(
        _ada_kernel,
        grid=(n_out // d,),
        in_specs=[
            pl.BlockSpec((batch, d), lambda j: (0, 0)),
            pl.BlockSpec((d, d), lambda j: (0, j)),
            pl.BlockSpec((1, d), lambda j: (0, j)),
        ],
        out_specs=pl.BlockSpec((batch, d), lambda j: (0, j)),
        out_shape=jax.ShapeDtypeStruct((batch, n_out), _F32),
        name="ada",
    )(c, w_ada, b_ada.reshape(1, n_out))


def _mixer_kernel(alpha, steps, batch, d, n_s5_blocks, n_experts, region_rows,
                  x_ref, mod_ref, vecs_ref, w_in_ref, wg_ref, w_rnn_ref, s5a_ref, s5b_ref, s5c_ref,
                  w_glu_ref, w_out_ref, w_r_ref,
                  x1_ref, h2_ref, dest_ref, prob_ref, cnt_ref,
                  xt_s, xc_s, a_s, u_s, bu_s, u5_s, ya_s, ga_s, gb_s, h_state, s5_state, cnt_s):
    m = steps * batch
    halo = (CONV_WIDTH - 1) * batch
    s5w = n_s5_blocks * V7X_LANES
    n_blk = d // V7X_LANES
    bs = S5_BLOCK_STATES
    step = pl.program_id(0)

    @pl.when(step == 0)
    def _():
        xc_s[0:halo, :] = jnp.zeros((halo, d), _F32)
        h_state[...] = jnp.zeros_like(h_state)
        s5_state[...] = jnp.zeros_like(s5_state)
        cnt_s[...] = jnp.zeros_like(cnt_s)
        for ref in (xt_s, u_s, bu_s, u5_s, ya_s, ga_s, gb_s):
            ref[...] = jnp.zeros_like(ref)

    def mod(k):
        return mod_ref[:, k * d:(k + 1) * d]

    slot = lax.rem(step, 2)
    for b in range(batch):
        for j in range(n_blk):
            xt_s[slot, j, pl.ds(b, steps, stride=batch), :] = x_ref[b, :, j * V7X_LANES:(j + 1) * V7X_LANES]
    x = jnp.concatenate([xt_s[slot, j] for j in range(n_blk)], axis=1)
    hb = (x * _rows(1.0 + mod(1), steps) + _rows(mod(0), steps)).astype(_BF16)

    def vec(k, width=d):
        return vecs_ref[k:k + 1, 0:width]

    def in_proj(c0, width, bias_row):
        return _dot(hb, w_in_ref[:, c0:c0 + width]) + vec(bias_row, width)

    c0 = 2 * d
    c1 = c0 + s5w
    branch_a = _dot((ya_s[...] * u_s[...]).astype(_BF16), w_rnn_ref[...])
    y5 = jnp.concatenate(
        [_dot(bu_s[:, 2 * bs * j:2 * bs * (j + 1)].astype(_BF16), s5c_ref[j]) for j in range(n_s5_blocks)],
        axis=1) + vec(_VEC_S5_D, s5w) * u5_s[...]
    xc_s[halo:halo + m, :] = in_proj(0, d, _VEC_B_X)
    glu = _dot(jax.nn.gelu(y5).astype(_BF16), w_glu_ref[...])
    xr = jnp.zeros((m, d), _F32) + vec(_VEC_CONV_B)
    for k in range(CONV_WIDTH):
        xr = xr + vec(_VEC_CONV_W + k) * xc_s[k * batch:k * batch + m, :]
    xc_s[0:halo, :] = xc_s[m:m + halo, :]
    xrb = xr.astype(_BF16)
    gates = [_dot(xrb[:, j * V7X_LANES:(j + 1) * V7X_LANES], wg_ref[j]) for j in range(n_blk)]
    merged = (ga_s[...] * branch_a + gb_s[...] * (glu[:, :d] * _sigmoid(glu[:, d:]))).astype(_BF16)
    u5 = in_proj(c0, s5w, _VEC_B_U5)
    u5_s[...] = u5
    r_gate = _sigmoid(jnp.concatenate([g[:, :V7X_LANES] for g in gates], axis=1) + vec(_VEC_B_RG_A))
    i_gate = _sigmoid(jnp.concatenate([g[:, V7X_LANES:] for g in gates], axis=1) + vec(_VEC_B_RG_X))
    a = jnp.exp(vec(_VEC_LAMC) * r_gate)
    a_s[...] = a
    z = 1.0 - a * a
    u_s[...] = jnp.where(z > 0.0, z * lax.rsqrt(z), 0.0) * (i_gate * xr)
    mix = _dot(merged, w_out_ref[...])
    u5b = u5.astype(_BF16)
    for j in range(n_s5_blocks):
        bu_s[:, 2 * bs * j:2 * bs * (j + 1)] = _dot(u5b[:, j * V7X_LANES:(j + 1) * V7X_LANES], s5b_ref[j])
    hc = h_state[...]
    s5c = [(s5_state[:, 2 * bs * j:2 * bs * j + bs], s5_state[:, 2 * bs * j + bs:2 * bs * (j + 1)])
           for j in range(n_s5_blocks)]
    for t in range(steps):
        r0 = t * batch
        hc = a_s[r0:r0 + batch, :] * hc + u_s[r0:r0 + batch, :]
        u_s[r0:r0 + batch, :] = hc
        for j in range(n_s5_blocks):
            re0, im0 = 2 * bs * j, 2 * bs * j + bs
            re, im = s5c[j]
            ar, ai = s5a_ref[0, j], s5a_ref[1, j]
            nre = ar * re - ai * im + bu_s[r0:r0 + batch, re0:re0 + bs]
            nim = ar * im + ai * re + bu_s[r0:r0 + batch, im0:im0 + bs]
            bu_s[r0:r0 + batch, re0:re0 + bs] = nre
            bu_s[r0:r0 + batch, im0:im0 + bs] = nim
            s5c[j] = (nre, nim)
    h_state[...] = hc
    for j in range(n_s5_blocks):
        s5_state[:, 2 * bs * j:2 * bs * j + bs] = s5c[j][0]
        s5_state[:, 2 * bs * j + bs:2 * bs * (j + 1)] = s5c[j][1]
    x_prev = jnp.concatenate([xt_s[1 - slot, j] for j in range(n_blk)], axis=1)
    x1 = _layer_norm(alpha * x_prev + _rows(1.0 + mod(2), steps) * mix, vec(_VEC_LN_G), vec(_VEC_LN_B))
    x1_ref[...] = x1
    h2 = x1 * _rows(1.0 + mod(4), steps) + _rows(mod(3), steps)
    h2b = h2.astype(_BF16)
    h2_ref[...] = _pack_rows(h2)
    ya_s[...] = jax.nn.gelu(in_proj(d, d, _VEC_B_Y))
    lane = lax.broadcasted_iota(jnp.int32, (m, V7X_LANES), 1)
    lane_f = lane.astype(_F32)
    neg_inf = jnp.float32(-jnp.inf)
    logits = jnp.where(lane < n_experts, _dot(h2b, w_r_ref[...]) + vec(_VEC_B_ROUTER, V7X_LANES), neg_inf)
    ga_s[...] = _sigmoid(in_proj(c1, d, _VEC_B_GA))
    gb_s[...] = _sigmoid(in_proj(c1 + d, d, _VEC_B_GB))
    onehot = jnp.zeros((m, V7X_LANES), _F32)
    picks, vals = [], []
    for _ in range(TOP_K):
        v = jnp.max(logits, axis=-1, keepdims=True)
        p = jnp.min(jnp.where(logits == v, lane_f, float(V7X_LANES)), axis=-1, keepdims=True)
        hit = lane_f == p
        onehot = jnp.where(hit, 1.0, onehot)
        logits = jnp.where(hit, neg_inf, logits)
        picks.append(p)
        vals.append(v)
    exps = [jnp.exp(v - vals[0]) for v in vals]
    inv_den = 1.0 / functools.reduce(lambda s, e: s + e, exps)
    row = lax.broadcasted_iota(jnp.int32, (m, m), 0)
    col = lax.broadcasted_iota(jnp.int32, (m, m), 1)
    earlier = jnp.where(col < row, 1.0, 0.0).astype(_BF16)
    before = _dot(earlier, onehot.astype(_BF16)) + cnt_s[0:1, :]
    prob_out = jnp.zeros((m, V7X_LANES), _F32)
    dest_out = jnp.zeros((m, V7X_LANES), _F32)
    for k in range(TOP_K):
        rank_k = jnp.sum(jnp.where(lane_f == picks[k], before, 0.0), axis=-1, keepdims=True)
        prob_out = jnp.where(lane == k, exps[k] * inv_den, prob_out)
        dest_out = jnp.where(lane == k, picks[k] * float(region_rows) + rank_k, dest_out)
    prob_ref[...] = prob_out
    dest_t = dest_out.T[0:V7X_SUBLANES, :].astype(jnp.int32)
    for j in range(m // V7X_LANES):
        dest_ref[j] = dest_t[:, j * V7X_LANES:(j + 1) * V7X_LANES]
    has_prev = jnp.where(step > 0, 1.0, 0.0)
    cnt_new = cnt_s[...] + has_prev * jnp.sum(onehot, axis=0, keepdims=True)
    cnt_s[...] = cnt_new
    cnt_ref[...] = cnt_new


def _mixer(x, mod, p, *, alpha, n_experts):
    batch, seq, d = x.shape
    tokens = batch * seq
    steps = MIXER_STEPS
    m = steps * batch
    n_chunks = seq // steps
    n_s5_blocks = p["s5b"].shape[0]
    s5_lanes = n_s5_blocks * 2 * S5_BLOCK_STATES
    halo = (CONV_WIDTH - 1) * batch

    def const(a):
        nd = a.ndim
        return pl.BlockSpec(a.shape, lambda i, nd=nd: (0,) * nd, pipeline_mode=pl.Buffered(1))

    weights = [p["vecs"], p["w_in"], p["wg"], p["w_rnn"], p["s5a"], p["s5b"], p["s5c"], p["w_glu"], p["w_out"],
               p["w_r"]]
    prev = lambda i: jnp.maximum(i - 1, 0)
    row_spec = lambda width: pl.BlockSpec((m, width), lambda i: (prev(i), 0))
    chunks = m // V7X_LANES
    out_shape = (
        jax.ShapeDtypeStruct((tokens, d), _F32),
        jax.ShapeDtypeStruct((tokens, d // 2), jnp.int32),
        jax.ShapeDtypeStruct((tokens // V7X_LANES, V7X_SUBLANES, V7X_LANES), jnp.int32),
        jax.ShapeDtypeStruct((tokens, V7X_LANES), _F32),
        jax.ShapeDtypeStruct((V7X_SUBLANES, V7X_LANES), _F32),
    )
    act = pltpu.VMEM((m, d), _F32)
    scratch = [
        pltpu.VMEM((2, d // V7X_LANES, m, V7X_LANES), _F32),
        pltpu.VMEM((m + halo, d), _F32),
        act, act,
        pltpu.VMEM((m, s5_lanes), _F32),
        pltpu.VMEM((m, n_s5_blocks * V7X_LANES), _F32),
        act, act, act,
        pltpu.VMEM((batch, d), _F32),
        pltpu.VMEM((batch, s5_lanes), _F32),
        pltpu.VMEM((V7X_SUBLANES, V7X_LANES), _F32),
    ]
    weight_bytes = sum(w.size * w.dtype.itemsize for w in weights)
    act_bytes = m * d * 4
    vmem = weight_bytes + 32 * act_bytes
    kern = functools.partial(_mixer_kernel, alpha, steps, batch, d, n_s5_blocks, n_experts, tokens)
    return pl.pallas_call(
        kern,
        grid=(n_chunks + 1,),
        in_specs=[pl.BlockSpec((batch, steps, d), lambda i: (0, jnp.minimum(i, n_chunks - 1), 0)),
                  const(mod)] + [const(w) for w in weights],
        out_specs=(row_spec(d), row_spec(d // 2),
                   pl.BlockSpec((chunks, V7X_SUBLANES, V7X_LANES), lambda i: (prev(i), 0, 0)),
                   row_spec(V7X_LANES),
                   pl.BlockSpec((V7X_SUBLANES, V7X_LANES), lambda i: (0, 0))),
        out_shape=out_shape,
        scratch_shapes=scratch,
        compiler_params=pltpu.CompilerParams(dimension_semantics=("arbitrary",),
                                             vmem_limit_bytes=_vmem_limit(vmem)),
        name="mixer",
    )(x, mod, *weights)


def _expert_kernel(d_ff, tile_e_ref, tile_blk_ref, tile_rows_ref, tile_first_ref, tile_slot_ref, tile_next_ref,
                   x_ref, wgu_hbm, bgu_ref, wd_hbm, bd_ref, y_ref, wgu_f, wd_f, wgu_s, wd_s, sems):
    i = pl.program_id(0)
    d = wgu_f.shape[1]
    r = x_ref.shape[0]

    def weight_copies(expert, slot):
        return (pltpu.make_async_copy(wgu_hbm.at[expert], wgu_f.at[slot], sems.at[slot, 0]),
                pltpu.make_async_copy(wd_hbm.at[expert], wd_f.at[slot], sems.at[slot, 1]))

    @pl.when(tile_first_ref[i] == 1)
    def _():
        slot = tile_slot_ref[i]

        @pl.when(i == 0)
        def _():
            for cp in weight_copies(tile_e_ref[i], slot):
                cp.start()

        for cp in weight_copies(tile_e_ref[i], slot):
            cp.wait()

        @pl.when(tile_next_ref[i] >= 0)
        def _():
            for cp in weight_copies(tile_next_ref[i], 1 - slot):
                cp.start()

        def cast(c, carry):
            r0 = pl.multiple_of(c * WEIGHT_CAST_ROWS, WEIGHT_CAST_ROWS)
            wgu_s[pl.ds(r0, WEIGHT_CAST_ROWS), :] = wgu_f[slot, pl.ds(r0, WEIGHT_CAST_ROWS), :].astype(_BF16)
            wd_s[pl.ds(r0, WEIGHT_CAST_ROWS), :] = wd_f[slot, pl.ds(r0, WEIGHT_CAST_ROWS), :].astype(_BF16)
            return carry

        lax.fori_loop(0, d // WEIGHT_CAST_ROWS, cast, 0)

    def mlp(rows):
        lo, hi = _unpack_rows(x_ref[0:rows, :])
        x = jnp.concatenate([lo, hi], axis=1).astype(_BF16)
        gu = _dot(x, wgu_s[...]) + bgu_ref[0]
        gate = jnp.minimum(gu[:, :d_ff], SWIGLU_LIMIT)
        up = jnp.clip(gu[:, d_ff:], -SWIGLU_LIMIT, SWIGLU_LIMIT)
        act = gate * _sigmoid(SWIGLU_ALPHA * gate) * (up + 1.0)
        y_ref[0:rows, :] = _pack_rows(_dot(act.astype(_BF16), wd_s[...]) + bd_ref[0])
        if rows < r:
            y_ref[rows:r, :] = jnp.zeros((r - rows, y_ref.shape[1]), y_ref.dtype)

    @pl.when(tile_rows_ref[i] == r)
    def _():
        mlp(r)

    @pl.when(tile_rows_ref[i] == r // 2)
    def _():
        mlp(r // 2)

    @pl.when(tile_rows_ref[i] == 0)
    def _():
        y_ref[...] = jnp.zeros_like(y_ref)


def _experts(xb, schedule, w_gu, b_gu, w_down, b_down):
    n_rows, half = xb.shape
    n_experts, d, two_ff = w_gu.shape
    d_ff = two_ff // 2
    assert d_ff == d, "the weight cast loop walks w_gu and w_down rows together"
    r = EXPERT_ROWS
    vmem = 2 * (d * two_ff + d_ff * d) * 4 + (d * two_ff + d_ff * d) * 2 + 8 * r * half * 4 + 6 * r * two_ff * 4
    n_sched = len(schedule)
    tile = lambda i, *s: (s[1][i], 0)
    expert = lambda i, *s: (s[0][i], 0, 0)
    grid_spec = pltpu.PrefetchScalarGridSpec(
        num_scalar_prefetch=n_sched,
        grid=(schedule[0].shape[0],),
        in_specs=[
            pl.BlockSpec((r, half), tile),
            pl.BlockSpec(memory_space=pl.ANY),
            pl.BlockSpec((1, 1, two_ff), expert),
            pl.BlockSpec(memory_space=pl.ANY),
            pl.BlockSpec((1, 1, d), expert),
        ],
        out_specs=pl.BlockSpec((r, half), tile),
        scratch_shapes=[pltpu.VMEM((2, d, two_ff), _F32), pltpu.VMEM((2, d_ff, d), _F32),
                        pltpu.VMEM((d, two_ff), _BF16), pltpu.VMEM((d_ff, d), _BF16),
                        pltpu.SemaphoreType.DMA((2, 2))],
    )
    return pl.pallas_call(
        functools.partial(_expert_kernel, d_ff),
        grid_spec=grid_spec,
        out_shape=jax.ShapeDtypeStruct((n_rows, half), jnp.int32),
        compiler_params=pltpu.CompilerParams(dimension_semantics=("arbitrary",),
                                             vmem_limit_bytes=_vmem_limit(vmem)),
        name="experts",
    )(*schedule, xb, w_gu, b_gu.reshape(n_experts, 1, two_ff), w_down, b_down.reshape(n_experts, 1, d))


def _sc_workers():
    info = plsc.get_sparse_core_info()
    return info.num_cores, info.num_subcores


def _dispatch(h2w, dest_c, n_rows):
    tokens, width = h2w.shape
    n_chunks, _, chunk = dest_c.shape
    nc, ns = _sc_workers()
    per_w = n_chunks // (nc * ns)
    assert per_w * nc * ns == n_chunks

    @functools.partial(
        pl.kernel, mesh=plsc.VectorSubcoreMesh(core_axis_name="c", subcore_axis_name="s"),
        out_type=jax.ShapeDtypeStruct((n_rows, width), h2w.dtype),
        scratch_types=[pltpu.VMEM(dest_c.shape[1:], jnp.int32), pltpu.VMEM((chunk, width), h2w.dtype)],
    )
    def scatter_rows(h_hbm, d_hbm, o_hbm, idx_v, rows_v):
        wid = lax.axis_index("s") * nc + lax.axis_index("c")

        @pl.loop(0, per_w)
        def _(j):
            blk = wid * per_w + j
            pltpu.sync_copy(d_hbm.at[blk], idx_v)
            pltpu.sync_copy(h_hbm.at[pl.ds(pl.multiple_of(blk * chunk, chunk), chunk)], rows_v)
            for k in range(TOP_K):
                pltpu.sync_copy(rows_v, o_hbm.at[idx_v.at[k]])

    return scatter_rows(h2w, dest_c)


def _collect(yb, dest_c):
    _, width = yb.shape
    n_chunks, _, chunk = dest_c.shape
    nc, ns = _sc_workers()
    per_w = n_chunks // (nc * ns)
    assert per_w * nc * ns == n_chunks

    @functools.partial(
        pl.kernel, mesh=plsc.VectorSubcoreMesh(core_axis_name="c", subcore_axis_name="s"),
        out_type=jax.ShapeDtypeStruct((TOP_K, n_chunks * chunk, width), yb.dtype),
        scratch_types=[pltpu.VMEM(dest_c.shape[1:], jnp.int32), pltpu.VMEM((chunk, width), yb.dtype)],
    )
    def gather_rows(y_hbm, d_hbm, o_hbm, idx_v, rows_v):
        wid = lax.axis_index("s") * nc + lax.axis_index("c")

        @pl.loop(0, per_w)
        def _(j):
            blk = wid * per_w + j
            pltpu.sync_copy(d_hbm.at[blk], idx_v)
            for k in range(TOP_K):
                pltpu.sync_copy(y_hbm.at[idx_v.at[k]], rows_v)
                pltpu.sync_copy(rows_v, o_hbm.at[k, pl.ds(pl.multiple_of(blk * chunk, chunk), chunk)])

    return gather_rows(yb, dest_c)


def _combine_kernel(alpha, steps, x1_ref, yg_ref, prob_ref, mod_ref, ln_g_ref, ln_b_ref, *rest):
    o_ref, ot_s = rest[-2:]
    d = x1_ref.shape[1]
    batch = o_ref.shape[0]
    ffn_lo = jnp.zeros((x1_ref.shape[0], d // 2), _F32)
    ffn_hi = jnp.zeros((x1_ref.shape[0], d // 2), _F32)
    for k in range(TOP_K):
        lo, hi = _unpack_rows(yg_ref[k])
        ffn_lo = ffn_lo + prob_ref[:, k:k + 1] * lo
        ffn_hi = ffn_hi + prob_ref[:, k:k + 1] * hi
    ffn = jnp.concatenate([ffn_lo, ffn_hi], axis=1)
    gate = _rows(1.0 + mod_ref[:, 5 * d:6 * d], steps)
    out = _layer_norm(alpha * x1_ref[...] + gate * ffn, ln_g_ref[...], ln_b_ref[...])
    n_blk = d // V7X_LANES
    for j in range(n_blk):
        ot_s[j] = out[:, j * V7X_LANES:(j + 1) * V7X_LANES]
    for b in range(batch):
        for j in range(n_blk):
            o_ref[b, :, j * V7X_LANES:(j + 1) * V7X_LANES] = ot_s[j, pl.ds(b, steps, stride=batch), :]


def _combine(x1, yg, prob, mod, ln_g, ln_b, out_prev, *, alpha, batch, first_block):
    tokens, d = x1.shape
    rows = COMBINE_ROWS
    steps = rows // batch
    const = lambda a: pl.BlockSpec(a.shape, lambda i: (0, 0))
    in_specs = [
        pl.BlockSpec((rows, d), lambda i: (first_block + i, 0)),
        pl.BlockSpec((TOP_K, rows, d // 2), lambda i: (0, i, 0)),
        pl.BlockSpec((rows, V7X_LANES), lambda i: (first_block + i, 0)),
        const(mod), const(ln_g), const(ln_b),
    ]
    operands = [x1, yg, prob, mod, ln_g, ln_b]
    aliases = {}
    if out_prev is not None:
        aliases = {len(operands): 0}
        in_specs.append(pl.BlockSpec(memory_space=pl.ANY))
        operands.append(out_prev)
    return pl.pallas_call(
        functools.partial(_combine_kernel, alpha, steps),
        grid=(yg.shape[1] // rows,),
        in_specs=in_specs,
        out_specs=pl.BlockSpec((batch, steps, d), lambda i: (0, first_block + i, 0)),
        out_shape=jax.ShapeDtypeStruct((batch, tokens // batch, d), _F32),
        scratch_shapes=[pltpu.VMEM((d // V7X_LANES, rows, V7X_LANES), _F32)],
        input_output_aliases=aliases,
        compiler_params=pltpu.CompilerParams(dimension_semantics=("parallel",)),
        name="combine",
    )(*operands)


def _block_diag(blocks):
    nb, n, a, b = blocks.shape
    eye = jnp.eye(n, dtype=blocks.dtype)
    return (eye[None, :, None, :, None] * blocks[:, :, :, None, :]).reshape(nb, n * a, n * b)


def _s5_params(lam_re, lam_im, log_dt, b_re, b_im, c_re, c_im):
    groups = lam_re.shape[0]
    nb = groups // S5_BLOCK_GROUPS
    dt = jnp.exp(log_dt)[:, None]
    mag = jnp.exp(lam_re * dt)
    ab_re, ab_im = mag * jnp.cos(lam_im * dt), mag * jnp.sin(lam_im * dt)
    den = lam_re * lam_re + lam_im * lam_im
    q_re = ((ab_re - 1.0) * lam_re + ab_im * lam_im) / den
    q_im = (ab_im * lam_re - (ab_re - 1.0) * lam_im) / den
    bb_re = q_re[..., None] * b_re - q_im[..., None] * b_im
    bb_im = q_re[..., None] * b_im + q_im[..., None] * b_re

    def per_block(a):
        return jnp.swapaxes(a.reshape(nb, S5_BLOCK_GROUPS, *a.shape[1:]), 2, 3)

    bmat = jnp.concatenate([_block_diag(per_block(bb_re)), _block_diag(per_block(bb_im))], axis=2)
    cmat = jnp.concatenate([_block_diag(per_block(c_re)), -_block_diag(per_block(c_im))], axis=1)
    s5a = jnp.broadcast_to(jnp.stack([ab_re, ab_im]).reshape(2, nb, 1, S5_BLOCK_STATES),
                           (2, nb, V7X_SUBLANES, S5_BLOCK_STATES))
    return dict(s5a=s5a, s5b=bmat.astype(_BF16), s5c=cmat.astype(_BF16))


def _mixer_vectors(d, b_in, conv_w, conv_b, b_rg_a, b_rg_x, lru_lambda, ln_g, ln_b, s5_d, b_router):
    pad = lambda v: jnp.pad(v, (0, d - v.shape[0]))
    s5w = s5_d.size
    rows = [b_in[0:d], b_in[d:2 * d], pad(b_in[2 * d:2 * d + s5w]), b_in[2 * d + s5w:3 * d + s5w],
            b_in[3 * d + s5w:4 * d + s5w], *conv_w, conv_b, b_rg_a, b_rg_x,
            -LRU_C * jax.nn.softplus(-lru_lambda), ln_g, ln_b, pad(s5_d.reshape(-1)), pad(b_router)]
    rows += [jnp.zeros((d,), _F32)] * (-len(rows) % V7X_SUBLANES)
    return jnp.stack(rows)


def kernel(x, c, w_ada, b_ada, w_in, b_in, conv_w, conv_b, w_rg_a, b_rg_a, w_rg_x, b_rg_x, lru_lambda, w_rnn_out, s5_lambda_re, s5_lambda_im, s5_log_dt, s5_b_re, s5_b_im, s5_c_re, s5_c_im, s5_d, w_glu, w_out, ln1_g, ln1_b, w_router, b_router, w_gu, b_gu, w_down, b_down, ln2_g, ln2_b):
    batch, seq, d = x.shape
    depth = w_ada.shape[0]
    n_experts = w_router.shape[-1]
    tokens = batch * seq
    alpha = (2.0 * depth) ** 0.25
    assert batch == V7X_SUBLANES and d % V7X_LANES == 0 and n_experts <= V7X_LANES
    assert seq % MIXER_STEPS == 0 and tokens % (COMBINE_ROWS * COLLECT_PARTS) == 0 and tokens % EXPERT_ROWS == 0
    assert (MIXER_STEPS * batch) % V7X_LANES == 0

    for l in range(depth):
        mod = _ada(c, w_ada[l], b_ada[l])
        p = dict(
            vecs=_mixer_vectors(d, b_in[l], conv_w[l], conv_b[l], b_rg_a[l], b_rg_x[l], lru_lambda[l],
                                ln1_g[l], ln1_b[l], s5_d[l], b_router[l]),
            w_in=w_in[l].astype(_BF16),
            wg=jnp.concatenate([w_rg_a[l], w_rg_x[l]], axis=-1).astype(_BF16),
            w_rnn=w_rnn_out[l].astype(_BF16), w_glu=w_glu[l].astype(_BF16), w_out=w_out[l].astype(_BF16),
            w_r=jnp.pad(w_router[l], ((0, 0), (0, V7X_LANES - n_experts))).astype(_BF16),
            **_s5_params(s5_lambda_re[l], s5_lambda_im[l], s5_log_dt[l], s5_b_re[l], s5_b_im[l],
                         s5_c_re[l], s5_c_im[l]),
        )
        x1, h2w, dest_c, prob, cnt = _mixer(x, mod, p, alpha=alpha, n_experts=n_experts)

        r = EXPERT_ROWS
        blocks_per_region = tokens // r
        spare_blk = n_experts * blocks_per_region
        n_tiles = -(-(tokens * TOP_K + n_experts * (r - 1)) // r)
        counts = cnt[0, :n_experts].astype(jnp.int32)
        tiles_e = (counts + r - 1) // r
        tile_end = jnp.cumsum(tiles_e)
        n_used = tile_end[-1:]
        t_ids = jnp.arange(n_tiles, dtype=jnp.int32)
        done = (tile_end[None, :] <= t_ids[:, None]).astype(jnp.int32)
        tile_e = jnp.minimum(jnp.sum(done, axis=1), n_experts - 1)
        first_tile = jnp.sum(done * tiles_e[None, :], axis=1)
        used = t_ids < n_used
        tile_blk = jnp.where(used, tile_e * blocks_per_region + t_ids - first_tile, spare_blk)
        e_ids = jnp.arange(n_experts, dtype=jnp.int32)[None, :]
        own = (e_ids == tile_e[:, None]).astype(jnp.int32)
        valid = jnp.sum(own * counts[None, :], axis=1) - (t_ids - first_tile) * r
        tile_rows = jnp.where(used, jnp.where(valid <= r // 2, r // 2, r), 0)
        has_tiles = (tiles_e > 0).astype(jnp.int32)[None, :]
        tile_first = (used & (t_ids == first_tile)).astype(jnp.int32)
        tile_slot = jnp.sum(done * has_tiles, axis=1) % 2
        later = jnp.where((has_tiles > 0) & (e_ids > tile_e[:, None]), e_ids, n_experts)
        tile_next = jnp.min(later, axis=1)
        tile_next = jnp.where(tile_next < n_experts, tile_next, -1)

        xb = _dispatch(h2w, dest_c, (spare_blk + 1) * r)
        yb = _experts(xb, (tile_e, tile_blk, tile_rows, tile_first, tile_slot, tile_next),
                      w_gu[l], b_gu[l], w_down[l], b_down[l])
        chunks_per_part = dest_c.shape[0] // COLLECT_PARTS
        blocks_per_part = tokens // COMBINE_ROWS // COLLECT_PARTS
        x = None
        for part in range(COLLECT_PARTS):
            yg = _collect(yb, dest_c[part * chunks_per_part:(part + 1) * chunks_per_part])
            x = _combine(x1, yg, prob, mod, ln2_g[l].reshape(1, -1), ln2_b[l].reshape(1, -1), x,
                         alpha=alpha, batch=batch, first_block=part * blocks_per_part)
    return x
```

```python
import functools

import jax
import jax.numpy as jnp
from jax import lax
from jax.experimental import pallas as pl
from jax.experimental.pallas import tpu as pltpu
from jax.experimental.pallas import tpu_sc as plsc

V7X_SUBLANES = 8
V7X_LANES = 128
V7X_VMEM_BYTES = 64 * 1024 * 1024

CONV_WIDTH = 4
LRU_C = 8.0
S5_GROUP = 16
S5_STATE = 64
TOP_K = 4
SWIGLU_LIMIT = 7.0
SWIGLU_ALPHA = 1.702
LN_EPS = 1e-5

S5_BLOCK_GROUPS = V7X_LANES // S5_GROUP
S5_BLOCK_STATES = S5_BLOCK_GROUPS * S5_STATE

(_VEC_B_X, _VEC_B_Y, _VEC_B_U5, _VEC_B_GA, _VEC_B_GB, _VEC_CONV_W) = range(6)
(_VEC_CONV_B, _VEC_B_RG_A, _VEC_B_RG_X, _VEC_LAMC, _VEC_LN_G, _VEC_LN_B, _VEC_S5_D, _VEC_B_ROUTER) = range(
    _VEC_CONV_W + CONV_WIDTH, _VEC_CONV_W + CONV_WIDTH + 8)

MIXER_STEPS = 32
EXPERT_ROWS = 512
COMBINE_ROWS = 1024
WEIGHT_CAST_ROWS = 64

_BF16 = jnp.bfloat16
_F32 = jnp.float32


def _dot(a, b):
    return jnp.dot(a, b, preferred_element_type=_F32)


def _sigmoid(v):
    return 0.5 * jnp.tanh(0.5 * v) + 0.5


def _vmem_limit(nbytes):
    return int(min(nbytes, V7X_VMEM_BYTES - 4 * 1024 * 1024))


def _layer_norm(z, gain, bias):
    mu = jnp.mean(z, axis=-1, keepdims=True)
    zc = z - mu
    var = jnp.mean(zc * zc, axis=-1, keepdims=True)
    return zc * lax.rsqrt(var + LN_EPS) * gain + bias


def _rows(v, steps):
    return jnp.tile(v, (steps, 1))


_HI_MASK = 0xFFFF0000


def _pack_rows(v):
    half = v.shape[1] // 2
    bits = lax.bitcast_convert_type(v.astype(_BF16).astype(_F32), jnp.uint32)
    packed = (bits[:, :half] >> 16) | (bits[:, half:] & jnp.uint32(_HI_MASK))
    return lax.bitcast_convert_type(packed, jnp.int32)


def _unpack_rows(w):
    bits = lax.bitcast_convert_type(w, jnp.uint32)
    lo = lax.bitcast_convert_type(bits << 16, _F32)
    hi = lax.bitcast_convert_type(bits & jnp.uint32(_HI_MASK), _F32)
    return lo, hi


def _ada_kernel(c_ref, w_ref, b_ref, o_ref):
    c = c_ref[...]
    c_act = (c * _sigmoid(c)).astype(_BF16)
    o_ref[...] = _dot(c_act, w_ref[...].astype(_BF16)) + b_ref[...]


def _ada(c, w_ada, b_ada):
    batch, d = c.shape
    n_out = w_ada.shape[1]
    return pl.pallas_call(
        _ada_kernel,
        grid=(n_out // d,),
        in_specs=[
            pl.BlockSpec((batch, d), lambda j: (0, 0)),
            pl.BlockSpec((d, d), lambda j: (0, j)),
            pl.BlockSpec((1, d), lambda j: (0, j)),
        ],
        out_specs=pl.BlockSpec((batch, d), lambda j: (0, j)),
        out_shape=jax.ShapeDtypeStruct((batch, n_out), _F32),
        name="ada",
    )(c, w_ada, b_ada.reshape(1, n_out))


def _mixer_kernel(alpha, steps, batch, d, n_s5_blocks, n_experts, region_rows,
                  x_ref, mod_ref, vecs_ref, w_in_ref, wg_ref, w_rnn_ref, s5a_ref, s5b_ref, s5c_ref,
                  w_glu_ref, w_out_ref, w_r_ref,
                  x1_ref, h2_ref, dest_ref, prob_ref, cnt_ref,
                  xt_s, xc_s, a_s, u_s, bu_s, u5_s, ya_s, ga_s, gb_s, h_state, s5_state, cnt_s):
    m = steps * batch
    halo = (CONV_WIDTH - 1) * batch
    s5w = n_s5_blocks * V7X_LANES
    n_blk = d // V7X_LANES
    bs = S5_BLOCK_STATES
    step = pl.program_id(0)

    @pl.when(step == 0)
    def _():
        xc_s[0:halo, :] = jnp.zeros((halo, d), _F32)
        h_state[...] = jnp.zeros_like(h_state)
        s5_state[...] = jnp.zeros_like(s5_state)
        cnt_s[...] = jnp.zeros_like(cnt_s)
        for ref in (xt_s, u_s, bu_s, u5_s, ya_s, ga_s, gb_s):
            ref[...] = jnp.zeros_like(ref)

    def mod(k):
        return mod_ref[:, k * d:(k + 1) * d]

    slot = lax.rem(step, 2)
    for b in range(batch):
        for j in range(n_blk):
            xt_s[slot, j, pl.ds(b, steps, stride=batch), :] = x_ref[b, :, j * V7X_LANES:(j + 1) * V7X_LANES]
    x = jnp.concatenate([xt_s[slot, j] for j in range(n_blk)], axis=1)
    hb = (x * _rows(1.0 + mod(1), steps) + _rows(mod(0), steps)).astype(_BF16)

    def vec(k, width=d):
        return vecs_ref[k:k + 1, 0:width]

    def in_proj(c0, width, bias_row):
        return _dot(hb, w_in_ref[:, c0:c0 + width]) + vec(bias_row, width)

    c0 = 2 * d
    c1 = c0 + s5w
    branch_a = _dot((ya_s[...] * u_s[...]).astype(_BF16), w_rnn_ref[...])
    y5 = jnp.concatenate(
        [_dot(bu_s[:, 2 * bs * j:2 * bs * (j + 1)].astype(_BF16), s5c_ref[j]) for j in range(n_s5_blocks)],
        axis=1) + vec(_VEC_S5_D, s5w) * u5_s[...]
    xc_s[halo:halo + m, :] = in_proj(0, d, _VEC_B_X)
    glu = _dot(jax.nn.gelu(y5).astype(_BF16), w_glu_ref[...])
    xr = jnp.zeros((m, d), _F32) + vec(_VEC_CONV_B)
    for k in range(CONV_WIDTH):
        xr = xr + vec(_VEC_CONV_W + k) * xc_s[k * batch:k * batch + m, :]
    xc_s[0:halo, :] = xc_s[m:m + halo, :]
    xrb = xr.astype(_BF16)
    gates = [_dot(xrb[:, j * V7X_LANES:(j + 1) * V7X_LANES], wg_ref[j]) for j in range(n_blk)]
    merged = (ga_s[...] * branch_a + gb_s[...] * (glu[:, :d] * _sigmoid(glu[:, d:]))).astype(_BF16)
    u5 = in_proj(c0, s5w, _VEC_B_U5)
    u5_s[...] = u5
    r_gate = _sigmoid(jnp.concatenate([g[:, :V7X_LANES] for g in gates], axis=1) + vec(_VEC_B_RG_A))
    i_gate = _sigmoid(jnp.concatenate([g[:, V7X_LANES:] for g in gates], axis=1) + vec(_VEC_B_RG_X))
    a = jnp.exp(vec(_VEC_LAMC) * r_gate)
    a_s[...] = a
    z = 1.0 - a * a
    u_s[...] = jnp.where(z > 0.0, z * lax.rsqrt(z), 0.0) * (i_gate * xr)
    mix = _dot(merged, w_out_ref[...])
    u5b = u5.astype(_BF16)
    for j in range(n_s5_blocks):
        bu_s[:, 2 * bs * j:2 * bs * (j + 1)] = _dot(u5b[:, j * V7X_LANES:(j + 1) * V7X_LANES], s5b_ref[j])
    hc = h_state[...]
    s5c = [(s5_state[:, 2 * bs * j:2 * bs * j + bs], s5_state[:, 2 * bs * j + bs:2 * bs * (j + 1)])
           for j in range(n_s5_blocks)]
    for t in range(steps):
        r0 = t * batch
        hc = a_s[r0:r0 + batch, :] * hc + u_s[r0:r0 + batch, :]
        u_s[r0:r0 + batch, :] = hc
        for j in range(n_s5_blocks):
            re0, im0 = 2 * bs * j, 2 * bs * j + bs
            re, im = s5c[j]
            ar, ai = s5a_ref[0, j], s5a_ref[1, j]
            nre = ar * re - ai * im + bu_s[r0:r0 + batch, re0:re0 + bs]
            nim = ar * im + ai * re + bu_s[r0:r0 + batch, im0:im0 + bs]
            bu_s[r0:r0 + batch, re0:re0 + bs] = nre
            bu_s[r0:r0 + batch, im0:im0 + bs] = nim
            s5c[j] = (nre, nim)
    h_state[...] = hc
    for j in range(n_s5_blocks):
        s5_state[:, 2 * bs * j:2 * bs * j + bs] = s5c[j][0]
        s5_state[:, 2 * bs * j + bs:2 * bs * (j + 1)] = s5c[j][1]
    x_prev = jnp.concatenate([xt_s[1 - slot, j] for j in range(n_blk)], axis=1)
    x1 = _layer_norm(alpha * x_prev + _rows(1.0 + mod(2), steps) * mix, vec(_VEC_LN_G), vec(_VEC_LN_B))
    x1_ref[...] = x1
    h2 = x1 * _rows(1.0 + mod(4), steps) + _rows(mod(3), steps)
    h2b = h2.astype(_BF16)
    h2_ref[...] = _pack_rows(h2)
    ya_s[...] = jax.nn.gelu(in_proj(d, d, _VEC_B_Y))
    lane = lax.broadcasted_iota(jnp.int32, (m, V7X_LANES), 1)
    lane_f = lane.astype(_F32)
    neg_inf = jnp.float32(-jnp.inf)
    logits = jnp.where(lane < n_experts, _dot(h2b, w_r_ref[...]) + vec(_VEC_B_ROUTER, V7X_LANES), neg_inf)
    ga_s[...] = _sigmoid(in_proj(c1, d, _VEC_B_GA))
    gb_s[...] = _sigmoid(in_proj(c1 + d, d, _VEC_B_GB))
    onehot = jnp.zeros((m, V7X_LANES), _F32)
    picks, vals = [], []
    for _ in range(TOP_K):
        v = jnp.max(logits, axis=-1, keepdims=True)
        p = jnp.min(jnp.where(logits == v, lane_f, float(V7X_LANES)), axis=-1, keepdims=True)
        hit = lane_f == p
        onehot = jnp.where(hit, 1.0, onehot)
        logits = jnp.where(hit, neg_inf, logits)
        picks.append(p)
        vals.append(v)
    exps = [jnp.exp(v - vals[0]) for v in vals]
    inv_den = 1.0 / functools.reduce(lambda s, e: s + e, exps)
    row = lax.broadcasted_iota(jnp.int32, (m, m), 0)
    col = lax.broadcasted_iota(jnp.int32, (m, m), 1)
    earlier = jnp.where(col < row, 1.0, 0.0).astype(_BF16)
    before = _dot(earlier, onehot.astype(_BF16)) + cnt_s[0:1, :]
    prob_out = jnp.zeros((m, V7X_LANES), _F32)
    dest_out = jnp.zeros((m, V7X_LANES), _F32)
    for k in range(TOP_K):
        rank_k = jnp.sum(jnp.where(lane_f == picks[k], before, 0.0), axis=-1, keepdims=True)
        prob_out = jnp.where(lane == k, exps[k] * inv_den, prob_out)
        dest_out = jnp.where(lane == k, picks[k] * float(region_rows) + rank_k, dest_out)
    prob_ref[...] = prob_out
    dest_t = dest_out.T[0:V7X_SUBLANES, :].astype(jnp.int32)
    for j in range(m // V7X_LANES):
        dest_ref[j] = dest_t[:, j * V7X_LANES:(j + 1) * V7X_LANES]
    has_prev = jnp.where(step > 0, 1.0, 0.0)
    cnt_new = cnt_s[...] + has_prev * jnp.sum(onehot, axis=0, keepdims=True)
    cnt_s[...] = cnt_new
    cnt_ref[...] = cnt_new


def _mixer(x, mod, p, *, alpha, n_experts):
    batch, seq, d = x.shape
    tokens = batch * seq
    steps = MIXER_STEPS
    m = steps * batch
    n_chunks = seq // steps
    n_s5_blocks = p["s5b"].shape[0]
    s5_lanes = n_s5_blocks * 2 * S5_BLOCK_STATES
    halo = (CONV_WIDTH - 1) * batch

    def const(a):
        nd = a.ndim
        return pl.BlockSpec(a.shape, lambda i, nd=nd: (0,) * nd, pipeline_mode=pl.Buffered(1))

    weights = [p["vecs"], p["w_in"], p["wg"], p["w_rnn"], p["s5a"], p["s5b"], p["s5c"], p["w_glu"], p["w_out"],
               p["w_r"]]
    prev = lambda i: jnp.maximum(i - 1, 0)
    row_spec = lambda width: pl.BlockSpec((m, width), lambda i: (prev(i), 0))
    chunks = m // V7X_LANES
    out_shape = (
        jax.ShapeDtypeStruct((tokens, d), _F32),
        jax.ShapeDtypeStruct((tokens, d // 2), jnp.int32),
        jax.ShapeDtypeStruct((tokens // V7X_LANES, V7X_SUBLANES, V7X_LANES), jnp.int32),
        jax.ShapeDtypeStruct((tokens, V7X_LANES), _F32),
        jax.ShapeDtypeStruct((V7X_SUBLANES, V7X_LANES), _F32),
    )
    act = pltpu.VMEM((m, d), _F32)
    scratch = [
        pltpu.VMEM((2, d // V7X_LANES, m, V7X_LANES), _F32),
        pltpu.VMEM((m + halo, d), _F32),
        act, act,
        pltpu.VMEM((m, s5_lanes), _F32),
        pltpu.VMEM((m, n_s5_blocks * V7X_LANES), _F32),
        act, act, act,
        pltpu.VMEM((batch, d), _F32),
        pltpu.VMEM((batch, s5_lanes), _F32),
        pltpu.VMEM((V7X_SUBLANES, V7X_LANES), _F32),
    ]
    weight_bytes = sum(w.size * w.dtype.itemsize for w in weights)
    act_bytes = m * d * 4
    vmem = weight_bytes + 32 * act_bytes
    kern = functools.partial(_mixer_kernel, alpha, steps, batch, d, n_s5_blocks, n_experts, tokens)
    return pl.pallas_call(
        kern,
        grid=(n_chunks + 1,),
        in_specs=[pl.BlockSpec((batch, steps, d), lambda i: (0, jnp.minimum(i, n_chunks - 1), 0)),
                  const(mod)] + [const(w) for w in weights],
        out_specs=(row_spec(d), row_spec(d // 2),
                   pl.BlockSpec((chunks, V7X_SUBLANES, V7X_LANES), lambda i: (prev(i), 0, 0)),
                   row_spec(V7X_LANES),
                   pl.BlockSpec((V7X_SUBLANES, V7X_LANES), lambda i: (0, 0))),
        out_shape=out_shape,
        scratch_shapes=scratch,
        compiler_params=pltpu.CompilerParams(dimension_semantics=("arbitrary",),
                                             vmem_limit_bytes=_vmem_limit(vmem)),
        name="mixer",
    )(x, mod, *weights)


def _expert_kernel(d_ff, tile_e_ref, tile_blk_ref, tile_rows_ref, tile_first_ref, tile_slot_ref, tile_next_ref,
                   x_ref, wgu_hbm, bgu_ref, wd_hbm, bd_ref, y_ref, wgu_f, wd_f, wgu_s, wd_s, sems):
    i = pl.program_id(0)
    d = wgu_f.shape[1]
    r = x_ref.shape[0]

    def weight_copies(expert, slot):
        return (pltpu.make_async_copy(wgu_hbm.at[expert], wgu_f.at[slot], sems.at[slot, 0]),
                pltpu.make_async_copy(wd_hbm.at[expert], wd_f.at[slot], sems.at[slot, 1]))

    @pl.when(tile_first_ref[i] == 1)
    def _():
        slot = tile_slot_ref[i]

        @pl.when(i == 0)
        def _():
            for cp in weight_copies(tile_e_ref[i], slot):
                cp.start()

        for cp in weight_copies(tile_e_ref[i], slot):
            cp.wait()

        @pl.when(tile_next_ref[i] >= 0)
        def _():
            for cp in weight_copies(tile_next_ref[i], 1 - slot):
                cp.start()

        def cast(c, carry):
            r0 = pl.multiple_of(c * WEIGHT_CAST_ROWS, WEIGHT_CAST_ROWS)
            wgu_s[pl.ds(r0, WEIGHT_CAST_ROWS), :] = wgu_f[slot, pl.ds(r0, WEIGHT_CAST_ROWS), :].astype(_BF16)
            wd_s[pl.ds(r0, WEIGHT_CAST_ROWS), :] = wd_f[slot, pl.ds(r0, WEIGHT_CAST_ROWS), :].astype(_BF16)
            return carry

        lax.fori_loop(0, d // WEIGHT_CAST_ROWS, cast, 0)

    def mlp(rows):
        lo, hi = _unpack_rows(x_ref[0:rows, :])
        x = jnp.concatenate([lo, hi], axis=1).astype(_BF16)
        gu = _dot(x, wgu_s[...]) + bgu_ref[0]
        gate = jnp.minimum(gu[:, :d_ff], SWIGLU_LIMIT)
        up = jnp.clip(gu[:, d_ff:], -SWIGLU_LIMIT, SWIGLU_LIMIT)
        act = gate * _sigmoid(SWIGLU_ALPHA * gate) * (up + 1.0)
        y_ref[0:rows, :] = _pack_rows(_dot(act.astype(_BF16), wd_s[...]) + bd_ref[0])
        if rows < r:
            y_ref[rows:r, :] = jnp.zeros((r - rows, y_ref.shape[1]), y_ref.dtype)

    @pl.when(tile_rows_ref[i] == r)
    def _():
        mlp(r)

    @pl.when(tile_rows_ref[i] == r // 2)
    def _():
        mlp(r // 2)

    @pl.when(tile_rows_ref[i] == 0)
    def _():
        y_ref[...] = jnp.zeros_like(y_ref)


def _experts(xb, schedule, w_gu, b_gu, w_down, b_down):
    n_rows, half = xb.shape
    n_experts, d, two_ff = w_gu.shape
    d_ff = two_ff // 2
    assert d_ff == d, "the weight cast loop walks w_gu and w_down rows together"
    r = EXPERT_ROWS
    vmem = 2 * (d * two_ff + d_ff * d) * 4 + (d * two_ff + d_ff * d) * 2 + 8 * r * half * 4 + 6 * r * two_ff * 4
    n_sched = len(schedule)
    tile = lambda i, *s: (s[1][i], 0)
    expert = lambda i, *s: (s[0][i], 0, 0)
    grid_spec = pltpu.PrefetchScalarGridSpec(
        num_scalar_prefetch=n_sched,
        grid=(schedule[0].shape[0],),
        in_specs=[
            pl.BlockSpec((r, half), tile),
            pl.BlockSpec(memory_space=pl.ANY),
            pl.BlockSpec((1, 1, two_ff), expert),
            pl.BlockSpec(memory_space=pl.ANY),
            pl.BlockSpec((1, 1, d), expert),
        ],
        out_specs=pl.BlockSpec((r, half), tile),
        scratch_shapes=[pltpu.VMEM((2, d, two_ff), _F32), pltpu.VMEM((2, d_ff, d), _F32),
                        pltpu.VMEM((d, two_ff), _BF16), pltpu.VMEM((d_ff, d), _BF16),
                        pltpu.SemaphoreType.DMA((2, 2))],
    )
    return pl.pallas_call(
        functools.partial(_expert_kernel, d_ff),
        grid_spec=grid_spec,
        out_shape=jax.ShapeDtypeStruct((n_rows, half), jnp.int32),
        compiler_params=pltpu.CompilerParams(dimension_semantics=("arbitrary",),
                                             vmem_limit_bytes=_vmem_limit(vmem)),
        name="experts",
    )(*schedule, xb, w_gu, b_gu.reshape(n_experts, 1, two_ff), w_down, b_down.reshape(n_experts, 1, d))


def _sc_workers():
    info = plsc.get_sparse_core_info()
    return info.num_cores, info.num_subcores


def _dispatch(h2w, dest_c, n_rows):
    tokens, width = h2w.shape
    n_chunks, _, chunk = dest_c.shape
    nc, ns = _sc_workers()
    per_w = n_chunks // (nc * ns)
    assert per_w * nc * ns == n_chunks

    @functools.partial(
        pl.kernel, mesh=plsc.VectorSubcoreMesh(core_axis_name="c", subcore_axis_name="s"),
        out_type=jax.ShapeDtypeStruct((n_rows, width), h2w.dtype),
        scratch_types=[pltpu.VMEM(dest_c.shape[1:], jnp.int32), pltpu.VMEM((chunk, width), h2w.dtype)],
    )
    def scatter_rows(h_hbm, d_hbm, o_hbm, idx_v, rows_v):
        wid = lax.axis_index("s") * nc + lax.axis_index("c")

        @pl.loop(0, per_w)
        def _(j):
            blk = wid * per_w + j
            pltpu.sync_copy(d_hbm.at[blk], idx_v)
            pltpu.sync_copy(h_hbm.at[pl.ds(pl.multiple_of(blk * chunk, chunk), chunk)], rows_v)
            for k in range(TOP_K):
                pltpu.sync_copy(rows_v, o_hbm.at[idx_v.at[k]])

    return scatter_rows(h2w, dest_c)


def _collect(yb, dest_c):
    _, width = yb.shape
    n_chunks, _, chunk = dest_c.shape
    nc, ns = _sc_workers()
    per_w = n_chunks // (nc * ns)
    assert per_w * nc * ns == n_chunks

    @functools.partial(
        pl.kernel, mesh=plsc.VectorSubcoreMesh(core_axis_name="c", subcore_axis_name="s"),
        out_type=jax.ShapeDtypeStruct((TOP_K, n_chunks * chunk, width), yb.dtype),
        scratch_types=[pltpu.VMEM(dest_c.shape[1:], jnp.int32), pltpu.VMEM((chunk, width), yb.dtype)],
    )
    def gather_rows(y_hbm, d_hbm, o_hbm, idx_v, rows_v):
        wid = lax.axis_index("s") * nc + lax.axis_index("c")

        @pl.loop(0, per_w)
        def _(j):
            blk = wid * per_w + j
            pltpu.sync_copy(d_hbm.at[blk], idx_v)
            for k in range(TOP_K):
                pltpu.sync_copy(y_hbm.at[idx_v.at[k]], rows_v)
                pltpu.sync_copy(rows_v, o_hbm.at[k, pl.ds(pl.multiple_of(blk * chunk, chunk), chunk)])

    return gather_rows(yb, dest_c)


def _combine_kernel(alpha, steps, x1_ref, yg_ref, prob_ref, mod_ref, ln_g_ref, ln_b_ref, o_ref, ot_s):
    d = x1_ref.shape[1]
    batch = o_ref.shape[0]
    ffn_lo = jnp.zeros((x1_ref.shape[0], d // 2), _F32)
    ffn_hi = jnp.zeros((x1_ref.shape[0], d // 2), _F32)
    for k in range(TOP_K):
        lo, hi = _unpack_rows(yg_ref[k])
        ffn_lo = ffn_lo + prob_ref[:, k:k + 1] * lo
        ffn_hi = ffn_hi + prob_ref[:, k:k + 1] * hi
    ffn = jnp.concatenate([ffn_lo, ffn_hi], axis=1)
    gate = _rows(1.0 + mod_ref[:, 5 * d:6 * d], steps)
    out = _layer_norm(alpha * x1_ref[...] + gate * ffn, ln_g_ref[...], ln_b_ref[...])
    n_blk = d // V7X_LANES
    for j in range(n_blk):
        ot_s[j] = out[:, j * V7X_LANES:(j + 1) * V7X_LANES]
    for b in range(batch):
        for j in range(n_blk):
            o_ref[b, :, j * V7X_LANES:(j + 1) * V7X_LANES] = ot_s[j, pl.ds(b, steps, stride=batch), :]


def _combine(x1, yg, prob, mod, ln_g, ln_b, *, alpha, batch):
    tokens, d = x1.shape
    rows = COMBINE_ROWS
    steps = rows // batch
    const = lambda a: pl.BlockSpec(a.shape, lambda i: (0, 0))
    return pl.pallas_call(
        functools.partial(_combine_kernel, alpha, steps),
        grid=(tokens // rows,),
        in_specs=[
            pl.BlockSpec((rows, d), lambda i: (i, 0)),
            pl.BlockSpec((TOP_K, rows, d // 2), lambda i: (0, i, 0)),
            pl.BlockSpec((rows, V7X_LANES), lambda i: (i, 0)),
            const(mod), const(ln_g), const(ln_b),
        ],
        out_specs=pl.BlockSpec((batch, steps, d), lambda i: (0, i, 0)),
        out_shape=jax.ShapeDtypeStruct((batch, tokens // batch, d), _F32),
        scratch_shapes=[pltpu.VMEM((d // V7X_LANES, rows, V7X_LANES), _F32)],
        compiler_params=pltpu.CompilerParams(dimension_semantics=("parallel",)),
        name="combine",
    )(x1, yg, prob, mod, ln_g, ln_b)


def _block_diag(blocks):
    nb, n, a, b = blocks.shape
    eye = jnp.eye(n, dtype=blocks.dtype)
    return (eye[None, :, None, :, None] * blocks[:, :, :, None, :]).reshape(nb, n * a, n * b)


def _s5_params(lam_re, lam_im, log_dt, b_re, b_im, c_re, c_im):
    groups = lam_re.shape[0]
    nb = groups // S5_BLOCK_GROUPS
    dt = jnp.exp(log_dt)[:, None]
    mag = jnp.exp(lam_re * dt)
    ab_re, ab_im = mag * jnp.cos(lam_im * dt), mag * jnp.sin(lam_im * dt)
    den = lam_re * lam_re + lam_im * lam_im
    q_re = ((ab_re - 1.0) * lam_re + ab_im * lam_im) / den
    q_im = (ab_im * lam_re - (ab_re - 1.0) * lam_im) / den
    bb_re = q_re[..., None] * b_re - q_im[..., None] * b_im
    bb_im = q_re[..., None] * b_im + q_im[..., None] * b_re

    def per_block(a):
        return jnp.swapaxes(a.reshape(nb, S5_BLOCK_GROUPS, *a.shape[1:]), 2, 3)

    bmat = jnp.concatenate([_block_diag(per_block(bb_re)), _block_diag(per_block(bb_im))], axis=2)
    cmat = jnp.concatenate([_block_diag(per_block(c_re)), -_block_diag(per_block(c_im))], axis=1)
    s5a = jnp.broadcast_to(jnp.stack([ab_re, ab_im]).reshape(2, nb, 1, S5_BLOCK_STATES),
                           (2, nb, V7X_SUBLANES, S5_BLOCK_STATES))
    return dict(s5a=s5a, s5b=bmat.astype(_BF16), s5c=cmat.astype(_BF16))


def _mixer_vectors(d, b_in, conv_w, conv_b, b_rg_a, b_rg_x, lru_lambda, ln_g, ln_b, s5_d, b_router):
    pad = lambda v: jnp.pad(v, (0, d - v.shape[0]))
    s5w = s5_d.size
    rows = [b_in[0:d], b_in[d:2 * d], pad(b_in[2 * d:2 * d + s5w]), b_in[2 * d + s5w:3 * d + s5w],
            b_in[3 * d + s5w:4 * d + s5w], *conv_w, conv_b, b_rg_a, b_rg_x,
            -LRU_C * jax.nn.softplus(-lru_lambda), ln_g, ln_b, pad(s5_d.reshape(-1)), pad(b_router)]
    rows += [jnp.zeros((d,), _F32)] * (-len(rows) % V7X_SUBLANES)
    return jnp.stack(rows)


def kernel(x, c, w_ada, b_ada, w_in, b_in, conv_w, conv_b, w_rg_a, b_rg_a, w_rg_x, b_rg_x, lru_lambda, w_rnn_out, s5_lambda_re, s5_lambda_im, s5_log_dt, s5_b_re, s5_b_im, s5_c_re, s5_c_im, s5_d, w_glu, w_out, ln1_g, ln1_b, w_router, b_router, w_gu, b_gu, w_down, b_down, ln2_g, ln2_b):
    batch, seq, d = x.shape
    depth = w_ada.shape[0]
    n_experts = w_router.shape[-1]
    tokens = batch * seq
    alpha = (2.0 * depth) ** 0.25
    assert batch == V7X_SUBLANES and d % V7X_LANES == 0 and n_experts <= V7X_LANES
    assert seq % MIXER_STEPS == 0 and tokens % COMBINE_ROWS == 0 and tokens % EXPERT_ROWS == 0
    assert (MIXER_STEPS * batch) % V7X_LANES == 0

    for l in range(depth):
        mod = _ada(c, w_ada[l], b_ada[l])
        p = dict(
            vecs=_mixer_vectors(d, b_in[l], conv_w[l], conv_b[l], b_rg_a[l], b_rg_x[l], lru_lambda[l],
                                ln1_g[l], ln1_b[l], s5_d[l], b_router[l]),
            w_in=w_in[l].astype(_BF16),
            wg=jnp.concatenate([w_rg_a[l], w_rg_x[l]], axis=-1).astype(_BF16),
            w_rnn=w_rnn_out[l].astype(_BF16), w_glu=w_glu[l].astype(_BF16), w_out=w_out[l].astype(_BF16),
            w_r=jnp.pad(w_router[l], ((0, 0), (0, V7X_LANES - n_experts))).astype(_BF16),
            **_s5_params(s5_lambda_re[l], s5_lambda_im[l], s5_log_dt[l], s5_b_re[l], s5_b_im[l],
                         s5_c_re[l], s5_c_im[l]),
        )
        x1, h2w, dest_c, prob, cnt = _mixer(x, mod, p, alpha=alpha, n_experts=n_experts)

        r = EXPERT_ROWS
        blocks_per_region = tokens // r
        spare_blk = n_experts * blocks_per_region
        n_tiles = -(-(tokens * TOP_K + n_experts * (r - 1)) // r)
        counts = cnt[0, :n_experts].astype(jnp.int32)
        tiles_e = (counts + r - 1) // r
        tile_end = jnp.cumsum(tiles_e)
        n_used = tile_end[-1:]
        t_ids = jnp.arange(n_tiles, dtype=jnp.int32)
        done = (tile_end[None, :] <= t_ids[:, None]).astype(jnp.int32)
        tile_e = jnp.minimum(jnp.sum(done, axis=1), n_experts - 1)
        first_tile = jnp.sum(done * tiles_e[None, :], axis=1)
        used = t_ids < n_used
        tile_blk = jnp.where(used, tile_e * blocks_per_region + t_ids - first_tile, spare_blk)
        e_ids = jnp.arange(n_experts, dtype=jnp.int32)[None, :]
        own = (e_ids == tile_e[:, None]).astype(jnp.int32)
        valid = jnp.sum(own * counts[None, :], axis=1) - (t_ids - first_tile) * r
        tile_rows = jnp.where(used, jnp.where(valid <= r // 2, r // 2, r), 0)
        has_tiles = (tiles_e > 0).astype(jnp.int32)[None, :]
        tile_first = (used & (t_ids == first_tile)).astype(jnp.int32)
        tile_slot = jnp.sum(done * has_tiles, axis=1) % 2
        later = jnp.where((has_tiles > 0) & (e_ids > tile_e[:, None]), e_ids, n_experts)
        tile_next = jnp.min(later, axis=1)
        tile_next = jnp.where(tile_next < n_experts, tile_next, -1)

        xb = _dispatch(h2w, dest_c, (spare_blk + 1) * r)
        yb = _experts(xb, (tile_e, tile_blk, tile_rows, tile_first, tile_slot, tile_next),
                      w_gu[l], b_gu[l], w_down[l], b_down[l])
        yg = _collect(yb, dest_c)
        x = _combine(x1, yg, prob, mod, ln2_g[l].reshape(1, -1), ln2_b[l].reshape(1, -1),
                     alpha=alpha, batch=batch)
    return x
```

```python
import functools

import jax
import jax.numpy as jnp
from jax import lax
from jax.experimental import pallas as pl
from jax.experimental.pallas import tpu as pltpu
from jax.experimental.pallas import tpu_sc as plsc

V7X_SUBLANES = 8
V7X_LANES = 128
V7X_VMEM_BYTES = 64 * 1024 * 1024

CONV_WIDTH = 4
LRU_C = 8.0
S5_GROUP = 16
S5_STATE = 64
TOP_K = 4
SWIGLU_LIMIT = 7.0
SWIGLU_ALPHA = 1.702
LN_EPS = 1e-5

S5_BLOCK_GROUPS = V7X_LANES // S5_GROUP
S5_BLOCK_STATES = S5_BLOCK_GROUPS * S5_STATE

(_VEC_B_X, _VEC_B_Y, _VEC_B_U5, _VEC_B_GA, _VEC_B_GB, _VEC_CONV_W) = range(6)
(_VEC_CONV_B, _VEC_B_RG_A, _VEC_B_RG_X, _VEC_LAMC, _VEC_LN_G, _VEC_LN_B, _VEC_S5_D, _VEC_B_ROUTER) = range(
    _VEC_CONV_W + CONV_WIDTH, _VEC_CONV_W + CONV_WIDTH + 8)

MIXER_STEPS = 32
EXPERT_ROWS = 512
COMBINE_ROWS = 1024
WEIGHT_CAST_ROWS = 64

_BF16 = jnp.bfloat16
_F32 = jnp.float32


def _dot(a, b):
    return jnp.dot(a, b, preferred_element_type=_F32)


def _sigmoid(v):
    return 0.5 * jnp.tanh(0.5 * v) + 0.5


def _vmem_limit(nbytes):
    return int(min(nbytes, V7X_VMEM_BYTES - 4 * 1024 * 1024))


def _layer_norm(z, gain, bias):
    mu = jnp.mean(z, axis=-1, keepdims=True)
    zc = z - mu
    var = jnp.mean(zc * zc, axis=-1, keepdims=True)
    return zc * lax.rsqrt(var + LN_EPS) * gain + bias


def _rows(v, steps):
    return jnp.tile(v, (steps, 1))


_HI_MASK = 0xFFFF0000


def _pack_rows(v):
    half = v.shape[1] // 2
    bits = lax.bitcast_convert_type(v.astype(_BF16).astype(_F32), jnp.uint32)
    packed = (bits[:, :half] >> 16) | (bits[:, half:] & jnp.uint32(_HI_MASK))
    return lax.bitcast_convert_type(packed, jnp.int32)


def _unpack_rows(w):
    bits = lax.bitcast_convert_type(w, jnp.uint32)
    lo = lax.bitcast_convert_type(bits << 16, _F32)
    hi = lax.bitcast_convert_type(bits & jnp.uint32(_HI_MASK), _F32)
    return lo, hi


def _ada_kernel(c_ref, w_ref, b_ref, o_ref):
    c = c_ref[...]
    c_act = (c * _sigmoid(c)).astype(_BF16)
    o_ref[...] = _dot(c_act, w_ref[...].astype(_BF16)) + b_ref[...]


def _ada(c, w_ada, b_ada):
    batch, d = c.shape
    n_out = w_ada.shape[1]
    return pl.pallas_call(
        _ada_kernel,
        grid=(n_out // d,),
        in_specs=[
            pl.BlockSpec((batch, d), lambda j: (0, 0)),
            pl.BlockSpec((d, d), lambda j: (0, j)),
            pl.BlockSpec((1, d), lambda j: (0, j)),
        ],
        out_specs=pl.BlockSpec((batch, d), lambda j: (0, j)),
        out_shape=jax.ShapeDtypeStruct((batch, n_out), _F32),
        name="ada",
    )(c, w_ada, b_ada.reshape(1, n_out))


def _mixer_kernel(alpha, steps, batch, d, n_s5_blocks, n_experts, region_rows,
                  x_ref, mod_ref, vecs_ref, w_in_ref, wg_ref, w_rnn_ref, s5a_ref, s5b_ref, s5c_ref,
                  w_glu_ref, w_out_ref, w_r_ref,
                  x1_ref, h2_ref, dest_ref, prob_ref, cnt_ref,
                  xt_s, xc_s, a_s, u_s, bu_s, u5_s, ya_s, ga_s, gb_s, h_state, s5_state, cnt_s):
    m = steps * batch
    halo = (CONV_WIDTH - 1) * batch
    s5w = n_s5_blocks * V7X_LANES
    n_blk = d // V7X_LANES
    bs = S5_BLOCK_STATES
    step = pl.program_id(0)

    @pl.when(step == 0)
    def _():
        xc_s[0:halo, :] = jnp.zeros((halo, d), _F32)
        h_state[...] = jnp.zeros_like(h_state)
        s5_state[...] = jnp.zeros_like(s5_state)
        cnt_s[...] = jnp.zeros_like(cnt_s)
        for ref in (xt_s, u_s, bu_s, u5_s, ya_s, ga_s, gb_s):
            ref[...] = jnp.zeros_like(ref)

    def mod(k):
        return mod_ref[:, k * d:(k + 1) * d]

    slot = lax.rem(step, 2)
    for b in range(batch):
        for j in range(n_blk):
            xt_s[slot, j, pl.ds(b, steps, stride=batch), :] = x_ref[b, :, j * V7X_LANES:(j + 1) * V7X_LANES]
    x = jnp.concatenate([xt_s[slot, j] for j in range(n_blk)], axis=1)
    hb = (x * _rows(1.0 + mod(1), steps) + _rows(mod(0), steps)).astype(_BF16)

    def vec(k, width=d):
        return vecs_ref[k:k + 1, 0:width]

    def in_proj(c0, width, bias_row):
        return _dot(hb, w_in_ref[:, c0:c0 + width]) + vec(bias_row, width)

    c0 = 2 * d
    c1 = c0 + s5w
    branch_a = _dot((ya_s[...] * u_s[...]).astype(_BF16), w_rnn_ref[...])
    y5 = jnp.concatenate(
        [_dot(bu_s[:, 2 * bs * j:2 * bs * (j + 1)].astype(_BF16), s5c_ref[j]) for j in range(n_s5_blocks)],
        axis=1) + vec(_VEC_S5_D, s5w) * u5_s[...]
    xc_s[halo:halo + m, :] = in_proj(0, d, _VEC_B_X)
    glu = _dot(jax.nn.gelu(y5).astype(_BF16), w_glu_ref[...])
    xr = jnp.zeros((m, d), _F32) + vec(_VEC_CONV_B)
    for k in range(CONV_WIDTH):
        xr = xr + vec(_VEC_CONV_W + k) * xc_s[k * batch:k * batch + m, :]
    xc_s[0:halo, :] = xc_s[m:m + halo, :]
    xrb = xr.astype(_BF16)
    gates = [_dot(xrb[:, j * V7X_LANES:(j + 1) * V7X_LANES], wg_ref[j]) for j in range(n_blk)]
    merged = (ga_s[...] * branch_a + gb_s[...] * (glu[:, :d] * _sigmoid(glu[:, d:]))).astype(_BF16)
    u5 = in_proj(c0, s5w, _VEC_B_U5)
    u5_s[...] = u5
    r_gate = _sigmoid(jnp.concatenate([g[:, :V7X_LANES] for g in gates], axis=1) + vec(_VEC_B_RG_A))
    i_gate = _sigmoid(jnp.concatenate([g[:, V7X_LANES:] for g in gates], axis=1) + vec(_VEC_B_RG_X))
    a = jnp.exp(vec(_VEC_LAMC) * r_gate)
    a_s[...] = a
    z = 1.0 - a * a
    u_s[...] = jnp.where(z > 0.0, z * lax.rsqrt(z), 0.0) * (i_gate * xr)
    mix = _dot(merged, w_out_ref[...])
    u5b = u5.astype(_BF16)
    for j in range(n_s5_blocks):
        bu_s[:, 2 * bs * j:2 * bs * (j + 1)] = _dot(u5b[:, j * V7X_LANES:(j + 1) * V7X_LANES], s5b_ref[j])
    hc = h_state[...]
    s5c = [(s5_state[:, 2 * bs * j:2 * bs * j + bs], s5_state[:, 2 * bs * j + bs:2 * bs * (j + 1)])
           for j in range(n_s5_blocks)]
    for t in range(steps):
        r0 = t * batch
        hc = a_s[r0:r0 + batch, :] * hc + u_s[r0:r0 + batch, :]
        u_s[r0:r0 + batch, :] = hc
        for j in range(n_s5_blocks):
            re0, im0 = 2 * bs * j, 2 * bs * j + bs
            re, im = s5c[j]
            ar, ai = s5a_ref[0, j], s5a_ref[1, j]
            nre = ar * re - ai * im + bu_s[r0:r0 + batch, re0:re0 + bs]
            nim = ar * im + ai * re + bu_s[r0:r0 + batch, im0:im0 + bs]
            bu_s[r0:r0 + batch, re0:re0 + bs] = nre
            bu_s[r0:r0 + batch, im0:im0 + bs] = nim
            s5c[j] = (nre, nim)
    h_state[...] = hc
    for j in range(n_s5_blocks):
        s5_state[:, 2 * bs * j:2 * bs * j + bs] = s5c[j][0]
        s5_state[:, 2 * bs * j + bs:2 * bs * (j + 1)] = s5c[j][1]
    x_prev = jnp.concatenate([xt_s[1 - slot, j] for j in range(n_blk)], axis=1)
    x1 = _layer_norm(alpha * x_prev + _rows(1.0 + mod(2), steps) * mix, vec(_VEC_LN_G), vec(_VEC_LN_B))
    x1_ref[...] = x1
    h2 = x1 * _rows(1.0 + mod(4), steps) + _rows(mod(3), steps)
    h2b = h2.astype(_BF16)
    h2_ref[...] = _pack_rows(h2)
    ya_s[...] = jax.nn.gelu(in_proj(d, d, _VEC_B_Y))
    lane = lax.broadcasted_iota(jnp.int32, (m, V7X_LANES), 1)
    lane_f = lane.astype(_F32)
    neg_inf = jnp.float32(-jnp.inf)
    logits = jnp.where(lane < n_experts, _dot(h2b, w_r_ref[...]) + vec(_VEC_B_ROUTER, V7X_LANES), neg_inf)
    ga_s[...] = _sigmoid(in_proj(c1, d, _VEC_B_GA))
    onehot = jnp.zeros((m, V7X_LANES), _F32)
    picks, vals = [], []
    for _ in range(TOP_K):
        v = jnp.max(logits, axis=-1, keepdims=True)
        p = jnp.argmax(logits, axis=-1, keepdims=True).astype(_F32)
        hit = lane_f == p
        onehot = jnp.where(hit, 1.0, onehot)
        logits = jnp.where(hit, neg_inf, logits)
        picks.append(p)
        vals.append(v)
    exps = [jnp.exp(v - vals[0]) for v in vals]
    inv_den = 1.0 / functools.reduce(lambda s, e: s + e, exps)
    row = lax.broadcasted_iota(jnp.int32, (m, m), 0)
    col = lax.broadcasted_iota(jnp.int32, (m, m), 1)
    earlier = jnp.where(col < row, 1.0, 0.0).astype(_BF16)
    before = _dot(earlier, onehot.astype(_BF16)) + cnt_s[0:1, :]
    gb_s[...] = _sigmoid(in_proj(c1 + d, d, _VEC_B_GB))
    prob_out = jnp.zeros((m, V7X_LANES), _F32)
    dest_out = jnp.zeros((m, V7X_LANES), _F32)
    for k in range(TOP_K):
        rank_k = jnp.sum(jnp.where(lane_f == picks[k], before, 0.0), axis=-1, keepdims=True)
        prob_out = jnp.where(lane == k, exps[k] * inv_den, prob_out)
        dest_out = jnp.where(lane == k, picks[k] * float(region_rows) + rank_k, dest_out)
    prob_ref[...] = prob_out
    dest_t = dest_out.T[0:V7X_SUBLANES, :].astype(jnp.int32)
    for j in range(m // V7X_LANES):
        dest_ref[j] = dest_t[:, j * V7X_LANES:(j + 1) * V7X_LANES]
    has_prev = jnp.where(step > 0, 1.0, 0.0)
    cnt_new = cnt_s[...] + has_prev * jnp.sum(onehot, axis=0, keepdims=True)
    cnt_s[...] = cnt_new
    cnt_ref[...] = cnt_new


def _mixer(x, mod, p, *, alpha, n_experts):
    batch, seq, d = x.shape
    tokens = batch * seq
    steps = MIXER_STEPS
    m = steps * batch
    n_chunks = seq // steps
    n_s5_blocks = p["s5b"].shape[0]
    s5_lanes = n_s5_blocks * 2 * S5_BLOCK_STATES
    halo = (CONV_WIDTH - 1) * batch

    def const(a):
        nd = a.ndim
        return pl.BlockSpec(a.shape, lambda i, nd=nd: (0,) * nd, pipeline_mode=pl.Buffered(1))

    weights = [p["vecs"], p["w_in"], p["wg"], p["w_rnn"], p["s5a"], p["s5b"], p["s5c"], p["w_glu"], p["w_out"],
               p["w_r"]]
    prev = lambda i: jnp.maximum(i - 1, 0)
    row_spec = lambda width: pl.BlockSpec((m, width), lambda i: (prev(i), 0))
    chunks = m // V7X_LANES
    out_shape = (
        jax.ShapeDtypeStruct((tokens, d), _F32),
        jax.ShapeDtypeStruct((tokens, d // 2), jnp.int32),
        jax.ShapeDtypeStruct((tokens // V7X_LANES, V7X_SUBLANES, V7X_LANES), jnp.int32),
        jax.ShapeDtypeStruct((tokens, V7X_LANES), _F32),
        jax.ShapeDtypeStruct((V7X_SUBLANES, V7X_LANES), _F32),
    )
    act = pltpu.VMEM((m, d), _F32)
    scratch = [
        pltpu.VMEM((2, d // V7X_LANES, m, V7X_LANES), _F32),
        pltpu.VMEM((m + halo, d), _F32),
        act, act,
        pltpu.VMEM((m, s5_lanes), _F32),
        pltpu.VMEM((m, n_s5_blocks * V7X_LANES), _F32),
        act, act, act,
        pltpu.VMEM((batch, d), _F32),
        pltpu.VMEM((batch, s5_lanes), _F32),
        pltpu.VMEM((V7X_SUBLANES, V7X_LANES), _F32),
    ]
    weight_bytes = sum(w.size * w.dtype.itemsize for w in weights)
    act_bytes = m * d * 4
    vmem = weight_bytes + 32 * act_bytes
    kern = functools.partial(_mixer_kernel, alpha, steps, batch, d, n_s5_blocks, n_experts, tokens)
    return pl.pallas_call(
        kern,
        grid=(n_chunks + 1,),
        in_specs=[pl.BlockSpec((batch, steps, d), lambda i: (0, jnp.minimum(i, n_chunks - 1), 0)),
                  const(mod)] + [const(w) for w in weights],
        out_specs=(row_spec(d), row_spec(d // 2),
                   pl.BlockSpec((chunks, V7X_SUBLANES, V7X_LANES), lambda i: (prev(i), 0, 0)),
                   row_spec(V7X_LANES),
                   pl.BlockSpec((V7X_SUBLANES, V7X_LANES), lambda i: (0, 0))),
        out_shape=out_shape,
        scratch_shapes=scratch,
        compiler_params=pltpu.CompilerParams(dimension_semantics=("arbitrary",),
                                             vmem_limit_bytes=_vmem_limit(vmem)),
        name="mixer",
    )(x, mod, *weights)


def _expert_kernel(d_ff, tile_e_ref, tile_blk_ref, tile_rows_ref, tile_first_ref, tile_slot_ref, tile_next_ref,
                   x_ref, wgu_hbm, bgu_ref, wd_hbm, bd_ref, y_ref, wgu_f, wd_f, wgu_s, wd_s, sems):
    i = pl.program_id(0)
    d = wgu_f.shape[1]
    r = x_ref.shape[0]

    def weight_copies(expert, slot):
        return (pltpu.make_async_copy(wgu_hbm.at[expert], wgu_f.at[slot], sems.at[slot, 0]),
                pltpu.make_async_copy(wd_hbm.at[expert], wd_f.at[slot], sems.at[slot, 1]))

    @pl.when(tile_first_ref[i] == 1)
    def _():
        slot = tile_slot_ref[i]

        @pl.when(i == 0)
        def _():
            for cp in weight_copies(tile_e_ref[i], slot):
                cp.start()

        for cp in weight_copies(tile_e_ref[i], slot):
            cp.wait()

        @pl.when(tile_next_ref[i] >= 0)
        def _():
            for cp in weight_copies(tile_next_ref[i], 1 - slot):
                cp.start()

        def cast(c, carry):
            r0 = pl.multiple_of(c * WEIGHT_CAST_ROWS, WEIGHT_CAST_ROWS)
            wgu_s[pl.ds(r0, WEIGHT_CAST_ROWS), :] = wgu_f[slot, pl.ds(r0, WEIGHT_CAST_ROWS), :].astype(_BF16)
            wd_s[pl.ds(r0, WEIGHT_CAST_ROWS), :] = wd_f[slot, pl.ds(r0, WEIGHT_CAST_ROWS), :].astype(_BF16)
            return carry

        lax.fori_loop(0, d // WEIGHT_CAST_ROWS, cast, 0)

    def mlp(rows):
        lo, hi = _unpack_rows(x_ref[0:rows, :])
        x = jnp.concatenate([lo, hi], axis=1).astype(_BF16)
        gu = _dot(x, wgu_s[...]) + bgu_ref[0]
        gate = jnp.minimum(gu[:, :d_ff], SWIGLU_LIMIT)
        up = jnp.clip(gu[:, d_ff:], -SWIGLU_LIMIT, SWIGLU_LIMIT)
        act = gate * _sigmoid(SWIGLU_ALPHA * gate) * (up + 1.0)
        y_ref[0:rows, :] = _pack_rows(_dot(act.astype(_BF16), wd_s[...]) + bd_ref[0])
        if rows < r:
            y_ref[rows:r, :] = jnp.zeros((r - rows, y_ref.shape[1]), y_ref.dtype)

    @pl.when(tile_rows_ref[i] == r)
    def _():
        mlp(r)

    @pl.when(tile_rows_ref[i] == r // 2)
    def _():
        mlp(r // 2)

    @pl.when(tile_rows_ref[i] == 0)
    def _():
        y_ref[...] = jnp.zeros_like(y_ref)


def _experts(xb, schedule, w_gu, b_gu, w_down, b_down):
    n_rows, half = xb.shape
    n_experts, d, two_ff = w_gu.shape
    d_ff = two_ff // 2
    assert d_ff == d, "the weight cast loop walks w_gu and w_down rows together"
    r = EXPERT_ROWS
    vmem = 2 * (d * two_ff + d_ff * d) * 4 + (d * two_ff + d_ff * d) * 2 + 8 * r * half * 4 + 6 * r * two_ff * 4
    n_sched = len(schedule)
    tile = lambda i, *s: (s[1][i], 0)
    expert = lambda i, *s: (s[0][i], 0, 0)
    grid_spec = pltpu.PrefetchScalarGridSpec(
        num_scalar_prefetch=n_sched,
        grid=(schedule[0].shape[0],),
        in_specs=[
            pl.BlockSpec((r, half), tile),
            pl.BlockSpec(memory_space=pl.ANY),
            pl.BlockSpec((1, 1, two_ff), expert),
            pl.BlockSpec(memory_space=pl.ANY),
            pl.BlockSpec((1, 1, d), expert),
        ],
        out_specs=pl.BlockSpec((r, half), tile),
        scratch_shapes=[pltpu.VMEM((2, d, two_ff), _F32), pltpu.VMEM((2, d_ff, d), _F32),
                        pltpu.VMEM((d, two_ff), _BF16), pltpu.VMEM((d_ff, d), _BF16),
                        pltpu.SemaphoreType.DMA((2, 2))],
    )
    return pl.pallas_call(
        functools.partial(_expert_kernel, d_ff),
        grid_spec=grid_spec,
        out_shape=jax.ShapeDtypeStruct((n_rows, half), jnp.int32),
        compiler_params=pltpu.CompilerParams(dimension_semantics=("arbitrary",),
                                             vmem_limit_bytes=_vmem_limit(vmem)),
        name="experts",
    )(*schedule, xb, w_gu, b_gu.reshape(n_experts, 1, two_ff), w_down, b_down.reshape(n_experts, 1, d))


def _sc_workers():
    info = plsc.get_sparse_core_info()
    return info.num_cores, info.num_subcores


def _dispatch(h2w, dest_c, n_rows):
    tokens, width = h2w.shape
    n_chunks, _, chunk = dest_c.shape
    nc, ns = _sc_workers()
    per_w = n_chunks // (nc * ns)
    assert per_w * nc * ns == n_chunks

    @functools.partial(
        pl.kernel, mesh=plsc.VectorSubcoreMesh(core_axis_name="c", subcore_axis_name="s"),
        out_type=jax.ShapeDtypeStruct((n_rows, width), h2w.dtype),
        scratch_types=[pltpu.VMEM(dest_c.shape[1:], jnp.int32), pltpu.VMEM((chunk, width), h2w.dtype)],
    )
    def scatter_rows(h_hbm, d_hbm, o_hbm, idx_v, rows_v):
        wid = lax.axis_index("s") * nc + lax.axis_index("c")

        @pl.loop(0, per_w)
        def _(j):
            blk = wid * per_w + j
            pltpu.sync_copy(d_hbm.at[blk], idx_v)
            pltpu.sync_copy(h_hbm.at[pl.ds(pl.multiple_of(blk * chunk, chunk), chunk)], rows_v)
            for k in range(TOP_K):
                pltpu.sync_copy(rows_v, o_hbm.at[idx_v.at[k]])

    return scatter_rows(h2w, dest_c)


def _collect(yb, dest_c):
    _, width = yb.shape
    n_chunks, _, chunk = dest_c.shape
    nc, ns = _sc_workers()
    per_w = n_chunks // (nc * ns)
    assert per_w * nc * ns == n_chunks

    @functools.partial(
        pl.kernel, mesh=plsc.VectorSubcoreMesh(core_axis_name="c", subcore_axis_name="s"),
        out_type=jax.ShapeDtypeStruct((TOP_K, n_chunks * chunk, width), yb.dtype),
        scratch_types=[pltpu.VMEM(dest_c.shape[1:], jnp.int32), pltpu.VMEM((chunk, width), yb.dtype)],
    )
    def gather_rows(y_hbm, d_hbm, o_hbm, idx_v, rows_v):
        wid = lax.axis_index("s") * nc + lax.axis_index("c")

        @pl.loop(0, per_w)
        def _(j):
            blk = wid * per_w + j
            pltpu.sync_copy(d_hbm.at[blk], idx_v)
            for k in range(TOP_K):
                pltpu.sync_copy(y_hbm.at[idx_v.at[k]], rows_v)
                pltpu.sync_copy(rows_v, o_hbm.at[k, pl.ds(pl.multiple_of(blk * chunk, chunk), chunk)])

    return gather_rows(yb, dest_c)


def _combine_kernel(alpha, steps, x1_ref, yg_ref, prob_ref, mod_ref, ln_g_ref, ln_b_ref, o_ref, ot_s):
    d = x1_ref.shape[1]
    batch = o_ref.shape[0]
    ffn_lo = jnp.zeros((x1_ref.shape[0], d // 2), _F32)
    ffn_hi = jnp.zeros((x1_ref.shape[0], d // 2), _F32)
    for k in range(TOP_K):
        lo, hi = _unpack_rows(yg_ref[k])
        ffn_lo = ffn_lo + prob_ref[:, k:k + 1] * lo
        ffn_hi = ffn_hi + prob_ref[:, k:k + 1] * hi
    ffn = jnp.concatenate([ffn_lo, ffn_hi], axis=1)
    gate = _rows(1.0 + mod_ref[:, 5 * d:6 * d], steps)
    out = _layer_norm(alpha * x1_ref[...] + gate * ffn, ln_g_ref[...], ln_b_ref[...])
    n_blk = d // V7X_LANES
    for j in range(n_blk):
        ot_s[j] = out[:, j * V7X_LANES:(j + 1) * V7X_LANES]
    for b in range(batch):
        for j in range(n_blk):
            o_ref[b, :, j * V7X_LANES:(j + 1) * V7X_LANES] = ot_s[j, pl.ds(b, steps, stride=batch), :]


def _combine(x1, yg, prob, mod, ln_g, ln_b, *, alpha, batch):
    tokens, d = x1.shape
    rows = COMBINE_ROWS
    steps = rows // batch
    const = lambda a: pl.BlockSpec(a.shape, lambda i: (0, 0))
    return pl.pallas_call(
        functools.partial(_combine_kernel, alpha, steps),
        grid=(tokens // rows,),
        in_specs=[
            pl.BlockSpec((rows, d), lambda i: (i, 0)),
            pl.BlockSpec((TOP_K, rows, d // 2), lambda i: (0, i, 0)),
            pl.BlockSpec((rows, V7X_LANES), lambda i: (i, 0)),
            const(mod), const(ln_g), const(ln_b),
        ],
        out_specs=pl.BlockSpec((batch, steps, d), lambda i: (0, i, 0)),
        out_shape=jax.ShapeDtypeStruct((batch, tokens // batch, d), _F32),
        scratch_shapes=[pltpu.VMEM((d // V7X_LANES, rows, V7X_LANES), _F32)],
        compiler_params=pltpu.CompilerParams(dimension_semantics=("parallel",)),
        name="combine",
    )(x1, yg, prob, mod, ln_g, ln_b)


def _block_diag(blocks):
    nb, n, a, b = blocks.shape
    eye = jnp.eye(n, dtype=blocks.dtype)
    return (eye[None, :, None, :, None] * blocks[:, :, :, None, :]).reshape(nb, n * a, n * b)


def _s5_params(lam_re, lam_im, log_dt, b_re, b_im, c_re, c_im):
    groups = lam_re.shape[0]
    nb = groups // S5_BLOCK_GROUPS
    dt = jnp.exp(log_dt)[:, None]
    mag = jnp.exp(lam_re * dt)
    ab_re, ab_im = mag * jnp.cos(lam_im * dt), mag * jnp.sin(lam_im * dt)
    den = lam_re * lam_re + lam_im * lam_im
    q_re = ((ab_re - 1.0) * lam_re + ab_im * lam_im) / den
    q_im = (ab_im * lam_re - (ab_re - 1.0) * lam_im) / den
    bb_re = q_re[..., None] * b_re - q_im[..., None] * b_im
    bb_im = q_re[..., None] * b_im + q_im[..., None] * b_re

    def per_block(a):
        return jnp.swapaxes(a.reshape(nb, S5_BLOCK_GROUPS, *a.shape[1:]), 2, 3)

    bmat = jnp.concatenate([_block_diag(per_block(bb_re)), _block_diag(per_block(bb_im))], axis=2)
    cmat = jnp.concatenate([_block_diag(per_block(c_re)), -_block_diag(per_block(c_im))], axis=1)
    s5a = jnp.broadcast_to(jnp.stack([ab_re, ab_im]).reshape(2, nb, 1, S5_BLOCK_STATES),
                           (2, nb, V7X_SUBLANES, S5_BLOCK_STATES))
    return dict(s5a=s5a, s5b=bmat.astype(_BF16), s5c=cmat.astype(_BF16))


def _mixer_vectors(d, b_in, conv_w, conv_b, b_rg_a, b_rg_x, lru_lambda, ln_g, ln_b, s5_d, b_router):
    pad = lambda v: jnp.pad(v, (0, d - v.shape[0]))
    s5w = s5_d.size
    rows = [b_in[0:d], b_in[d:2 * d], pad(b_in[2 * d:2 * d + s5w]), b_in[2 * d + s5w:3 * d + s5w],
            b_in[3 * d + s5w:4 * d + s5w], *conv_w, conv_b, b_rg_a, b_rg_x,
            -LRU_C * jax.nn.softplus(-lru_lambda), ln_g, ln_b, pad(s5_d.reshape(-1)), pad(b_router)]
    rows += [jnp.zeros((d,), _F32)] * (-len(rows) % V7X_SUBLANES)
    return jnp.stack(rows)


def kernel(x, c, w_ada, b_ada, w_in, b_in, conv_w, conv_b, w_rg_a, b_rg_a, w_rg_x, b_rg_x, lru_lambda, w_rnn_out, s5_lambda_re, s5_lambda_im, s5_log_dt, s5_b_re, s5_b_im, s5_c_re, s5_c_im, s5_d, w_glu, w_out, ln1_g, ln1_b, w_router, b_router, w_gu, b_gu, w_down, b_down, ln2_g, ln2_b):
    batch, seq, d = x.shape
    depth = w_ada.shape[0]
    n_experts = w_router.shape[-1]
    tokens = batch * seq
    alpha = (2.0 * depth) ** 0.25
    assert batch == V7X_SUBLANES and d % V7X_LANES == 0 and n_experts <= V7X_LANES
    assert seq % MIXER_STEPS == 0 and tokens % COMBINE_ROWS == 0 and tokens % EXPERT_ROWS == 0
    assert (MIXER_STEPS * batch) % V7X_LANES == 0

    for l in range(depth):
        mod = _ada(c, w_ada[l], b_ada[l])
        p = dict(
            vecs=_mixer_vectors(d, b_in[l], conv_w[l], conv_b[l], b_rg_a[l], b_rg_x[l], lru_lambda[l],
                                ln1_g[l], ln1_b[l], s5_d[l], b_router[l]),
            w_in=w_in[l].astype(_BF16),
            wg=jnp.concatenate([w_rg_a[l], w_rg_x[l]], axis=-1).astype(_BF16),
            w_rnn=w_rnn_out[l].astype(_BF16), w_glu=w_glu[l].astype(_BF16), w_out=w_out[l].astype(_BF16),
            w_r=jnp.pad(w_router[l], ((0, 0), (0, V7X_LANES - n_experts))).astype(_BF16),
            **_s5_params(s5_lambda_re[l], s5_lambda_im[l], s5_log_dt[l], s5_b_re[l], s5_b_im[l],
                         s5_c_re[l], s5_c_im[l]),
        )
        x1, h2w, dest_c, prob, cnt = _mixer(x, mod, p, alpha=alpha, n_experts=n_experts)

        r = EXPERT_ROWS
        blocks_per_region = tokens // r
        spare_blk = n_experts * blocks_per_region
        n_tiles = -(-(tokens * TOP_K + n_experts * (r - 1)) // r)
        counts = cnt[0, :n_experts].astype(jnp.int32)
        tiles_e = (counts + r - 1) // r
        tile_end = jnp.cumsum(tiles_e)
        n_used = tile_end[-1:]
        t_ids = jnp.arange(n_tiles, dtype=jnp.int32)
        done = (tile_end[None, :] <= t_ids[:, None]).astype(jnp.int32)
        tile_e = jnp.minimum(jnp.sum(done, axis=1), n_experts - 1)
        first_tile = jnp.sum(done * tiles_e[None, :], axis=1)
        used = t_ids < n_used
        tile_blk = jnp.where(used, tile_e * blocks_per_region + t_ids - first_tile, spare_blk)
        e_ids = jnp.arange(n_experts, dtype=jnp.int32)[None, :]
        own = (e_ids == tile_e[:, None]).astype(jnp.int32)
        valid = jnp.sum(own * counts[None, :], axis=1) - (t_ids - first_tile) * r
        tile_rows = jnp.where(used, jnp.where(valid <= r // 2, r // 2, r), 0)
        has_tiles = (tiles_e > 0).astype(jnp.int32)[None, :]
        tile_first = (used & (t_ids == first_tile)).astype(jnp.int32)
        tile_slot = jnp.sum(done * has_tiles, axis=1) % 2
        later = jnp.where((has_tiles > 0) & (e_ids > tile_e[:, None]), e_ids, n_experts)
        tile_next = jnp.min(later, axis=1)
        tile_next = jnp.where(tile_next < n_experts, tile_next, -1)

        xb = _dispatch(h2w, dest_c, (spare_blk + 1) * r)
        yb = _experts(xb, (tile_e, tile_blk, tile_rows, tile_first, tile_slot, tile_next),
                      w_gu[l], b_gu[l], w_down[l], b_down[l])
        yg = _collect(yb, dest_c)
        x = _combine(x1, yg, prob, mod, ln2_g[l].reshape(1, -1), ln2_b[l].reshape(1, -1),
                     alpha=alpha, batch=batch)
    return x
```

```python
import functools

import jax
import jax.numpy as jnp
from jax import lax
from jax.experimental import pallas as pl
from jax.experimental.pallas import tpu as pltpu
from jax.experimental.pallas import tpu_sc as plsc

V7X_SUBLANES = 8
V7X_LANES = 128
V7X_VMEM_BYTES = 64 * 1024 * 1024

CONV_WIDTH = 4
LRU_C = 8.0
S5_GROUP = 16
S5_STATE = 64
TOP_K = 4
SWIGLU_LIMIT = 7.0
SWIGLU_ALPHA = 1.702
LN_EPS = 1e-5

S5_BLOCK_GROUPS = V7X_LANES // S5_GROUP
S5_BLOCK_STATES = S5_BLOCK_GROUPS * S5_STATE

(_VEC_B_X, _VEC_B_Y, _VEC_B_U5, _VEC_B_GA, _VEC_B_GB, _VEC_CONV_W) = range(6)
(_VEC_CONV_B, _VEC_B_RG_A, _VEC_B_RG_X, _VEC_LAMC, _VEC_LN_G, _VEC_LN_B, _VEC_S5_D, _VEC_B_ROUTER) = range(
    _VEC_CONV_W + CONV_WIDTH, _VEC_CONV_W + CONV_WIDTH + 8)

MIXER_STEPS = 32
EXPERT_ROWS = 512
COMBINE_ROWS = 1024
WEIGHT_CAST_ROWS = 64

_BF16 = jnp.bfloat16
_F32 = jnp.float32


def _dot(a, b):
    return jnp.dot(a, b, preferred_element_type=_F32)


def _sigmoid(v):
    return 0.5 * jnp.tanh(0.5 * v) + 0.5


def _vmem_limit(nbytes):
    return int(min(nbytes, V7X_VMEM_BYTES - 4 * 1024 * 1024))


def _layer_norm(z, gain, bias):
    mu = jnp.mean(z, axis=-1, keepdims=True)
    zc = z - mu
    var = jnp.mean(zc * zc, axis=-1, keepdims=True)
    return zc * lax.rsqrt(var + LN_EPS) * gain + bias


def _rows(v, steps):
    return jnp.tile(v, (steps, 1))


_HI_MASK = 0xFFFF0000


def _pack_rows(v):
    half = v.shape[1] // 2
    bits = lax.bitcast_convert_type(v.astype(_BF16).astype(_F32), jnp.uint32)
    packed = (bits[:, :half] >> 16) | (bits[:, half:] & jnp.uint32(_HI_MASK))
    return lax.bitcast_convert_type(packed, jnp.int32)


def _unpack_rows(w):
    bits = lax.bitcast_convert_type(w, jnp.uint32)
    lo = lax.bitcast_convert_type(bits << 16, _F32)
    hi = lax.bitcast_convert_type(bits & jnp.uint32(_HI_MASK), _F32)
    return lo, hi


def _ada_kernel(c_ref, w_ref, b_ref, o_ref):
    c = c_ref[...]
    c_act = (c * _sigmoid(c)).astype(_BF16)
    o_ref[...] = _dot(c_act, w_ref[...].astype(_BF16)) + b_ref[...]


def _ada(c, w_ada, b_ada):
    batch, d = c.shape
    n_out = w_ada.shape[1]
    return pl.pallas_call(
        _ada_kernel,
        grid=(n_out // d,),
        in_specs=[
            pl.BlockSpec((batch, d), lambda j: (0, 0)),
            pl.BlockSpec((d, d), lambda j: (0, j)),
            pl.BlockSpec((1, d), lambda j: (0, j)),
        ],
        out_specs=pl.BlockSpec((batch, d), lambda j: (0, j)),
        out_shape=jax.ShapeDtypeStruct((batch, n_out), _F32),
        name="ada",
    )(c, w_ada, b_ada.reshape(1, n_out))


def _mixer_kernel(alpha, steps, batch, d, n_s5_blocks, n_experts, region_rows,
                  x_ref, mod_ref, vecs_ref, w_in_ref, wg_ref, w_rnn_ref, s5a_ref, s5b_ref, s5c_ref,
                  w_glu_ref, w_out_ref, w_r_ref,
                  x1_ref, h2_ref, dest_ref, prob_ref, cnt_ref,
                  xt_s, xc_s, a_s, u_s, bu_s, u5_s, ya_s, ga_s, gb_s, h_state, s5_state, cnt_s):
    m = steps * batch
    halo = (CONV_WIDTH - 1) * batch
    s5w = n_s5_blocks * V7X_LANES
    n_blk = d // V7X_LANES
    bs = S5_BLOCK_STATES
    step = pl.program_id(0)

    @pl.when(step == 0)
    def _():
        xc_s[0:halo, :] = jnp.zeros((halo, d), _F32)
        h_state[...] = jnp.zeros_like(h_state)
        s5_state[...] = jnp.zeros_like(s5_state)
        cnt_s[...] = jnp.zeros_like(cnt_s)
        for ref in (xt_s, u_s, bu_s, u5_s, ya_s, ga_s, gb_s):
            ref[...] = jnp.zeros_like(ref)

    def mod(k):
        return mod_ref[:, k * d:(k + 1) * d]

    slot = lax.rem(step, 2)
    for b in range(batch):
        for j in range(n_blk):
            xt_s[slot, j, pl.ds(b, steps, stride=batch), :] = x_ref[b, :, j * V7X_LANES:(j + 1) * V7X_LANES]
    x = jnp.concatenate([xt_s[slot, j] for j in range(n_blk)], axis=1)
    hb = (x * _rows(1.0 + mod(1), steps) + _rows(mod(0), steps)).astype(_BF16)

    def vec(k, width=d):
        return vecs_ref[k:k + 1, 0:width]

    def in_proj(c0, width, bias_row):
        return _dot(hb, w_in_ref[:, c0:c0 + width]) + vec(bias_row, width)

    c0 = 2 * d
    c1 = c0 + s5w
    branch_a = _dot((ya_s[...] * u_s[...]).astype(_BF16), w_rnn_ref[...])
    y5 = jnp.concatenate(
        [_dot(bu_s[:, 2 * bs * j:2 * bs * (j + 1)].astype(_BF16), s5c_ref[j]) for j in range(n_s5_blocks)],
        axis=1) + vec(_VEC_S5_D, s5w) * u5_s[...]
    xc_s[halo:halo + m, :] = in_proj(0, d, _VEC_B_X)
    glu = _dot(jax.nn.gelu(y5).astype(_BF16), w_glu_ref[...])
    xr = jnp.zeros((m, d), _F32) + vec(_VEC_CONV_B)
    for k in range(CONV_WIDTH):
        xr = xr + vec(_VEC_CONV_W + k) * xc_s[k * batch:k * batch + m, :]
    xc_s[0:halo, :] = xc_s[m:m + halo, :]
    xrb = xr.astype(_BF16)
    gates = [_dot(xrb[:, j * V7X_LANES:(j + 1) * V7X_LANES], wg_ref[j]) for j in range(n_blk)]
    merged = (ga_s[...] * branch_a + gb_s[...] * (glu[:, :d] * _sigmoid(glu[:, d:]))).astype(_BF16)
    u5 = in_proj(c0, s5w, _VEC_B_U5)
    u5_s[...] = u5
    r_gate = _sigmoid(jnp.concatenate([g[:, :V7X_LANES] for g in gates], axis=1) + vec(_VEC_B_RG_A))
    i_gate = _sigmoid(jnp.concatenate([g[:, V7X_LANES:] for g in gates], axis=1) + vec(_VEC_B_RG_X))
    a = jnp.exp(vec(_VEC_LAMC) * r_gate)
    a_s[...] = a
    z = 1.0 - a * a
    u_s[...] = jnp.where(z > 0.0, z * lax.rsqrt(z), 0.0) * (i_gate * xr)
    mix = _dot(merged, w_out_ref[...])
    u5b = u5.astype(_BF16)
    for j in range(n_s5_blocks):
        bu_s[:, 2 * bs * j:2 * bs * (j + 1)] = _dot(u5b[:, j * V7X_LANES:(j + 1) * V7X_LANES], s5b_ref[j])
    hc = h_state[...]
    s5c = [(s5_state[:, 2 * bs * j:2 * bs * j + bs], s5_state[:, 2 * bs * j + bs:2 * bs * (j + 1)])
           for j in range(n_s5_blocks)]
    for t in range(steps):
        r0 = t * batch
        hc = a_s[r0:r0 + batch, :] * hc + u_s[r0:r0 + batch, :]
        u_s[r0:r0 + batch, :] = hc
        for j in range(n_s5_blocks):
            re0, im0 = 2 * bs * j, 2 * bs * j + bs
            re, im = s5c[j]
            ar, ai = s5a_ref[0, j], s5a_ref[1, j]
            nre = ar * re - ai * im + bu_s[r0:r0 + batch, re0:re0 + bs]
            nim = ar * im + ai * re + bu_s[r0:r0 + batch, im0:im0 + bs]
            bu_s[r0:r0 + batch, re0:re0 + bs] = nre
            bu_s[r0:r0 + batch, im0:im0 + bs] = nim
            s5c[j] = (nre, nim)
    h_state[...] = hc
    for j in range(n_s5_blocks):
        s5_state[:, 2 * bs * j:2 * bs * j + bs] = s5c[j][0]
        s5_state[:, 2 * bs * j + bs:2 * bs * (j + 1)] = s5c[j][1]
    x_prev = jnp.concatenate([xt_s[1 - slot, j] for j in range(n_blk)], axis=1)
    x1 = _layer_norm(alpha * x_prev + _rows(1.0 + mod(2), steps) * mix, vec(_VEC_LN_G), vec(_VEC_LN_B))
    x1_ref[...] = x1
    h2 = x1 * _rows(1.0 + mod(4), steps) + _rows(mod(3), steps)
    h2b = h2.astype(_BF16)
    h2_ref[...] = _pack_rows(h2)
    ya_s[...] = jax.nn.gelu(in_proj(d, d, _VEC_B_Y))
    lane = lax.broadcasted_iota(jnp.int32, (m, V7X_LANES), 1)
    lane_f = lane.astype(_F32)
    neg_inf = jnp.float32(-jnp.inf)
    logits = jnp.where(lane < n_experts, _dot(h2b, w_r_ref[...]) + vec(_VEC_B_ROUTER, V7X_LANES), neg_inf)
    ga_s[...] = _sigmoid(in_proj(c1, d, _VEC_B_GA))
    gb_s[...] = _sigmoid(in_proj(c1 + d, d, _VEC_B_GB))
    onehot = jnp.zeros((m, V7X_LANES), _F32)
    picks, vals = [], []
    for _ in range(TOP_K):
        v = jnp.max(logits, axis=-1, keepdims=True)
        p = jnp.min(jnp.where(logits == v, lane_f, float(V7X_LANES)), axis=-1, keepdims=True)
        hit = lane_f == p
        onehot = jnp.where(hit, 1.0, onehot)
        logits = jnp.where(hit, neg_inf, logits)
        picks.append(p)
        vals.append(v)
    exps = [jnp.exp(v - vals[0]) for v in vals]
    inv_den = 1.0 / functools.reduce(lambda s, e: s + e, exps)
    row = lax.broadcasted_iota(jnp.int32, (m, m), 0)
    col = lax.broadcasted_iota(jnp.int32, (m, m), 1)
    earlier = jnp.where(col < row, 1.0, 0.0).astype(_BF16)
    before = _dot(earlier, onehot.astype(_BF16)) + cnt_s[0:1, :]
    prob_out = jnp.zeros((m, V7X_LANES), _F32)
    dest_out = jnp.zeros((m, V7X_LANES), _F32)
    for k in range(TOP_K):
        rank_k = jnp.sum(jnp.where(lane_f == picks[k], before, 0.0), axis=-1, keepdims=True)
        prob_out = jnp.where(lane == k, exps[k] * inv_den, prob_out)
        dest_out = jnp.where(lane == k, picks[k] * float(region_rows) + rank_k, dest_out)
    prob_ref[...] = prob_out
    dest_t = dest_out.T[0:V7X_SUBLANES, :].astype(jnp.int32)
    for j in range(m // V7X_LANES):
        dest_ref[j] = dest_t[:, j * V7X_LANES:(j + 1) * V7X_LANES]
    has_prev = jnp.where(step > 0, 1.0, 0.0)
    cnt_new = cnt_s[...] + has_prev * jnp.sum(onehot, axis=0, keepdims=True)
    cnt_s[...] = cnt_new
    cnt_ref[...] = cnt_new


def _mixer(x, mod, p, *, alpha, n_experts):
    batch, seq, d = x.shape
    tokens = batch * seq
    steps = MIXER_STEPS
    m = steps * batch
    n_chunks = seq // steps
    n_s5_blocks = p["s5b"].shape[0]
    s5_lanes = n_s5_blocks * 2 * S5_BLOCK_STATES
    halo = (CONV_WIDTH - 1) * batch

    def const(a):
        nd = a.ndim
        return pl.BlockSpec(a.shape, lambda i, nd=nd: (0,) * nd, pipeline_mode=pl.Buffered(1))

    weights = [p["vecs"], p["w_in"], p["wg"], p["w_rnn"], p["s5a"], p["s5b"], p["s5c"], p["w_glu"], p["w_out"],
               p["w_r"]]
    prev = lambda i: jnp.maximum(i - 1, 0)
    row_spec = lambda width: pl.BlockSpec((m, width), lambda i: (prev(i), 0))
    chunks = m // V7X_LANES
    out_shape = (
        jax.ShapeDtypeStruct((tokens, d), _F32),
        jax.ShapeDtypeStruct((tokens, d // 2), jnp.int32),
        jax.ShapeDtypeStruct((tokens // V7X_LANES, V7X_SUBLANES, V7X_LANES), jnp.int32),
        jax.ShapeDtypeStruct((tokens, V7X_LANES), _F32),
        jax.ShapeDtypeStruct((V7X_SUBLANES, V7X_LANES), _F32),
    )
    act = pltpu.VMEM((m, d), _F32)
    scratch = [
        pltpu.VMEM((2, d // V7X_LANES, m, V7X_LANES), _F32),
        pltpu.VMEM((m + halo, d), _F32),
        act, act,
        pltpu.VMEM((m, s5_lanes), _F32),
        pltpu.VMEM((m, n_s5_blocks * V7X_LANES), _F32),
        act, act, act,
        pltpu.VMEM((batch, d), _F32),
        pltpu.VMEM((batch, s5_lanes), _F32),
        pltpu.VMEM((V7X_SUBLANES, V7X_LANES), _F32),
    ]
    weight_bytes = sum(w.size * w.dtype.itemsize for w in weights)
    act_bytes = m * d * 4
    vmem = weight_bytes + 32 * act_bytes
    kern = functools.partial(_mixer_kernel, alpha, steps, batch, d, n_s5_blocks, n_experts, tokens)
    return pl.pallas_call(
        kern,
        grid=(n_chunks + 1,),
        in_specs=[pl.BlockSpec((batch, steps, d), lambda i: (0, jnp.minimum(i, n_chunks - 1), 0)),
                  const(mod)] + [const(w) for w in weights],
        out_specs=(row_spec(d), row_spec(d // 2),
                   pl.BlockSpec((chunks, V7X_SUBLANES, V7X_LANES), lambda i: (prev(i), 0, 0)),
                   row_spec(V7X_LANES),
                   pl.BlockSpec((V7X_SUBLANES, V7X_LANES), lambda i: (0, 0))),
        out_shape=out_shape,
        scratch_shapes=scratch,
        compiler_params=pltpu.CompilerParams(dimension_semantics=("arbitrary",),
                                             vmem_limit_bytes=_vmem_limit(vmem)),
        name="mixer",
    )(x, mod, *weights)


def _expert_kernel(d_ff, tile_e_ref, tile_blk_ref, tile_rows_ref, tile_first_ref, tile_slot_ref, tile_next_ref,
                   x_ref, wgu_hbm, bgu_ref, wd_hbm, bd_ref, y_ref, wgu_f, wd_f, wgu_s, wd_s, sems):
    i = pl.program_id(0)
    d = wgu_f.shape[1]
    r = x_ref.shape[0]

    def weight_copies(expert, slot):
        return (pltpu.make_async_copy(wgu_hbm.at[expert], wgu_f.at[slot], sems.at[slot, 0]),
                pltpu.make_async_copy(wd_hbm.at[expert], wd_f.at[slot], sems.at[slot, 1]))

    @pl.when(tile_first_ref[i] == 1)
    def _():
        slot = tile_slot_ref[i]

        @pl.when(i == 0)
        def _():
            for cp in weight_copies(tile_e_ref[i], slot):
                cp.start()

        for cp in weight_copies(tile_e_ref[i], slot):
            cp.wait()

        @pl.when(tile_next_ref[i] >= 0)
        def _():
            for cp in weight_copies(tile_next_ref[i], 1 - slot):
                cp.start()

        def cast(c, carry):
            r0 = pl.multiple_of(c * WEIGHT_CAST_ROWS, WEIGHT_CAST_ROWS)
            wgu_s[pl.ds(r0, WEIGHT_CAST_ROWS), :] = wgu_f[slot, pl.ds(r0, WEIGHT_CAST_ROWS), :].astype(_BF16)
            wd_s[pl.ds(r0, WEIGHT_CAST_ROWS), :] = wd_f[slot, pl.ds(r0, WEIGHT_CAST_ROWS), :].astype(_BF16)
            return carry

        lax.fori_loop(0, d // WEIGHT_CAST_ROWS, cast, 0)

    def mlp(rows):
        lo, hi = _unpack_rows(x_ref[0:rows, :])
        x = jnp.concatenate([lo, hi], axis=1).astype(_BF16)
        gu = _dot(x, wgu_s[...]) + bgu_ref[0]
        gate = jnp.minimum(gu[:, :d_ff], SWIGLU_LIMIT)
        up = jnp.clip(gu[:, d_ff:], -SWIGLU_LIMIT, SWIGLU_LIMIT)
        act = gate * _sigmoid(SWIGLU_ALPHA * gate) * (up + 1.0)
        y_ref[0:rows, :] = _pack_rows(_dot(act.astype(_BF16), wd_s[...]) + bd_ref[0])
        if rows < r:
            y_ref[rows:r, :] = jnp.zeros((r - rows, y_ref.shape[1]), y_ref.dtype)

    @pl.when(tile_rows_ref[i] == r)
    def _():
        mlp(r)

    @pl.when(tile_rows_ref[i] == r // 2)
    def _():
        mlp(r // 2)

    @pl.when(tile_rows_ref[i] == 0)
    def _():
        y_ref[...] = jnp.zeros_like(y_ref)


def _experts(xb, schedule, w_gu, b_gu, w_down, b_down):
    n_rows, half = xb.shape
    n_experts, d, two_ff = w_gu.shape
    d_ff = two_ff // 2
    assert d_ff == d, "the weight cast loop walks w_gu and w_down rows together"
    r = EXPERT_ROWS
    vmem = 2 * (d * two_ff + d_ff * d) * 4 + (d * two_ff + d_ff * d) * 2 + 8 * r * half * 4 + 6 * r * two_ff * 4
    n_sched = len(schedule)
    tile = lambda i, *s: (s[1][i], 0)
    expert = lambda i, *s: (s[0][i], 0, 0)
    grid_spec = pltpu.PrefetchScalarGridSpec(
        num_scalar_prefetch=n_sched,
        grid=(schedule[0].shape[0],),
        in_specs=[
            pl.BlockSpec((r, half), tile),
            pl.BlockSpec(memory_space=pl.ANY),
            pl.BlockSpec((1, 1, two_ff), expert),
            pl.BlockSpec(memory_space=pl.ANY),
            pl.BlockSpec((1, 1, d), expert),
        ],
        out_specs=pl.BlockSpec((r, half), tile),
        scratch_shapes=[pltpu.VMEM((2, d, two_ff), _F32), pltpu.VMEM((2, d_ff, d), _F32),
                        pltpu.VMEM((d, two_ff), _BF16), pltpu.VMEM((d_ff, d), _BF16),
                        pltpu.SemaphoreType.DMA((2, 2))],
    )
    return pl.pallas_call(
        functools.partial(_expert_kernel, d_ff),
        grid_spec=grid_spec,
        out_shape=jax.ShapeDtypeStruct((n_rows, half), jnp.int32),
        compiler_params=pltpu.CompilerParams(dimension_semantics=("arbitrary",),
                                             vmem_limit_bytes=_vmem_limit(vmem)),
        name="experts",
    )(*schedule, xb, w_gu, b_gu.reshape(n_experts, 1, two_ff), w_down, b_down.reshape(n_experts, 1, d))


def _sc_workers():
    info = plsc.get_sparse_core_info()
    return info.num_cores, info.num_subcores


def _dispatch(h2w, dest_c, n_rows):
    tokens, width = h2w.shape
    n_chunks, _, chunk = dest_c.shape
    nc, ns = _sc_workers()
    per_w = n_chunks // (nc * ns)
    assert per_w * nc * ns == n_chunks

    @functools.partial(
        pl.kernel, mesh=plsc.VectorSubcoreMesh(core_axis_name="c", subcore_axis_name="s"),
        out_type=jax.ShapeDtypeStruct((n_rows, width), h2w.dtype),
        scratch_types=[pltpu.VMEM(dest_c.shape[1:], jnp.int32), pltpu.VMEM((chunk, width), h2w.dtype)],
    )
    def scatter_rows(h_hbm, d_hbm, o_hbm, idx_v, rows_v):
        wid = lax.axis_index("s") * nc + lax.axis_index("c")

        @pl.loop(0, per_w)
        def _(j):
            blk = wid * per_w + j
            pltpu.sync_copy(d_hbm.at[blk], idx_v)
            pltpu.sync_copy(h_hbm.at[pl.ds(pl.multiple_of(blk * chunk, chunk), chunk)], rows_v)
            for k in range(TOP_K):
                pltpu.sync_copy(rows_v, o_hbm.at[idx_v.at[k]])

    return scatter_rows(h2w, dest_c)


def _collect(yb, dest_c):
    _, width = yb.shape
    n_chunks, _, chunk = dest_c.shape
    nc, ns = _sc_workers()
    per_w = n_chunks // (nc * ns)
    assert per_w * nc * ns == n_chunks

    @functools.partial(
        pl.kernel, mesh=plsc.VectorSubcoreMesh(core_axis_name="c", subcore_axis_name="s"),
        out_type=jax.ShapeDtypeStruct((TOP_K, n_chunks * chunk, width), yb.dtype),
        scratch_types=[pltpu.VMEM(dest_c.shape[1:], jnp.int32), pltpu.VMEM((chunk, width), yb.dtype)],
    )
    def gather_rows(y_hbm, d_hbm, o_hbm, idx_v, rows_v):
        wid = lax.axis_index("s") * nc + lax.axis_index("c")

        @pl.loop(0, per_w)
        def _(j):
            blk = wid * per_w + j
            pltpu.sync_copy(d_hbm.at[blk], idx_v)
            for k in range(TOP_K):
                pltpu.sync_copy(y_hbm.at[idx_v.at[k]], rows_v)
                pltpu.sync_copy(rows_v, o_hbm.at[k, pl.ds(pl.multiple_of(blk * chunk, chunk), chunk)])

    return gather_rows(yb, dest_c)


def _combine_kernel(alpha, steps, x1_ref, yg_ref, prob_ref, mod_ref, ln_g_ref, ln_b_ref, o_ref, ot_s):
    d = x1_ref.shape[1]
    batch = o_ref.shape[0]
    ffn_lo = jnp.zeros((x1_ref.shape[0], d // 2), _F32)
    ffn_hi = jnp.zeros((x1_ref.shape[0], d // 2), _F32)
    for k in range(TOP_K):
        lo, hi = _unpack_rows(yg_ref[k])
        ffn_lo = ffn_lo + prob_ref[:, k:k + 1] * lo
        ffn_hi = ffn_hi + prob_ref[:, k:k + 1] * hi
    ffn = jnp.concatenate([ffn_lo, ffn_hi], axis=1)
    gate = _rows(1.0 + mod_ref[:, 5 * d:6 * d], steps)
    out = _layer_norm(alpha * x1_ref[...] + gate * ffn, ln_g_ref[...], ln_b_ref[...])
    n_blk = d // V7X_LANES
    for j in range(n_blk):
        ot_s[j] = out[:, j * V7X_LANES:(j + 1) * V7X_LANES]
    for b in range(batch):
        for j in range(n_blk):
            o_ref[b, :, j * V7X_LANES:(j + 1) * V7X_LANES] = ot_s[j, pl.ds(b, steps, stride=batch), :]


def _combine(x1, yg, prob, mod, ln_g, ln_b, *, alpha, batch):
    tokens, d = x1.shape
    rows = COMBINE_ROWS
    steps = rows // batch
    const = lambda a: pl.BlockSpec(a.shape, lambda i: (0, 0))
    return pl.pallas_call(
        functools.partial(_combine_kernel, alpha, steps),
        grid=(tokens // rows,),
        in_specs=[
            pl.BlockSpec((rows, d), lambda i: (i, 0)),
            pl.BlockSpec((TOP_K, rows, d // 2), lambda i: (0, i, 0)),
            pl.BlockSpec((rows, V7X_LANES), lambda i: (i, 0)),
            const(mod), const(ln_g), const(ln_b),
        ],
        out_specs=pl.BlockSpec((batch, steps, d), lambda i: (0, i, 0)),
        out_shape=jax.ShapeDtypeStruct((batch, tokens // batch, d), _F32),
        scratch_shapes=[pltpu.VMEM((d // V7X_LANES, rows, V7X_LANES), _F32)],
        compiler_params=pltpu.CompilerParams(dimension_semantics=("parallel",)),
        name="combine",
    )(x1, yg, prob, mod, ln_g, ln_b)


def _block_diag(blocks):
    nb, n, a, b = blocks.shape
    eye = jnp.eye(n, dtype=blocks.dtype)
    return (eye[None, :, None, :, None] * blocks[:, :, :, None, :]).reshape(nb, n * a, n * b)


def _s5_params(lam_re, lam_im, log_dt, b_re, b_im, c_re, c_im):
    groups = lam_re.shape[0]
    nb = groups // S5_BLOCK_GROUPS
    dt = jnp.exp(log_dt)[:, None]
    mag = jnp.exp(lam_re * dt)
    ab_re, ab_im = mag * jnp.cos(lam_im * dt), mag * jnp.sin(lam_im * dt)
    den = lam_re * lam_re + lam_im * lam_im
    q_re = ((ab_re - 1.0) * lam_re + ab_im * lam_im) / den
    q_im = (ab_im * lam_re - (ab_re - 1.0) * lam_im) / den
    bb_re = q_re[..., None] * b_re - q_im[..., None] * b_im
    bb_im = q_re[..., None] * b_im + q_im[..., None] * b_re

    def per_block(a):
        return jnp.swapaxes(a.reshape(nb, S5_BLOCK_GROUPS, *a.shape[1:]), 2, 3)

    bmat = jnp.concatenate([_block_diag(per_block(bb_re)), _block_diag(per_block(bb_im))], axis=2)
    cmat = jnp.concatenate([_block_diag(per_block(c_re)), -_block_diag(per_block(c_im))], axis=1)
    s5a = jnp.broadcast_to(jnp.stack([ab_re, ab_im]).reshape(2, nb, 1, S5_BLOCK_STATES),
                           (2, nb, V7X_SUBLANES, S5_BLOCK_STATES))
    return dict(s5a=s5a, s5b=bmat.astype(_BF16), s5c=cmat.astype(_BF16))


def _mixer_vectors(d, b_in, conv_w, conv_b, b_rg_a, b_rg_x, lru_lambda, ln_g, ln_b, s5_d, b_router):
    pad = lambda v: jnp.pad(v, (0, d - v.shape[0]))
    s5w = s5_d.size
    rows = [b_in[0:d], b_in[d:2 * d], pad(b_in[2 * d:2 * d + s5w]), b_in[2 * d + s5w:3 * d + s5w],
            b_in[3 * d + s5w:4 * d + s5w], *conv_w, conv_b, b_rg_a, b_rg_x,
            -LRU_C * jax.nn.softplus(-lru_lambda), ln_g, ln_b, pad(s5_d.reshape(-1)), pad(b_router)]
    rows += [jnp.zeros((d,), _F32)] * (-len(rows) % V7X_SUBLANES)
    return jnp.stack(rows)


def kernel(x, c, w_ada, b_ada, w_in, b_in, conv_w, conv_b, w_rg_a, b_rg_a, w_rg_x, b_rg_x, lru_lambda, w_rnn_out, s5_lambda_re, s5_lambda_im, s5_log_dt, s5_b_re, s5_b_im, s5_c_re, s5_c_im, s5_d, w_glu, w_out, ln1_g, ln1_b, w_router, b_router, w_gu, b_gu, w_down, b_down, ln2_g, ln2_b):
    batch, seq, d = x.shape
    depth = w_ada.shape[0]
    n_experts = w_router.shape[-1]
    tokens = batch * seq
    alpha = (2.0 * depth) ** 0.25
    assert batch == V7X_SUBLANES and d % V7X_LANES == 0 and n_experts <= V7X_LANES
    assert seq % MIXER_STEPS == 0 and tokens % COMBINE_ROWS == 0 and tokens % EXPERT_ROWS == 0
    assert (MIXER_STEPS * batch) % V7X_LANES == 0

    for l in range(depth):
        mod = _ada(c, w_ada[l], b_ada[l])
        p = dict(
            vecs=_mixer_vectors(d, b_in[l], conv_w[l], conv_b[l], b_rg_a[l], b_rg_x[l], lru_lambda[l],
                                ln1_g[l], ln1_b[l], s5_d[l], b_router[l]),
            w_in=w_in[l].astype(_BF16),
            wg=jnp.concatenate([w_rg_a[l], w_rg_x[l]], axis=-1).astype(_BF16),
            w_rnn=w_rnn_out[l].astype(_BF16), w_glu=w_glu[l].astype(_BF16), w_out=w_out[l].astype(_BF16),
            w_r=jnp.pad(w_router[l], ((0, 0), (0, V7X_LANES - n_experts))).astype(_BF16),
            **_s5_params(s5_lambda_re[l], s5_lambda_im[l], s5_log_dt[l], s5_b_re[l], s5_b_im[l],
                         s5_c_re[l], s5_c_im[l]),
        )
        x1, h2w, dest_c, prob, cnt = _mixer(x, mod, p, alpha=alpha, n_experts=n_experts)

        r = EXPERT_ROWS
        blocks_per_region = tokens // r
        spare_blk = n_experts * blocks_per_region
        n_tiles = -(-(tokens * TOP_K + n_experts * (r - 1)) // r)
        counts = cnt[0, :n_experts].astype(jnp.int32)
        tiles_e = (counts + r - 1) // r
        tile_end = jnp.cumsum(tiles_e)
        n_used = tile_end[-1:]
        t_ids = jnp.arange(n_tiles, dtype=jnp.int32)
        done = (tile_end[None, :] <= t_ids[:, None]).astype(jnp.int32)
        tile_e = jnp.minimum(jnp.sum(done, axis=1), n_experts - 1)
        first_tile = jnp.sum(done * tiles_e[None, :], axis=1)
        used = t_ids < n_used
        tile_blk = jnp.where(used, tile_e * blocks_per_region + t_ids - first_tile, spare_blk)
        e_ids = jnp.arange(n_experts, dtype=jnp.int32)[None, :]
        own = (e_ids == tile_e[:, None]).astype(jnp.int32)
        valid = jnp.sum(own * counts[None, :], axis=1) - (t_ids - first_tile) * r
        tile_rows = jnp.where(used, jnp.where(valid <= r // 2, r // 2, r), 0)
        has_tiles = (tiles_e > 0).astype(jnp.int32)[None, :]
        tile_first = (used & (t_ids == first_tile)).astype(jnp.int32)
        tile_slot = jnp.sum(done * has_tiles, axis=1) % 2
        later = jnp.where((has_tiles > 0) & (e_ids > tile_e[:, None]), e_ids, n_experts)
        tile_next = jnp.min(later, axis=1)
        tile_next = jnp.where(tile_next < n_experts, tile_next, -1)

        xb = _dispatch(h2w, dest_c, (spare_blk + 1) * r)
        yb = _experts(xb, (tile_e, tile_blk, tile_rows, tile_first, tile_slot, tile_next),
                      w_gu[l], b_gu[l], w_down[l], b_down[l])
        yg = _collect(yb, dest_c)
        x = _combine(x1, yg, prob, mod, ln2_g[l].reshape(1, -1), ln2_b[l].reshape(1, -1),
                     alpha=alpha, batch=batch)
    return x
```

```python
import functools

import jax
import jax.numpy as jnp
from jax import lax
from jax.experimental import pallas as pl
from jax.experimental.pallas import tpu as pltpu
from jax.experimental.pallas import tpu_sc as plsc

V7X_SUBLANES = 8
V7X_LANES = 128
V7X_VMEM_BYTES = 64 * 1024 * 1024

CONV_WIDTH = 4
LRU_C = 8.0
S5_GROUP = 16
S5_STATE = 64
TOP_K = 4
SWIGLU_LIMIT = 7.0
SWIGLU_ALPHA = 1.702
LN_EPS = 1e-5

S5_BLOCK_GROUPS = V7X_LANES // S5_GROUP
S5_BLOCK_STATES = S5_BLOCK_GROUPS * S5_STATE

(_VEC_B_X, _VEC_B_Y, _VEC_B_U5, _VEC_B_GA, _VEC_B_GB, _VEC_CONV_W) = range(6)
(_VEC_CONV_B, _VEC_B_RG_A, _VEC_B_RG_X, _VEC_LAMC, _VEC_LN_G, _VEC_LN_B, _VEC_S5_D, _VEC_B_ROUTER) = range(
    _VEC_CONV_W + CONV_WIDTH, _VEC_CONV_W + CONV_WIDTH + 8)

MIXER_STEPS = 64
EXPERT_ROWS = 512
COMBINE_ROWS = 1024
WEIGHT_CAST_ROWS = 64

_BF16 = jnp.bfloat16
_F32 = jnp.float32


def _dot(a, b):
    return jnp.dot(a, b, preferred_element_type=_F32)


def _sigmoid(v):
    return 0.5 * jnp.tanh(0.5 * v) + 0.5


def _vmem_limit(nbytes):
    return int(min(nbytes, V7X_VMEM_BYTES - 4 * 1024 * 1024))


def _layer_norm(z, gain, bias):
    mu = jnp.mean(z, axis=-1, keepdims=True)
    zc = z - mu
    var = jnp.mean(zc * zc, axis=-1, keepdims=True)
    return zc * lax.rsqrt(var + LN_EPS) * gain + bias


def _rows(v, steps):
    return jnp.tile(v, (steps, 1))


_HI_MASK = 0xFFFF0000


def _pack_rows(v):
    half = v.shape[1] // 2
    bits = lax.bitcast_convert_type(v.astype(_BF16).astype(_F32), jnp.uint32)
    packed = (bits[:, :half] >> 16) | (bits[:, half:] & jnp.uint32(_HI_MASK))
    return lax.bitcast_convert_type(packed, jnp.int32)


def _unpack_rows(w):
    bits = lax.bitcast_convert_type(w, jnp.uint32)
    lo = lax.bitcast_convert_type(bits << 16, _F32)
    hi = lax.bitcast_convert_type(bits & jnp.uint32(_HI_MASK), _F32)
    return lo, hi


def _ada_kernel(c_ref, w_ref, b_ref, o_ref):
    c = c_ref[...]
    c_act = (c * _sigmoid(c)).astype(_BF16)
    o_ref[...] = _dot(c_act, w_ref[...].astype(_BF16)) + b_ref[...]


def _ada(c, w_ada, b_ada):
    batch, d = c.shape
    n_out = w_ada.shape[1]
    return pl.pallas_call(
        _ada_kernel,
        grid=(n_out // d,),
        in_specs=[
            pl.BlockSpec((batch, d), lambda j: (0, 0)),
            pl.BlockSpec((d, d), lambda j: (0, j)),
            pl.BlockSpec((1, d), lambda j: (0, j)),
        ],
        out_specs=pl.BlockSpec((batch, d), lambda j: (0, j)),
        out_shape=jax.ShapeDtypeStruct((batch, n_out), _F32),
        name="ada",
    )(c, w_ada, b_ada.reshape(1, n_out))


def _mixer_kernel(alpha, steps, batch, d, n_s5_blocks, n_experts, region_rows,
                  x_ref, mod_ref, vecs_ref, w_in_ref, wg_ref, w_rnn_ref, s5a_ref, s5b_ref, s5c_ref,
                  w_glu_ref, w_out_ref, w_r_ref,
                  x1_ref, h2_ref, dest_ref, prob_ref, cnt_ref,
                  xt_s, xc_s, a_s, u_s, bu_s, u5_s, ya_s, ga_s, gb_s, h_state, s5_state, cnt_s):
    m = steps * batch
    halo = (CONV_WIDTH - 1) * batch
    s5w = n_s5_blocks * V7X_LANES
    n_blk = d // V7X_LANES
    bs = S5_BLOCK_STATES
    step = pl.program_id(0)

    @pl.when(step == 0)
    def _():
        xc_s[0:halo, :] = jnp.zeros((halo, d), _F32)
        h_state[...] = jnp.zeros_like(h_state)
        s5_state[...] = jnp.zeros_like(s5_state)
        cnt_s[...] = jnp.zeros_like(cnt_s)
        for ref in (xt_s, u_s, bu_s, u5_s, ya_s, ga_s, gb_s):
            ref[...] = jnp.zeros_like(ref)

    def mod(k):
        return mod_ref[:, k * d:(k + 1) * d]

    slot = lax.rem(step, 2)
    for b in range(batch):
        for j in range(n_blk):
            xt_s[slot, j, pl.ds(b, steps, stride=batch), :] = x_ref[b, :, j * V7X_LANES:(j + 1) * V7X_LANES]
    x = jnp.concatenate([xt_s[slot, j] for j in range(n_blk)], axis=1)
    hb = (x * _rows(1.0 + mod(1), steps) + _rows(mod(0), steps)).astype(_BF16)

    def vec(k, width=d):
        return vecs_ref[k:k + 1, 0:width]

    def in_proj(c0, width, bias_row):
        return _dot(hb, w_in_ref[:, c0:c0 + width]) + vec(bias_row, width)

    c0 = 2 * d
    c1 = c0 + s5w
    branch_a = _dot((ya_s[...] * u_s[...]).astype(_BF16), w_rnn_ref[...])
    y5 = jnp.concatenate(
        [_dot(bu_s[:, 2 * bs * j:2 * bs * (j + 1)].astype(_BF16), s5c_ref[j]) for j in range(n_s5_blocks)],
        axis=1) + vec(_VEC_S5_D, s5w) * u5_s[...]
    xc_s[halo:halo + m, :] = in_proj(0, d, _VEC_B_X)
    glu = _dot(jax.nn.gelu(y5).astype(_BF16), w_glu_ref[...])
    xr = jnp.zeros((m, d), _F32) + vec(_VEC_CONV_B)
    for k in range(CONV_WIDTH):
        xr = xr + vec(_VEC_CONV_W + k) * xc_s[k * batch:k * batch + m, :]
    xc_s[0:halo, :] = xc_s[m:m + halo, :]
    xrb = xr.astype(_BF16)
    gates = [_dot(xrb[:, j * V7X_LANES:(j + 1) * V7X_LANES], wg_ref[j]) for j in range(n_blk)]
    merged = (ga_s[...] * branch_a + gb_s[...] * (glu[:, :d] * _sigmoid(glu[:, d:]))).astype(_BF16)
    u5 = in_proj(c0, s5w, _VEC_B_U5)
    u5_s[...] = u5
    r_gate = _sigmoid(jnp.concatenate([g[:, :V7X_LANES] for g in gates], axis=1) + vec(_VEC_B_RG_A))
    i_gate = _sigmoid(jnp.concatenate([g[:, V7X_LANES:] for g in gates], axis=1) + vec(_VEC_B_RG_X))
    a = jnp.exp(vec(_VEC_LAMC) * r_gate)
    a_s[...] = a
    z = 1.0 - a * a
    u_s[...] = jnp.where(z > 0.0, z * lax.rsqrt(z), 0.0) * (i_gate * xr)
    mix = _dot(merged, w_out_ref[...])
    u5b = u5.astype(_BF16)
    for j in range(n_s5_blocks):
        bu_s[:, 2 * bs * j:2 * bs * (j + 1)] = _dot(u5b[:, j * V7X_LANES:(j + 1) * V7X_LANES], s5b_ref[j])
    hc = h_state[...]
    s5c = [(s5_state[:, 2 * bs * j:2 * bs * j + bs], s5_state[:, 2 * bs * j + bs:2 * bs * (j + 1)])
           for j in range(n_s5_blocks)]
    for t in range(steps):
        r0 = t * batch
        hc = a_s[r0:r0 + batch, :] * hc + u_s[r0:r0 + batch, :]
        u_s[r0:r0 + batch, :] = hc
        for j in range(n_s5_blocks):
            re0, im0 = 2 * bs * j, 2 * bs * j + bs
            re, im = s5c[j]
            ar, ai = s5a_ref[0, j], s5a_ref[1, j]
            nre = ar * re - ai * im + bu_s[r0:r0 + batch, re0:re0 + bs]
            nim = ar * im + ai * re + bu_s[r0:r0 + batch, im0:im0 + bs]
            bu_s[r0:r0 + batch, re0:re0 + bs] = nre
            bu_s[r0:r0 + batch, im0:im0 + bs] = nim
            s5c[j] = (nre, nim)
    h_state[...] = hc
    for j in range(n_s5_blocks):
        s5_state[:, 2 * bs * j:2 * bs * j + bs] = s5c[j][0]
        s5_state[:, 2 * bs * j + bs:2 * bs * (j + 1)] = s5c[j][1]
    x_prev = jnp.concatenate([xt_s[1 - slot, j] for j in range(n_blk)], axis=1)
    x1 = _layer_norm(alpha * x_prev + _rows(1.0 + mod(2), steps) * mix, vec(_VEC_LN_G), vec(_VEC_LN_B))
    x1_ref[...] = x1
    h2 = x1 * _rows(1.0 + mod(4), steps) + _rows(mod(3), steps)
    h2b = h2.astype(_BF16)
    h2_ref[...] = _pack_rows(h2)
    ya_s[...] = jax.nn.gelu(in_proj(d, d, _VEC_B_Y))
    lane = lax.broadcasted_iota(jnp.int32, (m, V7X_LANES), 1)
    lane_f = lane.astype(_F32)
    neg_inf = jnp.float32(-jnp.inf)
    logits = jnp.where(lane < n_experts, _dot(h2b, w_r_ref[...]) + vec(_VEC_B_ROUTER, V7X_LANES), neg_inf)
    ga_s[...] = _sigmoid(in_proj(c1, d, _VEC_B_GA))
    gb_s[...] = _sigmoid(in_proj(c1 + d, d, _VEC_B_GB))
    onehot = jnp.zeros((m, V7X_LANES), _F32)
    picks, vals = [], []
    for _ in range(TOP_K):
        v = jnp.max(logits, axis=-1, keepdims=True)
        p = jnp.min(jnp.where(logits == v, lane_f, float(V7X_LANES)), axis=-1, keepdims=True)
        hit = lane_f == p
        onehot = jnp.where(hit, 1.0, onehot)
        logits = jnp.where(hit, neg_inf, logits)
        picks.append(p)
        vals.append(v)
    exps = [jnp.exp(v - vals[0]) for v in vals]
    inv_den = 1.0 / functools.reduce(lambda s, e: s + e, exps)
    row = lax.broadcasted_iota(jnp.int32, (m, m), 0)
    col = lax.broadcasted_iota(jnp.int32, (m, m), 1)
    earlier = jnp.where(col < row, 1.0, 0.0).astype(_BF16)
    before = _dot(earlier, onehot.astype(_BF16)) + cnt_s[0:1, :]
    prob_out = jnp.zeros((m, V7X_LANES), _F32)
    dest_out = jnp.zeros((m, V7X_LANES), _F32)
    for k in range(TOP_K):
        rank_k = jnp.sum(jnp.where(lane_f == picks[k], before, 0.0), axis=-1, keepdims=True)
        prob_out = jnp.where(lane == k, exps[k] * inv_den, prob_out)
        dest_out = jnp.where(lane == k, picks[k] * float(region_rows) + rank_k, dest_out)
    prob_ref[...] = prob_out
    dest_t = dest_out.T[0:V7X_SUBLANES, :].astype(jnp.int32)
    for j in range(m // V7X_LANES):
        dest_ref[j] = dest_t[:, j * V7X_LANES:(j + 1) * V7X_LANES]
    has_prev = jnp.where(step > 0, 1.0, 0.0)
    cnt_new = cnt_s[...] + has_prev * jnp.sum(onehot, axis=0, keepdims=True)
    cnt_s[...] = cnt_new
    cnt_ref[...] = cnt_new


def _mixer(x, mod, p, *, alpha, n_experts):
    batch, seq, d = x.shape
    tokens = batch * seq
    steps = MIXER_STEPS
    m = steps * batch
    n_chunks = seq // steps
    n_s5_blocks = p["s5b"].shape[0]
    s5_lanes = n_s5_blocks * 2 * S5_BLOCK_STATES
    halo = (CONV_WIDTH - 1) * batch

    def const(a):
        nd = a.ndim
        return pl.BlockSpec(a.shape, lambda i, nd=nd: (0,) * nd, pipeline_mode=pl.Buffered(1))

    weights = [p["vecs"], p["w_in"], p["wg"], p["w_rnn"], p["s5a"], p["s5b"], p["s5c"], p["w_glu"], p["w_out"],
               p["w_r"]]
    prev = lambda i: jnp.maximum(i - 1, 0)
    row_spec = lambda width: pl.BlockSpec((m, width), lambda i: (prev(i), 0))
    chunks = m // V7X_LANES
    out_shape = (
        jax.ShapeDtypeStruct((tokens, d), _F32),
        jax.ShapeDtypeStruct((tokens, d // 2), jnp.int32),
        jax.ShapeDtypeStruct((tokens // V7X_LANES, V7X_SUBLANES, V7X_LANES), jnp.int32),
        jax.ShapeDtypeStruct((tokens, V7X_LANES), _F32),
        jax.ShapeDtypeStruct((V7X_SUBLANES, V7X_LANES), _F32),
    )
    act = pltpu.VMEM((m, d), _F32)
    scratch = [
        pltpu.VMEM((2, d // V7X_LANES, m, V7X_LANES), _F32),
        pltpu.VMEM((m + halo, d), _F32),
        act, act,
        pltpu.VMEM((m, s5_lanes), _F32),
        pltpu.VMEM((m, n_s5_blocks * V7X_LANES), _F32),
        act, act, act,
        pltpu.VMEM((batch, d), _F32),
        pltpu.VMEM((batch, s5_lanes), _F32),
        pltpu.VMEM((V7X_SUBLANES, V7X_LANES), _F32),
    ]
    weight_bytes = sum(w.size * w.dtype.itemsize for w in weights)
    act_bytes = m * d * 4
    vmem = weight_bytes + 32 * act_bytes
    kern = functools.partial(_mixer_kernel, alpha, steps, batch, d, n_s5_blocks, n_experts, tokens)
    return pl.pallas_call(
        kern,
        grid=(n_chunks + 1,),
        in_specs=[pl.BlockSpec((batch, steps, d), lambda i: (0, jnp.minimum(i, n_chunks - 1), 0)),
                  const(mod)] + [const(w) for w in weights],
        out_specs=(row_spec(d), row_spec(d // 2),
                   pl.BlockSpec((chunks, V7X_SUBLANES, V7X_LANES), lambda i: (prev(i), 0, 0)),
                   row_spec(V7X_LANES),
                   pl.BlockSpec((V7X_SUBLANES, V7X_LANES), lambda i: (0, 0))),
        out_shape=out_shape,
        scratch_shapes=scratch,
        compiler_params=pltpu.CompilerParams(dimension_semantics=("arbitrary",),
                                             vmem_limit_bytes=_vmem_limit(vmem)),
        name="mixer",
    )(x, mod, *weights)


def _expert_kernel(d_ff, tile_e_ref, tile_blk_ref, tile_rows_ref, tile_first_ref, tile_slot_ref, tile_next_ref,
                   x_ref, wgu_hbm, bgu_ref, wd_hbm, bd_ref, y_ref, wgu_f, wd_f, wgu_s, wd_s, sems):
    i = pl.program_id(0)
    d = wgu_f.shape[1]
    r = x_ref.shape[0]

    def weight_copies(expert, slot):
        return (pltpu.make_async_copy(wgu_hbm.at[expert], wgu_f.at[slot], sems.at[slot, 0]),
                pltpu.make_async_copy(wd_hbm.at[expert], wd_f.at[slot], sems.at[slot, 1]))

    @pl.when(tile_first_ref[i] == 1)
    def _():
        slot = tile_slot_ref[i]

        @pl.when(i == 0)
        def _():
            for cp in weight_copies(tile_e_ref[i], slot):
                cp.start()

        for cp in weight_copies(tile_e_ref[i], slot):
            cp.wait()

        @pl.when(tile_next_ref[i] >= 0)
        def _():
            for cp in weight_copies(tile_next_ref[i], 1 - slot):
                cp.start()

        def cast(c, carry):
            r0 = pl.multiple_of(c * WEIGHT_CAST_ROWS, WEIGHT_CAST_ROWS)
            wgu_s[pl.ds(r0, WEIGHT_CAST_ROWS), :] = wgu_f[slot, pl.ds(r0, WEIGHT_CAST_ROWS), :].astype(_BF16)
            wd_s[pl.ds(r0, WEIGHT_CAST_ROWS), :] = wd_f[slot, pl.ds(r0, WEIGHT_CAST_ROWS), :].astype(_BF16)
            return carry

        lax.fori_loop(0, d // WEIGHT_CAST_ROWS, cast, 0)

    def mlp(rows):
        lo, hi = _unpack_rows(x_ref[0:rows, :])
        x = jnp.concatenate([lo, hi], axis=1).astype(_BF16)
        gu = _dot(x, wgu_s[...]) + bgu_ref[0]
        gate = jnp.minimum(gu[:, :d_ff], SWIGLU_LIMIT)
        up = jnp.clip(gu[:, d_ff:], -SWIGLU_LIMIT, SWIGLU_LIMIT)
        act = gate * _sigmoid(SWIGLU_ALPHA * gate) * (up + 1.0)
        y_ref[0:rows, :] = _pack_rows(_dot(act.astype(_BF16), wd_s[...]) + bd_ref[0])
        if rows < r:
            y_ref[rows:r, :] = jnp.zeros((r - rows, y_ref.shape[1]), y_ref.dtype)

    @pl.when(tile_rows_ref[i] == r)
    def _():
        mlp(r)

    @pl.when(tile_rows_ref[i] == r // 2)
    def _():
        mlp(r // 2)

    @pl.when(tile_rows_ref[i] == 0)
    def _():
        y_ref[...] = jnp.zeros_like(y_ref)


def _experts(xb, schedule, w_gu, b_gu, w_down, b_down):
    n_rows, half = xb.shape
    n_experts, d, two_ff = w_gu.shape
    d_ff = two_ff // 2
    assert d_ff == d, "the weight cast loop walks w_gu and w_down rows together"
    r = EXPERT_ROWS
    vmem = 2 * (d * two_ff + d_ff * d) * 4 + (d * two_ff + d_ff * d) * 2 + 8 * r * half * 4 + 6 * r * two_ff * 4
    n_sched = len(schedule)
    tile = lambda i, *s: (s[1][i], 0)
    expert = lambda i, *s: (s[0][i], 0, 0)
    grid_spec = pltpu.PrefetchScalarGridSpec(
        num_scalar_prefetch=n_sched,
        grid=(schedule[0].shape[0],),
        in_specs=[
            pl.BlockSpec((r, half), tile),
            pl.BlockSpec(memory_space=pl.ANY),
            pl.BlockSpec((1, 1, two_ff), expert),
            pl.BlockSpec(memory_space=pl.ANY),
            pl.BlockSpec((1, 1, d), expert),
        ],
        out_specs=pl.BlockSpec((r, half), tile),
        scratch_shapes=[pltpu.VMEM((2, d, two_ff), _F32), pltpu.VMEM((2, d_ff, d), _F32),
                        pltpu.VMEM((d, two_ff), _BF16), pltpu.VMEM((d_ff, d), _BF16),
                        pltpu.SemaphoreType.DMA((2, 2))],
    )
    return pl.pallas_call(
        functools.partial(_expert_kernel, d_ff),
        grid_spec=grid_spec,
        out_shape=jax.ShapeDtypeStruct((n_rows, half), jnp.int32),
        compiler_params=pltpu.CompilerParams(dimension_semantics=("arbitrary",),
                                             vmem_limit_bytes=_vmem_limit(vmem)),
        name="experts",
    )(*schedule, xb, w_gu, b_gu.reshape(n_experts, 1, two_ff), w_down, b_down.reshape(n_experts, 1, d))


def _sc_workers():
    info = plsc.get_sparse_core_info()
    return info.num_cores, info.num_subcores


def _dispatch(h2w, dest_c, n_rows):
    tokens, width = h2w.shape
    n_chunks, _, chunk = dest_c.shape
    nc, ns = _sc_workers()
    per_w = n_chunks // (nc * ns)
    assert per_w * nc * ns == n_chunks

    @functools.partial(
        pl.kernel, mesh=plsc.VectorSubcoreMesh(core_axis_name="c", subcore_axis_name="s"),
        out_type=jax.ShapeDtypeStruct((n_rows, width), h2w.dtype),
        scratch_types=[pltpu.VMEM(dest_c.shape[1:], jnp.int32), pltpu.VMEM((chunk, width), h2w.dtype)],
    )
    def scatter_rows(h_hbm, d_hbm, o_hbm, idx_v, rows_v):
        wid = lax.axis_index("s") * nc + lax.axis_index("c")

        @pl.loop(0, per_w)
        def _(j):
            blk = wid * per_w + j
            pltpu.sync_copy(d_hbm.at[blk], idx_v)
            pltpu.sync_copy(h_hbm.at[pl.ds(pl.multiple_of(blk * chunk, chunk), chunk)], rows_v)
            for k in range(TOP_K):
                pltpu.sync_copy(rows_v, o_hbm.at[idx_v.at[k]])

    return scatter_rows(h2w, dest_c)


def _collect(yb, dest_c):
    _, width = yb.shape
    n_chunks, _, chunk = dest_c.shape
    nc, ns = _sc_workers()
    per_w = n_chunks // (nc * ns)
    assert per_w * nc * ns == n_chunks

    @functools.partial(
        pl.kernel, mesh=plsc.VectorSubcoreMesh(core_axis_name="c", subcore_axis_name="s"),
        out_type=jax.ShapeDtypeStruct((TOP_K, n_chunks * chunk, width), yb.dtype),
        scratch_types=[pltpu.VMEM(dest_c.shape[1:], jnp.int32), pltpu.VMEM((chunk, width), yb.dtype)],
    )
    def gather_rows(y_hbm, d_hbm, o_hbm, idx_v, rows_v):
        wid = lax.axis_index("s") * nc + lax.axis_index("c")

        @pl.loop(0, per_w)
        def _(j):
            blk = wid * per_w + j
            pltpu.sync_copy(d_hbm.at[blk], idx_v)
            for k in range(TOP_K):
                pltpu.sync_copy(y_hbm.at[idx_v.at[k]], rows_v)
                pltpu.sync_copy(rows_v, o_hbm.at[k, pl.ds(pl.multiple_of(blk * chunk, chunk), chunk)])

    return gather_rows(yb, dest_c)


def _combine_kernel(alpha, steps, x1_ref, yg_ref, prob_ref, mod_ref, ln_g_ref, ln_b_ref, o_ref, ot_s):
    d = x1_ref.shape[1]
    batch = o_ref.shape[0]
    ffn_lo = jnp.zeros((x1_ref.shape[0], d // 2), _F32)
    ffn_hi = jnp.zeros((x1_ref.shape[0], d // 2), _F32)
    for k in range(TOP_K):
        lo, hi = _unpack_rows(yg_ref[k])
        ffn_lo = ffn_lo + prob_ref[:, k:k + 1] * lo
        ffn_hi = ffn_hi + prob_ref[:, k:k + 1] * hi
    ffn = jnp.concatenate([ffn_lo, ffn_hi], axis=1)
    gate = _rows(1.0 + mod_ref[:, 5 * d:6 * d], steps)
    out = _layer_norm(alpha * x1_ref[...] + gate * ffn, ln_g_ref[...], ln_b_ref[...])
    n_blk = d // V7X_LANES
    for j in range(n_blk):
        ot_s[j] = out[:, j * V7X_LANES:(j + 1) * V7X_LANES]
    for b in range(batch):
        for j in range(n_blk):
            o_ref[b, :, j * V7X_LANES:(j + 1) * V7X_LANES] = ot_s[j, pl.ds(b, steps, stride=batch), :]


def _combine(x1, yg, prob, mod, ln_g, ln_b, *, alpha, batch):
    tokens, d = x1.shape
    rows = COMBINE_ROWS
    steps = rows // batch
    const = lambda a: pl.BlockSpec(a.shape, lambda i: (0, 0))
    return pl.pallas_call(
        functools.partial(_combine_kernel, alpha, steps),
        grid=(tokens // rows,),
        in_specs=[
            pl.BlockSpec((rows, d), lambda i: (i, 0)),
            pl.BlockSpec((TOP_K, rows, d // 2), lambda i: (0, i, 0)),
            pl.BlockSpec((rows, V7X_LANES), lambda i: (i, 0)),
            const(mod), const(ln_g), const(ln_b),
        ],
        out_specs=pl.BlockSpec((batch, steps, d), lambda i: (0, i, 0)),
        out_shape=jax.ShapeDtypeStruct((batch, tokens // batch, d), _F32),
        scratch_shapes=[pltpu.VMEM((d // V7X_LANES, rows, V7X_LANES), _F32)],
        compiler_params=pltpu.CompilerParams(dimension_semantics=("parallel",)),
        name="combine",
    )(x1, yg, prob, mod, ln_g, ln_b)


def _block_diag(blocks):
    nb, n, a, b = blocks.shape
    eye = jnp.eye(n, dtype=blocks.dtype)
    return (eye[None, :, None, :, None] * blocks[:, :, :, None, :]).reshape(nb, n * a, n * b)


def _s5_params(lam_re, lam_im, log_dt, b_re, b_im, c_re, c_im):
    groups = lam_re.shape[0]
    nb = groups // S5_BLOCK_GROUPS
    dt = jnp.exp(log_dt)[:, None]
    mag = jnp.exp(lam_re * dt)
    ab_re, ab_im = mag * jnp.cos(lam_im * dt), mag * jnp.sin(lam_im * dt)
    den = lam_re * lam_re + lam_im * lam_im
    q_re = ((ab_re - 1.0) * lam_re + ab_im * lam_im) / den
    q_im = (ab_im * lam_re - (ab_re - 1.0) * lam_im) / den
    bb_re = q_re[..., None] * b_re - q_im[..., None] * b_im
    bb_im = q_re[..., None] * b_im + q_im[..., None] * b_re

    def per_block(a):
        return jnp.swapaxes(a.reshape(nb, S5_BLOCK_GROUPS, *a.shape[1:]), 2, 3)

    bmat = jnp.concatenate([_block_diag(per_block(bb_re)), _block_diag(per_block(bb_im))], axis=2)
    cmat = jnp.concatenate([_block_diag(per_block(c_re)), -_block_diag(per_block(c_im))], axis=1)
    s5a = jnp.broadcast_to(jnp.stack([ab_re, ab_im]).reshape(2, nb, 1, S5_BLOCK_STATES),
                           (2, nb, V7X_SUBLANES, S5_BLOCK_STATES))
    return dict(s5a=s5a, s5b=bmat.astype(_BF16), s5c=cmat.astype(_BF16))


def _mixer_vectors(d, b_in, conv_w, conv_b, b_rg_a, b_rg_x, lru_lambda, ln_g, ln_b, s5_d, b_router):
    pad = lambda v: jnp.pad(v, (0, d - v.shape[0]))
    s5w = s5_d.size
    rows = [b_in[0:d], b_in[d:2 * d], pad(b_in[2 * d:2 * d + s5w]), b_in[2 * d + s5w:3 * d + s5w],
            b_in[3 * d + s5w:4 * d + s5w], *conv_w, conv_b, b_rg_a, b_rg_x,
            -LRU_C * jax.nn.softplus(-lru_lambda), ln_g, ln_b, pad(s5_d.reshape(-1)), pad(b_router)]
    rows += [jnp.zeros((d,), _F32)] * (-len(rows) % V7X_SUBLANES)
    return jnp.stack(rows)


def kernel(x, c, w_ada, b_ada, w_in, b_in, conv_w, conv_b, w_rg_a, b_rg_a, w_rg_x, b_rg_x, lru_lambda, w_rnn_out, s5_lambda_re, s5_lambda_im, s5_log_dt, s5_b_re, s5_b_im, s5_c_re, s5_c_im, s5_d, w_glu, w_out, ln1_g, ln1_b, w_router, b_router, w_gu, b_gu, w_down, b_down, ln2_g, ln2_b):
    batch, seq, d = x.shape
    depth = w_ada.shape[0]
    n_experts = w_router.shape[-1]
    tokens = batch * seq
    alpha = (2.0 * depth) ** 0.25
    assert batch == V7X_SUBLANES and d % V7X_LANES == 0 and n_experts <= V7X_LANES
    assert seq % MIXER_STEPS == 0 and tokens % COMBINE_ROWS == 0 and tokens % EXPERT_ROWS == 0
    assert (MIXER_STEPS * batch) % V7X_LANES == 0

    for l in range(depth):
        mod = _ada(c, w_ada[l], b_ada[l])
        p = dict(
            vecs=_mixer_vectors(d, b_in[l], conv_w[l], conv_b[l], b_rg_a[l], b_rg_x[l], lru_lambda[l],
                                ln1_g[l], ln1_b[l], s5_d[l], b_router[l]),
            w_in=w_in[l].astype(_BF16),
            wg=jnp.concatenate([w_rg_a[l], w_rg_x[l]], axis=-1).astype(_BF16),
            w_rnn=w_rnn_out[l].astype(_BF16), w_glu=w_glu[l].astype(_BF16), w_out=w_out[l].astype(_BF16),
            w_r=jnp.pad(w_router[l], ((0, 0), (0, V7X_LANES - n_experts))).astype(_BF16),
            **_s5_params(s5_lambda_re[l], s5_lambda_im[l], s5_log_dt[l], s5_b_re[l], s5_b_im[l],
                         s5_c_re[l], s5_c_im[l]),
        )
        x1, h2w, dest_c, prob, cnt = _mixer(x, mod, p, alpha=alpha, n_experts=n_experts)

        r = EXPERT_ROWS
        blocks_per_region = tokens // r
        spare_blk = n_experts * blocks_per_region
        n_tiles = -(-(tokens * TOP_K + n_experts * (r - 1)) // r)
        counts = cnt[0, :n_experts].astype(jnp.int32)
        tiles_e = (counts + r - 1) // r
        tile_end = jnp.cumsum(tiles_e)
        n_used = tile_end[-1:]
        t_ids = jnp.arange(n_tiles, dtype=jnp.int32)
        done = (tile_end[None, :] <= t_ids[:, None]).astype(jnp.int32)
        tile_e = jnp.minimum(jnp.sum(done, axis=1), n_experts - 1)
        first_tile = jnp.sum(done * tiles_e[None, :], axis=1)
        used = t_ids < n_used
        tile_blk = jnp.where(used, tile_e * blocks_per_region + t_ids - first_tile, spare_blk)
        e_ids = jnp.arange(n_experts, dtype=jnp.int32)[None, :]
        own = (e_ids == tile_e[:, None]).astype(jnp.int32)
        valid = jnp.sum(own * counts[None, :], axis=1) - (t_ids - first_tile) * r
        tile_rows = jnp.where(used, jnp.where(valid <= r // 2, r // 2, r), 0)
        has_tiles = (tiles_e > 0).astype(jnp.int32)[None, :]
        tile_first = (used & (t_ids == first_tile)).astype(jnp.int32)
        tile_slot = jnp.sum(done * has_tiles, axis=1) % 2
        later = jnp.where((has_tiles > 0) & (e_ids > tile_e[:, None]), e_ids, n_experts)
        tile_next = jnp.min(later, axis=1)
        tile_next = jnp.where(tile_next < n_experts, tile_next, -1)

        xb = _dispatch(h2w, dest_c, (spare_blk + 1) * r)
        yb = _experts(xb, (tile_e, tile_blk, tile_rows, tile_first, tile_slot, tile_next),
                      w_gu[l], b_gu[l], w_down[l], b_down[l])
        yg = _collect(yb, dest_c)
        x = _combine(x1, yg, prob, mod, ln2_g[l].reshape(1, -1), ln2_b[l].reshape(1, -1),
                     alpha=alpha, batch=batch)
    return x
```

```python
import functools

import jax
import jax.numpy as jnp
from jax import lax
from jax.experimental import pallas as pl
from jax.experimental.pallas import tpu as pltpu
from jax.experimental.pallas import tpu_sc as plsc

V7X_SUBLANES = 8
V7X_LANES = 128
V7X_VMEM_BYTES = 64 * 1024 * 1024

CONV_WIDTH = 4
LRU_C = 8.0
S5_GROUP = 16
S5_STATE = 64
TOP_K = 4
SWIGLU_LIMIT = 7.0
SWIGLU_ALPHA = 1.702
LN_EPS = 1e-5

S5_BLOCK_GROUPS = V7X_LANES // S5_GROUP
S5_BLOCK_STATES = S5_BLOCK_GROUPS * S5_STATE

(_VEC_B_X, _VEC_B_Y, _VEC_B_U5, _VEC_B_GA, _VEC_B_GB, _VEC_CONV_W) = range(6)
(_VEC_CONV_B, _VEC_B_RG_A, _VEC_B_RG_X, _VEC_LAMC, _VEC_LN_G, _VEC_LN_B, _VEC_S5_D, _VEC_B_ROUTER) = range(
    _VEC_CONV_W + CONV_WIDTH, _VEC_CONV_W + CONV_WIDTH + 8)

MIXER_STEPS = 64
EXPERT_ROWS = 512
COMBINE_ROWS = 1024
WEIGHT_CAST_ROWS = 64

_BF16 = jnp.bfloat16
_F32 = jnp.float32


def _dot(a, b):
    return jnp.dot(a, b, preferred_element_type=_F32)


def _sigmoid(v):
    return 0.5 * jnp.tanh(0.5 * v) + 0.5


def _vmem_limit(nbytes):
    return int(min(nbytes, V7X_VMEM_BYTES - 4 * 1024 * 1024))


def _layer_norm(z, gain, bias):
    mu = jnp.mean(z, axis=-1, keepdims=True)
    zc = z - mu
    var = jnp.mean(zc * zc, axis=-1, keepdims=True)
    return zc * lax.rsqrt(var + LN_EPS) * gain + bias


def _rows(v, steps):
    return jnp.tile(v, (steps, 1))


_HI_MASK = 0xFFFF0000


def _pack_rows(v):
    half = v.shape[1] // 2
    bits = lax.bitcast_convert_type(v.astype(_BF16).astype(_F32), jnp.uint32)
    packed = (bits[:, :half] >> 16) | (bits[:, half:] & jnp.uint32(_HI_MASK))
    return lax.bitcast_convert_type(packed, jnp.int32)


def _unpack_rows(w):
    bits = lax.bitcast_convert_type(w, jnp.uint32)
    lo = lax.bitcast_convert_type(bits << 16, _F32)
    hi = lax.bitcast_convert_type(bits & jnp.uint32(_HI_MASK), _F32)
    return lo, hi


def _ada_kernel(c_ref, w_ref, b_ref, o_ref):
    c = c_ref[...]
    c_act = (c * _sigmoid(c)).astype(_BF16)
    o_ref[...] = _dot(c_act, w_ref[...].astype(_BF16)) + b_ref[...]


def _ada(c, w_ada, b_ada):
    batch, d = c.shape
    n_out = w_ada.shape[1]
    return pl.pallas_call(
        _ada_kernel,
        grid=(n_out // d,),
        in_specs=[
            pl.BlockSpec((batch, d), lambda j: (0, 0)),
            pl.BlockSpec((d, d), lambda j: (0, j)),
            pl.BlockSpec((1, d), lambda j: (0, j)),
        ],
        out_specs=pl.BlockSpec((batch, d), lambda j: (0, j)),
        out_shape=jax.ShapeDtypeStruct((batch, n_out), _F32),
        name="ada",
    )(c, w_ada, b_ada.reshape(1, n_out))


def _mixer_kernel(alpha, steps, batch, d, n_s5_blocks, n_experts, region_rows,
                  x_ref, mod_ref, vecs_ref, w_in_ref, wg_ref, w_rnn_ref, s5a_ref, s5b_ref, s5c_ref,
                  w_glu_ref, w_out_ref, w_r_ref,
                  x1_ref, h2_ref, dest_ref, prob_ref, cnt_ref,
                  xt_s, xc_s, a_s, u_s, bu_s, u5_s, ya_s, ga_s, gb_s, h_state, s5_state, cnt_s):
    m = steps * batch
    halo = (CONV_WIDTH - 1) * batch
    s5w = n_s5_blocks * V7X_LANES
    n_blk = d // V7X_LANES
    bs = S5_BLOCK_STATES
    step = pl.program_id(0)

    @pl.when(step == 0)
    def _():
        xc_s[0:halo, :] = jnp.zeros((halo, d), _F32)
        h_state[...] = jnp.zeros_like(h_state)
        s5_state[...] = jnp.zeros_like(s5_state)
        cnt_s[...] = jnp.zeros_like(cnt_s)
        for ref in (xt_s, u_s, bu_s, u5_s, ya_s, ga_s, gb_s):
            ref[...] = jnp.zeros_like(ref)

    def mod(k):
        return mod_ref[:, k * d:(k + 1) * d]

    slot = lax.rem(step, 2)
    for b in range(batch):
        for j in range(n_blk):
            xt_s[slot, j, pl.ds(b, steps, stride=batch), :] = x_ref[b, :, j * V7X_LANES:(j + 1) * V7X_LANES]
    x = jnp.concatenate([xt_s[slot, j] for j in range(n_blk)], axis=1)
    hb = (x * _rows(1.0 + mod(1), steps) + _rows(mod(0), steps)).astype(_BF16)

    def vec(k, width=d):
        return vecs_ref[k:k + 1, 0:width]

    def in_proj(c0, width, bias_row):
        return _dot(hb, w_in_ref[:, c0:c0 + width]) + vec(bias_row, width)

    c0 = 2 * d
    c1 = c0 + s5w
    branch_a = _dot((ya_s[...] * u_s[...]).astype(_BF16), w_rnn_ref[...])
    y5 = jnp.concatenate(
        [_dot(bu_s[:, 2 * bs * j:2 * bs * (j + 1)].astype(_BF16), s5c_ref[j]) for j in range(n_s5_blocks)],
        axis=1) + vec(_VEC_S5_D, s5w) * u5_s[...]
    xc_s[halo:halo + m, :] = in_proj(0, d, _VEC_B_X)
    glu = _dot(jax.nn.gelu(y5).astype(_BF16), w_glu_ref[...])
    xr = jnp.zeros((m, d), _F32) + vec(_VEC_CONV_B)
    for k in range(CONV_WIDTH):
        xr = xr + vec(_VEC_CONV_W + k) * xc_s[k * batch:k * batch + m, :]
    xc_s[0:halo, :] = xc_s[m:m + halo, :]
    xrb = xr.astype(_BF16)
    gates = [_dot(xrb[:, j * V7X_LANES:(j + 1) * V7X_LANES], wg_ref[j]) for j in range(n_blk)]
    merged = (ga_s[...] * branch_a + gb_s[...] * (glu[:, :d] * _sigmoid(glu[:, d:]))).astype(_BF16)
    u5 = in_proj(c0, s5w, _VEC_B_U5)
    u5_s[...] = u5
    r_gate = _sigmoid(jnp.concatenate([g[:, :V7X_LANES] for g in gates], axis=1) + vec(_VEC_B_RG_A))
    i_gate = _sigmoid(jnp.concatenate([g[:, V7X_LANES:] for g in gates], axis=1) + vec(_VEC_B_RG_X))
    a = jnp.exp(vec(_VEC_LAMC) * r_gate)
    a_s[...] = a
    z = 1.0 - a * a
    u_s[...] = jnp.where(z > 0.0, z * lax.rsqrt(z), 0.0) * (i_gate * xr)
    mix = _dot(merged, w_out_ref[...])
    u5b = u5.astype(_BF16)
    for j in range(n_s5_blocks):
        bu_s[:, 2 * bs * j:2 * bs * (j + 1)] = _dot(u5b[:, j * V7X_LANES:(j + 1) * V7X_LANES], s5b_ref[j])
    hc = h_state[...]
    s5c = [(s5_state[:, 2 * bs * j:2 * bs * j + bs], s5_state[:, 2 * bs * j + bs:2 * bs * (j + 1)])
           for j in range(n_s5_blocks)]
    for t in range(steps):
        r0 = t * batch
        hc = a_s[r0:r0 + batch, :] * hc + u_s[r0:r0 + batch, :]
        u_s[r0:r0 + batch, :] = hc
        for j in range(n_s5_blocks):
            re0, im0 = 2 * bs * j, 2 * bs * j + bs
            re, im = s5c[j]
            ar, ai = s5a_ref[0, j], s5a_ref[1, j]
            nre = ar * re - ai * im + bu_s[r0:r0 + batch, re0:re0 + bs]
            nim = ar * im + ai * re + bu_s[r0:r0 + batch, im0:im0 + bs]
            bu_s[r0:r0 + batch, re0:re0 + bs] = nre
            bu_s[r0:r0 + batch, im0:im0 + bs] = nim
            s5c[j] = (nre, nim)
    h_state[...] = hc
    for j in range(n_s5_blocks):
        s5_state[:, 2 * bs * j:2 * bs * j + bs] = s5c[j][0]
        s5_state[:, 2 * bs * j + bs:2 * bs * (j + 1)] = s5c[j][1]
    x_prev = jnp.concatenate([xt_s[1 - slot, j] for j in range(n_blk)], axis=1)
    x1 = _layer_norm(alpha * x_prev + _rows(1.0 + mod(2), steps) * mix, vec(_VEC_LN_G), vec(_VEC_LN_B))
    x1_ref[...] = _pack_rows(x1)
    h2 = x1 * _rows(1.0 + mod(4), steps) + _rows(mod(3), steps)
    h2b = h2.astype(_BF16)
    h2_ref[...] = _pack_rows(h2)
    ya_s[...] = jax.nn.gelu(in_proj(d, d, _VEC_B_Y))
    lane = lax.broadcasted_iota(jnp.int32, (m, V7X_LANES), 1)
    lane_f = lane.astype(_F32)
    neg_inf = jnp.float32(-jnp.inf)
    logits = jnp.where(lane < n_experts, _dot(h2b, w_r_ref[...]) + vec(_VEC_B_ROUTER, V7X_LANES), neg_inf)
    ga_s[...] = _sigmoid(in_proj(c1, d, _VEC_B_GA))
    gb_s[...] = _sigmoid(in_proj(c1 + d, d, _VEC_B_GB))
    onehot = jnp.zeros((m, V7X_LANES), _F32)
    picks, vals = [], []
    for _ in range(TOP_K):
        v = jnp.max(logits, axis=-1, keepdims=True)
        p = jnp.min(jnp.where(logits == v, lane_f, float(V7X_LANES)), axis=-1, keepdims=True)
        hit = lane_f == p
        onehot = jnp.where(hit, 1.0, onehot)
        logits = jnp.where(hit, neg_inf, logits)
        picks.append(p)
        vals.append(v)
    exps = [jnp.exp(v - vals[0]) for v in vals]
    inv_den = 1.0 / functools.reduce(lambda s, e: s + e, exps)
    row = lax.broadcasted_iota(jnp.int32, (m, m), 0)
    col = lax.broadcasted_iota(jnp.int32, (m, m), 1)
    earlier = jnp.where(col < row, 1.0, 0.0).astype(_BF16)
    before = _dot(earlier, onehot.astype(_BF16)) + cnt_s[0:1, :]
    prob_out = jnp.zeros((m, V7X_LANES), _F32)
    dest_out = jnp.zeros((m, V7X_LANES), _F32)
    for k in range(TOP_K):
        rank_k = jnp.sum(jnp.where(lane_f == picks[k], before, 0.0), axis=-1, keepdims=True)
        prob_out = jnp.where(lane == k, exps[k] * inv_den, prob_out)
        dest_out = jnp.where(lane == k, picks[k] * float(region_rows) + rank_k, dest_out)
    prob_ref[...] = prob_out
    dest_t = dest_out.T[0:V7X_SUBLANES, :].astype(jnp.int32)
    for j in range(m // V7X_LANES):
        dest_ref[j] = dest_t[:, j * V7X_LANES:(j + 1) * V7X_LANES]
    has_prev = jnp.where(step > 0, 1.0, 0.0)
    cnt_new = cnt_s[...] + has_prev * jnp.sum(onehot, axis=0, keepdims=True)
    cnt_s[...] = cnt_new
    cnt_ref[...] = cnt_new


def _mixer(x, mod, p, *, alpha, n_experts):
    batch, seq, d = x.shape
    tokens = batch * seq
    steps = MIXER_STEPS
    m = steps * batch
    n_chunks = seq // steps
    n_s5_blocks = p["s5b"].shape[0]
    s5_lanes = n_s5_blocks * 2 * S5_BLOCK_STATES
    halo = (CONV_WIDTH - 1) * batch

    def const(a):
        nd = a.ndim
        return pl.BlockSpec(a.shape, lambda i, nd=nd: (0,) * nd, pipeline_mode=pl.Buffered(1))

    weights = [p["vecs"], p["w_in"], p["wg"], p["w_rnn"], p["s5a"], p["s5b"], p["s5c"], p["w_glu"], p["w_out"],
               p["w_r"]]
    prev = lambda i: jnp.maximum(i - 1, 0)
    row_spec = lambda width: pl.BlockSpec((m, width), lambda i: (prev(i), 0))
    chunks = m // V7X_LANES
    out_shape = (
        jax.ShapeDtypeStruct((tokens, d // 2), jnp.int32),
        jax.ShapeDtypeStruct((tokens, d // 2), jnp.int32),
        jax.ShapeDtypeStruct((tokens // V7X_LANES, V7X_SUBLANES, V7X_LANES), jnp.int32),
        jax.ShapeDtypeStruct((tokens, V7X_LANES), _F32),
        jax.ShapeDtypeStruct((V7X_SUBLANES, V7X_LANES), _F32),
    )
    act = pltpu.VMEM((m, d), _F32)
    scratch = [
        pltpu.VMEM((2, d // V7X_LANES, m, V7X_LANES), _F32),
        pltpu.VMEM((m + halo, d), _F32),
        act, act,
        pltpu.VMEM((m, s5_lanes), _F32),
        pltpu.VMEM((m, n_s5_blocks * V7X_LANES), _F32),
        act, act, act,
        pltpu.VMEM((batch, d), _F32),
        pltpu.VMEM((batch, s5_lanes), _F32),
        pltpu.VMEM((V7X_SUBLANES, V7X_LANES), _F32),
    ]
    weight_bytes = sum(w.size * w.dtype.itemsize for w in weights)
    act_bytes = m * d * 4
    vmem = weight_bytes + 32 * act_bytes
    kern = functools.partial(_mixer_kernel, alpha, steps, batch, d, n_s5_blocks, n_experts, tokens)
    return pl.pallas_call(
        kern,
        grid=(n_chunks + 1,),
        in_specs=[pl.BlockSpec((batch, steps, d), lambda i: (0, jnp.minimum(i, n_chunks - 1), 0)),
                  const(mod)] + [const(w) for w in weights],
        out_specs=(row_spec(d // 2), row_spec(d // 2),
                   pl.BlockSpec((chunks, V7X_SUBLANES, V7X_LANES), lambda i: (prev(i), 0, 0)),
                   row_spec(V7X_LANES),
                   pl.BlockSpec((V7X_SUBLANES, V7X_LANES), lambda i: (0, 0))),
        out_shape=out_shape,
        scratch_shapes=scratch,
        compiler_params=pltpu.CompilerParams(dimension_semantics=("arbitrary",),
                                             vmem_limit_bytes=_vmem_limit(vmem)),
        name="mixer",
    )(x, mod, *weights)


def _expert_kernel(d_ff, tile_e_ref, tile_blk_ref, tile_rows_ref, tile_first_ref, tile_slot_ref, tile_next_ref,
                   x_ref, wgu_hbm, bgu_ref, wd_hbm, bd_ref, y_ref, wgu_f, wd_f, wgu_s, wd_s, sems):
    i = pl.program_id(0)
    d = wgu_f.shape[1]
    r = x_ref.shape[0]

    def weight_copies(expert, slot):
        return (pltpu.make_async_copy(wgu_hbm.at[expert], wgu_f.at[slot], sems.at[slot, 0]),
                pltpu.make_async_copy(wd_hbm.at[expert], wd_f.at[slot], sems.at[slot, 1]))

    @pl.when(tile_first_ref[i] == 1)
    def _():
        slot = tile_slot_ref[i]

        @pl.when(i == 0)
        def _():
            for cp in weight_copies(tile_e_ref[i], slot):
                cp.start()

        for cp in weight_copies(tile_e_ref[i], slot):
            cp.wait()

        @pl.when(tile_next_ref[i] >= 0)
        def _():
            for cp in weight_copies(tile_next_ref[i], 1 - slot):
                cp.start()

        def cast(c, carry):
            r0 = pl.multiple_of(c * WEIGHT_CAST_ROWS, WEIGHT_CAST_ROWS)
            wgu_s[pl.ds(r0, WEIGHT_CAST_ROWS), :] = wgu_f[slot, pl.ds(r0, WEIGHT_CAST_ROWS), :].astype(_BF16)
            wd_s[pl.ds(r0, WEIGHT_CAST_ROWS), :] = wd_f[slot, pl.ds(r0, WEIGHT_CAST_ROWS), :].astype(_BF16)
            return carry

        lax.fori_loop(0, d // WEIGHT_CAST_ROWS, cast, 0)

    def mlp(rows):
        lo, hi = _unpack_rows(x_ref[0:rows, :])
        x = jnp.concatenate([lo, hi], axis=1).astype(_BF16)
        gu = _dot(x, wgu_s[...]) + bgu_ref[0]
        gate = jnp.minimum(gu[:, :d_ff], SWIGLU_LIMIT)
        up = jnp.clip(gu[:, d_ff:], -SWIGLU_LIMIT, SWIGLU_LIMIT)
        act = gate * _sigmoid(SWIGLU_ALPHA * gate) * (up + 1.0)
        y_ref[0:rows, :] = _pack_rows(_dot(act.astype(_BF16), wd_s[...]) + bd_ref[0])
        if rows < r:
            y_ref[rows:r, :] = jnp.zeros((r - rows, y_ref.shape[1]), y_ref.dtype)

    @pl.when(tile_rows_ref[i] == r)
    def _():
        mlp(r)

    @pl.when(tile_rows_ref[i] == r // 2)
    def _():
        mlp(r // 2)

    @pl.when(tile_rows_ref[i] == 0)
    def _():
        y_ref[...] = jnp.zeros_like(y_ref)


def _experts(xb, schedule, w_gu, b_gu, w_down, b_down):
    n_rows, half = xb.shape
    n_experts, d, two_ff = w_gu.shape
    d_ff = two_ff // 2
    assert d_ff == d, "the weight cast loop walks w_gu and w_down rows together"
    r = EXPERT_ROWS
    vmem = 2 * (d * two_ff + d_ff * d) * 4 + (d * two_ff + d_ff * d) * 2 + 8 * r * half * 4 + 6 * r * two_ff * 4
    n_sched = len(schedule)
    tile = lambda i, *s: (s[1][i], 0)
    expert = lambda i, *s: (s[0][i], 0, 0)
    grid_spec = pltpu.PrefetchScalarGridSpec(
        num_scalar_prefetch=n_sched,
        grid=(schedule[0].shape[0],),
        in_specs=[
            pl.BlockSpec((r, half), tile),
            pl.BlockSpec(memory_space=pl.ANY),
            pl.BlockSpec((1, 1, two_ff), expert),
            pl.BlockSpec(memory_space=pl.ANY),
            pl.BlockSpec((1, 1, d), expert),
        ],
        out_specs=pl.BlockSpec((r, half), tile),
        scratch_shapes=[pltpu.VMEM((2, d, two_ff), _F32), pltpu.VMEM((2, d_ff, d), _F32),
                        pltpu.VMEM((d, two_ff), _BF16), pltpu.VMEM((d_ff, d), _BF16),
                        pltpu.SemaphoreType.DMA((2, 2))],
    )
    return pl.pallas_call(
        functools.partial(_expert_kernel, d_ff),
        grid_spec=grid_spec,
        out_shape=jax.ShapeDtypeStruct((n_rows, half), jnp.int32),
        compiler_params=pltpu.CompilerParams(dimension_semantics=("arbitrary",),
                                             vmem_limit_bytes=_vmem_limit(vmem)),
        name="experts",
    )(*schedule, xb, w_gu, b_gu.reshape(n_experts, 1, two_ff), w_down, b_down.reshape(n_experts, 1, d))


def _sc_workers():
    info = plsc.get_sparse_core_info()
    return info.num_cores, info.num_subcores


def _dispatch(h2w, dest_c, n_rows):
    tokens, width = h2w.shape
    n_chunks, _, chunk = dest_c.shape
    nc, ns = _sc_workers()
    per_w = n_chunks // (nc * ns)
    assert per_w * nc * ns == n_chunks

    @functools.partial(
        pl.kernel, mesh=plsc.VectorSubcoreMesh(core_axis_name="c", subcore_axis_name="s"),
        out_type=jax.ShapeDtypeStruct((n_rows, width), h2w.dtype),
        scratch_types=[pltpu.VMEM(dest_c.shape[1:], jnp.int32), pltpu.VMEM((chunk, width), h2w.dtype)],
    )
    def scatter_rows(h_hbm, d_hbm, o_hbm, idx_v, rows_v):
        wid = lax.axis_index("s") * nc + lax.axis_index("c")

        @pl.loop(0, per_w)
        def _(j):
            blk = wid * per_w + j
            pltpu.sync_copy(d_hbm.at[blk], idx_v)
            pltpu.sync_copy(h_hbm.at[pl.ds(pl.multiple_of(blk * chunk, chunk), chunk)], rows_v)
            for k in range(TOP_K):
                pltpu.sync_copy(rows_v, o_hbm.at[idx_v.at[k]])

    return scatter_rows(h2w, dest_c)


def _collect(yb, dest_c):
    _, width = yb.shape
    n_chunks, _, chunk = dest_c.shape
    nc, ns = _sc_workers()
    per_w = n_chunks // (nc * ns)
    assert per_w * nc * ns == n_chunks

    @functools.partial(
        pl.kernel, mesh=plsc.VectorSubcoreMesh(core_axis_name="c", subcore_axis_name="s"),
        out_type=jax.ShapeDtypeStruct((TOP_K, n_chunks * chunk, width), yb.dtype),
        scratch_types=[pltpu.VMEM(dest_c.shape[1:], jnp.int32), pltpu.VMEM((chunk, width), yb.dtype)],
    )
    def gather_rows(y_hbm, d_hbm, o_hbm, idx_v, rows_v):
        wid = lax.axis_index("s") * nc + lax.axis_index("c")

        @pl.loop(0, per_w)
        def _(j):
            blk = wid * per_w + j
            pltpu.sync_copy(d_hbm.at[blk], idx_v)
            for k in range(TOP_K):
                pltpu.sync_copy(y_hbm.at[idx_v.at[k]], rows_v)
                pltpu.sync_copy(rows_v, o_hbm.at[k, pl.ds(pl.multiple_of(blk * chunk, chunk), chunk)])

    return gather_rows(yb, dest_c)


def _combine_kernel(alpha, steps, x1_ref, yg_ref, prob_ref, mod_ref, ln_g_ref, ln_b_ref, o_ref, ot_s):
    batch, _, d = o_ref.shape
    ffn_lo = jnp.zeros((x1_ref.shape[0], d // 2), _F32)
    ffn_hi = jnp.zeros((x1_ref.shape[0], d // 2), _F32)
    for k in range(TOP_K):
        lo, hi = _unpack_rows(yg_ref[k])
        ffn_lo = ffn_lo + prob_ref[:, k:k + 1] * lo
        ffn_hi = ffn_hi + prob_ref[:, k:k + 1] * hi
    ffn = jnp.concatenate([ffn_lo, ffn_hi], axis=1)
    gate = _rows(1.0 + mod_ref[:, 5 * d:6 * d], steps)
    x1 = jnp.concatenate(_unpack_rows(x1_ref[...]), axis=1)
    out = _layer_norm(alpha * x1 + gate * ffn, ln_g_ref[...], ln_b_ref[...])
    n_blk = d // V7X_LANES
    for j in range(n_blk):
        ot_s[j] = out[:, j * V7X_LANES:(j + 1) * V7X_LANES]
    for b in range(batch):
        for j in range(n_blk):
            o_ref[b, :, j * V7X_LANES:(j + 1) * V7X_LANES] = ot_s[j, pl.ds(b, steps, stride=batch), :]


def _combine(x1, yg, prob, mod, ln_g, ln_b, *, alpha, batch):
    tokens, half = x1.shape
    d = 2 * half
    rows = COMBINE_ROWS
    steps = rows // batch
    const = lambda a: pl.BlockSpec(a.shape, lambda i: (0, 0))
    return pl.pallas_call(
        functools.partial(_combine_kernel, alpha, steps),
        grid=(tokens // rows,),
        in_specs=[
            pl.BlockSpec((rows, half), lambda i: (i, 0)),
            pl.BlockSpec((TOP_K, rows, d // 2), lambda i: (0, i, 0)),
            pl.BlockSpec((rows, V7X_LANES), lambda i: (i, 0)),
            const(mod), const(ln_g), const(ln_b),
        ],
        out_specs=pl.BlockSpec((batch, steps, d), lambda i: (0, i, 0)),
        out_shape=jax.ShapeDtypeStruct((batch, tokens // batch, d), _F32),
        scratch_shapes=[pltpu.VMEM((d // V7X_LANES, rows, V7X_LANES), _F32)],
        compiler_params=pltpu.CompilerParams(dimension_semantics=("parallel",)),
        name="combine",
    )(x1, yg, prob, mod, ln_g, ln_b)


def _block_diag(blocks):
    nb, n, a, b = blocks.shape
    eye = jnp.eye(n, dtype=blocks.dtype)
    return (eye[None, :, None, :, None] * blocks[:, :, :, None, :]).reshape(nb, n * a, n * b)


def _s5_params(lam_re, lam_im, log_dt, b_re, b_im, c_re, c_im):
    groups = lam_re.shape[0]
    nb = groups // S5_BLOCK_GROUPS
    dt = jnp.exp(log_dt)[:, None]
    mag = jnp.exp(lam_re * dt)
    ab_re, ab_im = mag * jnp.cos(lam_im * dt), mag * jnp.sin(lam_im * dt)
    den = lam_re * lam_re + lam_im * lam_im
    q_re = ((ab_re - 1.0) * lam_re + ab_im * lam_im) / den
    q_im = (ab_im * lam_re - (ab_re - 1.0) * lam_im) / den
    bb_re = q_re[..., None] * b_re - q_im[..., None] * b_im
    bb_im = q_re[..., None] * b_im + q_im[..., None] * b_re

    def per_block(a):
        return jnp.swapaxes(a.reshape(nb, S5_BLOCK_GROUPS, *a.shape[1:]), 2, 3)

    bmat = jnp.concatenate([_block_diag(per_block(bb_re)), _block_diag(per_block(bb_im))], axis=2)
    cmat = jnp.concatenate([_block_diag(per_block(c_re)), -_block_diag(per_block(c_im))], axis=1)
    s5a = jnp.broadcast_to(jnp.stack([ab_re, ab_im]).reshape(2, nb, 1, S5_BLOCK_STATES),
                           (2, nb, V7X_SUBLANES, S5_BLOCK_STATES))
    return dict(s5a=s5a, s5b=bmat.astype(_BF16), s5c=cmat.astype(_BF16))


def _mixer_vectors(d, b_in, conv_w, conv_b, b_rg_a, b_rg_x, lru_lambda, ln_g, ln_b, s5_d, b_router):
    pad = lambda v: jnp.pad(v, (0, d - v.shape[0]))
    s5w = s5_d.size
    rows = [b_in[0:d], b_in[d:2 * d], pad(b_in[2 * d:2 * d + s5w]), b_in[2 * d + s5w:3 * d + s5w],
            b_in[3 * d + s5w:4 * d + s5w], *conv_w, conv_b, b_rg_a, b_rg_x,
            -LRU_C * jax.nn.softplus(-lru_lambda), ln_g, ln_b, pad(s5_d.reshape(-1)), pad(b_router)]
    rows += [jnp.zeros((d,), _F32)] * (-len(rows) % V7X_SUBLANES)
    return jnp.stack(rows)


def kernel(x, c, w_ada, b_ada, w_in, b_in, conv_w, conv_b, w_rg_a, b_rg_a, w_rg_x, b_rg_x, lru_lambda, w_rnn_out, s5_lambda_re, s5_lambda_im, s5_log_dt, s5_b_re, s5_b_im, s5_c_re, s5_c_im, s5_d, w_glu, w_out, ln1_g, ln1_b, w_router, b_router, w_gu, b_gu, w_down, b_down, ln2_g, ln2_b):
    batch, seq, d = x.shape
    depth = w_ada.shape[0]
    n_experts = w_router.shape[-1]
    tokens = batch * seq
    alpha = (2.0 * depth) ** 0.25
    assert batch == V7X_SUBLANES and d % V7X_LANES == 0 and n_experts <= V7X_LANES
    assert seq % MIXER_STEPS == 0 and tokens % COMBINE_ROWS == 0 and tokens % EXPERT_ROWS == 0
    assert (MIXER_STEPS * batch) % V7X_LANES == 0

    for l in range(depth):
        mod = _ada(c, w_ada[l], b_ada[l])
        p = dict(
            vecs=_mixer_vectors(d, b_in[l], conv_w[l], conv_b[l], b_rg_a[l], b_rg_x[l], lru_lambda[l],
                                ln1_g[l], ln1_b[l], s5_d[l], b_router[l]),
            w_in=w_in[l].astype(_BF16),
            wg=jnp.concatenate([w_rg_a[l], w_rg_x[l]], axis=-1).astype(_BF16),
            w_rnn=w_rnn_out[l].astype(_BF16), w_glu=w_glu[l].astype(_BF16), w_out=w_out[l].astype(_BF16),
            w_r=jnp.pad(w_router[l], ((0, 0), (0, V7X_LANES - n_experts))).astype(_BF16),
            **_s5_params(s5_lambda_re[l], s5_lambda_im[l], s5_log_dt[l], s5_b_re[l], s5_b_im[l],
                         s5_c_re[l], s5_c_im[l]),
        )
        x1, h2w, dest_c, prob, cnt = _mixer(x, mod, p, alpha=alpha, n_experts=n_experts)

        r = EXPERT_ROWS
        blocks_per_region = tokens // r
        spare_blk = n_experts * blocks_per_region
        n_tiles = -(-(tokens * TOP_K + n_experts * (r - 1)) // r)
        counts = cnt[0, :n_experts].astype(jnp.int32)
        tiles_e = (counts + r - 1) // r
        tile_end = jnp.cumsum(tiles_e)
        n_used = tile_end[-1:]
        t_ids = jnp.arange(n_tiles, dtype=jnp.int32)
        done = (tile_end[None, :] <= t_ids[:, None]).astype(jnp.int32)
        tile_e = jnp.minimum(jnp.sum(done, axis=1), n_experts - 1)
        first_tile = jnp.sum(done * tiles_e[None, :], axis=1)
        used = t_ids < n_used
        tile_blk = jnp.where(used, tile_e * blocks_per_region + t_ids - first_tile, spare_blk)
        e_ids = jnp.arange(n_experts, dtype=jnp.int32)[None, :]
        own = (e_ids == tile_e[:, None]).astype(jnp.int32)
        valid = jnp.sum(own * counts[None, :], axis=1) - (t_ids - first_tile) * r
        tile_rows = jnp.where(used, jnp.where(valid <= r // 2, r // 2, r), 0)
        has_tiles = (tiles_e > 0).astype(jnp.int32)[None, :]
        tile_first = (used & (t_ids == first_tile)).astype(jnp.int32)
        tile_slot = jnp.sum(done * has_tiles, axis=1) % 2
        later = jnp.where((has_tiles > 0) & (e_ids > tile_e[:, None]), e_ids, n_experts)
        tile_next = jnp.min(later, axis=1)
        tile_next = jnp.where(tile_next < n_experts, tile_next, -1)

        xb = _dispatch(h2w, dest_c, (spare_blk + 1) * r)
        yb = _experts(xb, (tile_e, tile_blk, tile_rows, tile_first, tile_slot, tile_next),
                      w_gu[l], b_gu[l], w_down[l], b_down[l])
        yg = _collect(yb, dest_c)
        x = _combine(x1, yg, prob, mod, ln2_g[l].reshape(1, -1), ln2_b[l].reshape(1, -1),
                     alpha=alpha, batch=batch)
    return x
```

```python
import functools

import jax
import jax.numpy as jnp
from jax import lax
from jax.experimental import pallas as pl
from jax.experimental.pallas import tpu as pltpu
from jax.experimental.pallas import tpu_sc as plsc

V7X_SUBLANES = 8
V7X_LANES = 128
V7X_VMEM_BYTES = 64 * 1024 * 1024

CONV_WIDTH = 4
LRU_C = 8.0
S5_GROUP = 16
S5_STATE = 64
TOP_K = 4
SWIGLU_LIMIT = 7.0
SWIGLU_ALPHA = 1.702
LN_EPS = 1e-5

S5_BLOCK_GROUPS = V7X_LANES // S5_GROUP
S5_BLOCK_STATES = S5_BLOCK_GROUPS * S5_STATE

(_VEC_B_X, _VEC_B_Y, _VEC_B_U5, _VEC_B_GA, _VEC_B_GB, _VEC_CONV_W) = range(6)
(_VEC_CONV_B, _VEC_B_RG_A, _VEC_B_RG_X, _VEC_LAMC, _VEC_LN_G, _VEC_LN_B, _VEC_S5_D, _VEC_B_ROUTER) = range(
    _VEC_CONV_W + CONV_WIDTH, _VEC_CONV_W + CONV_WIDTH + 8)

MIXER_STEPS = 64
EXPERT_ROWS = 512
EXPERT_ROW_PATHS = 4
COMBINE_ROWS = 1024
WEIGHT_CAST_ROWS = 64

_BF16 = jnp.bfloat16
_F32 = jnp.float32


def _dot(a, b):
    return jnp.dot(a, b, preferred_element_type=_F32)


def _sigmoid(v):
    return 0.5 * jnp.tanh(0.5 * v) + 0.5


def _vmem_limit(nbytes):
    return int(min(nbytes, V7X_VMEM_BYTES - 4 * 1024 * 1024))


def _layer_norm(z, gain, bias):
    mu = jnp.mean(z, axis=-1, keepdims=True)
    zc = z - mu
    var = jnp.mean(zc * zc, axis=-1, keepdims=True)
    return zc * lax.rsqrt(var + LN_EPS) * gain + bias


def _rows(v, steps):
    return jnp.tile(v, (steps, 1))


_HI_MASK = 0xFFFF0000


def _pack_rows(v):
    half = v.shape[1] // 2
    bits = lax.bitcast_convert_type(v.astype(_BF16).astype(_F32), jnp.uint32)
    packed = (bits[:, :half] >> 16) | (bits[:, half:] & jnp.uint32(_HI_MASK))
    return lax.bitcast_convert_type(packed, jnp.int32)


def _unpack_rows(w):
    bits = lax.bitcast_convert_type(w, jnp.uint32)
    lo = lax.bitcast_convert_type(bits << 16, _F32)
    hi = lax.bitcast_convert_type(bits & jnp.uint32(_HI_MASK), _F32)
    return lo, hi


def _ada_kernel(c_ref, w_ref, b_ref, o_ref):
    c = c_ref[...]
    c_act = (c * _sigmoid(c)).astype(_BF16)
    o_ref[...] = _dot(c_act, w_ref[...].astype(_BF16)) + b_ref[...]


def _ada(c, w_ada, b_ada):
    batch, d = c.shape
    n_out = w_ada.shape[1]
    return pl.pallas_call(
        _ada_kernel,
        grid=(n_out // d,),
        in_specs=[
            pl.BlockSpec((batch, d), lambda j: (0, 0)),
            pl.BlockSpec((d, d), lambda j: (0, j)),
            pl.BlockSpec((1, d), lambda j: (0, j)),
        ],
        out_specs=pl.BlockSpec((batch, d), lambda j: (0, j)),
        out_shape=jax.ShapeDtypeStruct((batch, n_out), _F32),
        name="ada",
    )(c, w_ada, b_ada.reshape(1, n_out))


def _mixer_kernel(alpha, steps, batch, d, n_s5_blocks, n_experts, region_rows,
                  x_ref, mod_ref, vecs_ref, w_in_ref, wg_ref, w_rnn_ref, s5a_ref, s5b_ref, s5c_ref,
                  w_glu_ref, w_out_ref, w_r_ref,
                  x1_ref, h2_ref, dest_ref, prob_ref, cnt_ref,
                  xt_s, xc_s, a_s, u_s, bu_s, u5_s, ya_s, ga_s, gb_s, h_state, s5_state, cnt_s):
    m = steps * batch
    halo = (CONV_WIDTH - 1) * batch
    s5w = n_s5_blocks * V7X_LANES
    n_blk = d // V7X_LANES
    bs = S5_BLOCK_STATES
    step = pl.program_id(0)

    @pl.when(step == 0)
    def _():
        xc_s[0:halo, :] = jnp.zeros((halo, d), _F32)
        h_state[...] = jnp.zeros_like(h_state)
        s5_state[...] = jnp.zeros_like(s5_state)
        cnt_s[...] = jnp.zeros_like(cnt_s)
        for ref in (xt_s, u_s, bu_s, u5_s, ya_s, ga_s, gb_s):
            ref[...] = jnp.zeros_like(ref)

    def mod(k):
        return mod_ref[:, k * d:(k + 1) * d]

    slot = lax.rem(step, 2)
    for b in range(batch):
        for j in range(n_blk):
            xt_s[slot, j, pl.ds(b, steps, stride=batch), :] = x_ref[b, :, j * V7X_LANES:(j + 1) * V7X_LANES]
    x = jnp.concatenate([xt_s[slot, j] for j in range(n_blk)], axis=1)
    hb = (x * _rows(1.0 + mod(1), steps) + _rows(mod(0), steps)).astype(_BF16)

    def vec(k, width=d):
        return vecs_ref[k:k + 1, 0:width]

    def in_proj(c0, width, bias_row):
        return _dot(hb, w_in_ref[:, c0:c0 + width]) + vec(bias_row, width)

    c0 = 2 * d
    c1 = c0 + s5w
    branch_a = _dot((ya_s[...] * u_s[...]).astype(_BF16), w_rnn_ref[...])
    y5 = jnp.concatenate(
        [_dot(bu_s[:, 2 * bs * j:2 * bs * (j + 1)].astype(_BF16), s5c_ref[j]) for j in range(n_s5_blocks)],
        axis=1) + vec(_VEC_S5_D, s5w) * u5_s[...]
    xc_s[halo:halo + m, :] = in_proj(0, d, _VEC_B_X)
    glu = _dot(jax.nn.gelu(y5).astype(_BF16), w_glu_ref[...])
    xr = jnp.zeros((m, d), _F32) + vec(_VEC_CONV_B)
    for k in range(CONV_WIDTH):
        xr = xr + vec(_VEC_CONV_W + k) * xc_s[k * batch:k * batch + m, :]
    xc_s[0:halo, :] = xc_s[m:m + halo, :]
    xrb = xr.astype(_BF16)
    gates = [_dot(xrb[:, j * V7X_LANES:(j + 1) * V7X_LANES], wg_ref[j]) for j in range(n_blk)]
    merged = (ga_s[...] * branch_a + gb_s[...] * (glu[:, :d] * _sigmoid(glu[:, d:]))).astype(_BF16)
    u5 = in_proj(c0, s5w, _VEC_B_U5)
    u5_s[...] = u5
    r_gate = _sigmoid(jnp.concatenate([g[:, :V7X_LANES] for g in gates], axis=1) + vec(_VEC_B_RG_A))
    i_gate = _sigmoid(jnp.concatenate([g[:, V7X_LANES:] for g in gates], axis=1) + vec(_VEC_B_RG_X))
    a = jnp.exp(vec(_VEC_LAMC) * r_gate)
    a_s[...] = a
    z = 1.0 - a * a
    u_s[...] = jnp.where(z > 0.0, z * lax.rsqrt(z), 0.0) * (i_gate * xr)
    mix = _dot(merged, w_out_ref[...])
    u5b = u5.astype(_BF16)
    for j in range(n_s5_blocks):
        bu_s[:, 2 * bs * j:2 * bs * (j + 1)] = _dot(u5b[:, j * V7X_LANES:(j + 1) * V7X_LANES], s5b_ref[j])
    hc = h_state[...]
    s5c = [(s5_state[:, 2 * bs * j:2 * bs * j + bs], s5_state[:, 2 * bs * j + bs:2 * bs * (j + 1)])
           for j in range(n_s5_blocks)]
    for t in range(steps):
        r0 = t * batch
        hc = a_s[r0:r0 + batch, :] * hc + u_s[r0:r0 + batch, :]
        u_s[r0:r0 + batch, :] = hc
        for j in range(n_s5_blocks):
            re0, im0 = 2 * bs * j, 2 * bs * j + bs
            re, im = s5c[j]
            ar, ai = s5a_ref[0, j], s5a_ref[1, j]
            nre = ar * re - ai * im + bu_s[r0:r0 + batch, re0:re0 + bs]
            nim = ar * im + ai * re + bu_s[r0:r0 + batch, im0:im0 + bs]
            bu_s[r0:r0 + batch, re0:re0 + bs] = nre
            bu_s[r0:r0 + batch, im0:im0 + bs] = nim
            s5c[j] = (nre, nim)
    h_state[...] = hc
    for j in range(n_s5_blocks):
        s5_state[:, 2 * bs * j:2 * bs * j + bs] = s5c[j][0]
        s5_state[:, 2 * bs * j + bs:2 * bs * (j + 1)] = s5c[j][1]
    x_prev = jnp.concatenate([xt_s[1 - slot, j] for j in range(n_blk)], axis=1)
    x1 = _layer_norm(alpha * x_prev + _rows(1.0 + mod(2), steps) * mix, vec(_VEC_LN_G), vec(_VEC_LN_B))
    x1_ref[...] = x1
    h2 = x1 * _rows(1.0 + mod(4), steps) + _rows(mod(3), steps)
    h2b = h2.astype(_BF16)
    h2_ref[...] = _pack_rows(h2)
    ya_s[...] = jax.nn.gelu(in_proj(d, d, _VEC_B_Y))
    lane = lax.broadcasted_iota(jnp.int32, (m, V7X_LANES), 1)
    lane_f = lane.astype(_F32)
    neg_inf = jnp.float32(-jnp.inf)
    logits = jnp.where(lane < n_experts, _dot(h2b, w_r_ref[...]) + vec(_VEC_B_ROUTER, V7X_LANES), neg_inf)
    ga_s[...] = _sigmoid(in_proj(c1, d, _VEC_B_GA))
    gb_s[...] = _sigmoid(in_proj(c1 + d, d, _VEC_B_GB))
    onehot = jnp.zeros((m, V7X_LANES), _F32)
    picks, vals = [], []
    for _ in range(TOP_K):
        v = jnp.max(logits, axis=-1, keepdims=True)
        p = jnp.min(jnp.where(logits == v, lane_f, float(V7X_LANES)), axis=-1, keepdims=True)
        hit = lane_f == p
        onehot = jnp.where(hit, 1.0, onehot)
        logits = jnp.where(hit, neg_inf, logits)
        picks.append(p)
        vals.append(v)
    exps = [jnp.exp(v - vals[0]) for v in vals]
    inv_den = 1.0 / functools.reduce(lambda s, e: s + e, exps)
    row = lax.broadcasted_iota(jnp.int32, (m, m), 0)
    col = lax.broadcasted_iota(jnp.int32, (m, m), 1)
    earlier = jnp.where(col < row, 1.0, 0.0).astype(_BF16)
    before = _dot(earlier, onehot.astype(_BF16)) + cnt_s[0:1, :]
    prob_out = jnp.zeros((m, V7X_LANES), _F32)
    dest_out = jnp.zeros((m, V7X_LANES), _F32)
    for k in range(TOP_K):
        rank_k = jnp.sum(jnp.where(lane_f == picks[k], before, 0.0), axis=-1, keepdims=True)
        prob_out = jnp.where(lane == k, exps[k] * inv_den, prob_out)
        dest_out = jnp.where(lane == k, picks[k] * float(region_rows) + rank_k, dest_out)
    prob_ref[...] = prob_out
    dest_t = dest_out.T[0:V7X_SUBLANES, :].astype(jnp.int32)
    for j in range(m // V7X_LANES):
        dest_ref[j] = dest_t[:, j * V7X_LANES:(j + 1) * V7X_LANES]
    has_prev = jnp.where(step > 0, 1.0, 0.0)
    cnt_new = cnt_s[...] + has_prev * jnp.sum(onehot, axis=0, keepdims=True)
    cnt_s[...] = cnt_new
    cnt_ref[...] = cnt_new


def _mixer(x, mod, p, *, alpha, n_experts):
    batch, seq, d = x.shape
    tokens = batch * seq
    steps = MIXER_STEPS
    m = steps * batch
    n_chunks = seq // steps
    n_s5_blocks = p["s5b"].shape[0]
    s5_lanes = n_s5_blocks * 2 * S5_BLOCK_STATES
    halo = (CONV_WIDTH - 1) * batch

    def const(a):
        nd = a.ndim
        return pl.BlockSpec(a.shape, lambda i, nd=nd: (0,) * nd, pipeline_mode=pl.Buffered(1))

    weights = [p["vecs"], p["w_in"], p["wg"], p["w_rnn"], p["s5a"], p["s5b"], p["s5c"], p["w_glu"], p["w_out"],
               p["w_r"]]
    prev = lambda i: jnp.maximum(i - 1, 0)
    row_spec = lambda width: pl.BlockSpec((m, width), lambda i: (prev(i), 0))
    chunks = m // V7X_LANES
    out_shape = (
        jax.ShapeDtypeStruct((tokens, d), _F32),
        jax.ShapeDtypeStruct((tokens, d // 2), jnp.int32),
        jax.ShapeDtypeStruct((tokens // V7X_LANES, V7X_SUBLANES, V7X_LANES), jnp.int32),
        jax.ShapeDtypeStruct((tokens, V7X_LANES), _F32),
        jax.ShapeDtypeStruct((V7X_SUBLANES, V7X_LANES), _F32),
    )
    act = pltpu.VMEM((m, d), _F32)
    scratch = [
        pltpu.VMEM((2, d // V7X_LANES, m, V7X_LANES), _F32),
        pltpu.VMEM((m + halo, d), _F32),
        act, act,
        pltpu.VMEM((m, s5_lanes), _F32),
        pltpu.VMEM((m, n_s5_blocks * V7X_LANES), _F32),
        act, act, act,
        pltpu.VMEM((batch, d), _F32),
        pltpu.VMEM((batch, s5_lanes), _F32),
        pltpu.VMEM((V7X_SUBLANES, V7X_LANES), _F32),
    ]
    weight_bytes = sum(w.size * w.dtype.itemsize for w in weights)
    act_bytes = m * d * 4
    vmem = weight_bytes + 32 * act_bytes
    kern = functools.partial(_mixer_kernel, alpha, steps, batch, d, n_s5_blocks, n_experts, tokens)
    return pl.pallas_call(
        kern,
        grid=(n_chunks + 1,),
        in_specs=[pl.BlockSpec((batch, steps, d), lambda i: (0, jnp.minimum(i, n_chunks - 1), 0)),
                  const(mod)] + [const(w) for w in weights],
        out_specs=(row_spec(d), row_spec(d // 2),
                   pl.BlockSpec((chunks, V7X_SUBLANES, V7X_LANES), lambda i: (prev(i), 0, 0)),
                   row_spec(V7X_LANES),
                   pl.BlockSpec((V7X_SUBLANES, V7X_LANES), lambda i: (0, 0))),
        out_shape=out_shape,
        scratch_shapes=scratch,
        compiler_params=pltpu.CompilerParams(dimension_semantics=("arbitrary",),
                                             vmem_limit_bytes=_vmem_limit(vmem)),
        name="mixer",
    )(x, mod, *weights)


def _expert_kernel(d_ff, tile_e_ref, tile_blk_ref, tile_rows_ref, tile_first_ref, tile_slot_ref, tile_next_ref,
                   x_ref, wgu_hbm, bgu_ref, wd_hbm, bd_ref, y_ref, wgu_f, wd_f, wgu_s, wd_s, sems):
    i = pl.program_id(0)
    d = wgu_f.shape[1]
    r = x_ref.shape[0]

    def weight_copies(expert, slot):
        return (pltpu.make_async_copy(wgu_hbm.at[expert], wgu_f.at[slot], sems.at[slot, 0]),
                pltpu.make_async_copy(wd_hbm.at[expert], wd_f.at[slot], sems.at[slot, 1]))

    @pl.when(tile_first_ref[i] == 1)
    def _():
        slot = tile_slot_ref[i]

        @pl.when(i == 0)
        def _():
            for cp in weight_copies(tile_e_ref[i], slot):
                cp.start()

        for cp in weight_copies(tile_e_ref[i], slot):
            cp.wait()

        @pl.when(tile_next_ref[i] >= 0)
        def _():
            for cp in weight_copies(tile_next_ref[i], 1 - slot):
                cp.start()

        def cast(c, carry):
            r0 = pl.multiple_of(c * WEIGHT_CAST_ROWS, WEIGHT_CAST_ROWS)
            wgu_s[pl.ds(r0, WEIGHT_CAST_ROWS), :] = wgu_f[slot, pl.ds(r0, WEIGHT_CAST_ROWS), :].astype(_BF16)
            wd_s[pl.ds(r0, WEIGHT_CAST_ROWS), :] = wd_f[slot, pl.ds(r0, WEIGHT_CAST_ROWS), :].astype(_BF16)
            return carry

        lax.fori_loop(0, d // WEIGHT_CAST_ROWS, cast, 0)

    def mlp(rows):
        lo, hi = _unpack_rows(x_ref[0:rows, :])
        x = jnp.concatenate([lo, hi], axis=1).astype(_BF16)
        gu = _dot(x, wgu_s[...]) + bgu_ref[0]
        gate = jnp.minimum(gu[:, :d_ff], SWIGLU_LIMIT)
        up = jnp.clip(gu[:, d_ff:], -SWIGLU_LIMIT, SWIGLU_LIMIT)
        act = gate * _sigmoid(SWIGLU_ALPHA * gate) * (up + 1.0)
        y_ref[0:rows, :] = _pack_rows(_dot(act.astype(_BF16), wd_s[...]) + bd_ref[0])
        if rows < r:
            y_ref[rows:r, :] = jnp.zeros((r - rows, y_ref.shape[1]), y_ref.dtype)

    for rows in range(r // EXPERT_ROW_PATHS, r + 1, r // EXPERT_ROW_PATHS):
        pl.when(tile_rows_ref[i] == rows)(functools.partial(mlp, rows))

    @pl.when(tile_rows_ref[i] == 0)
    def _():
        y_ref[...] = jnp.zeros_like(y_ref)


def _experts(xb, schedule, w_gu, b_gu, w_down, b_down):
    n_rows, half = xb.shape
    n_experts, d, two_ff = w_gu.shape
    d_ff = two_ff // 2
    assert d_ff == d, "the weight cast loop walks w_gu and w_down rows together"
    r = EXPERT_ROWS
    vmem = 2 * (d * two_ff + d_ff * d) * 4 + (d * two_ff + d_ff * d) * 2 + 8 * r * half * 4 + 6 * r * two_ff * 4
    n_sched = len(schedule)
    tile = lambda i, *s: (s[1][i], 0)
    expert = lambda i, *s: (s[0][i], 0, 0)
    grid_spec = pltpu.PrefetchScalarGridSpec(
        num_scalar_prefetch=n_sched,
        grid=(schedule[0].shape[0],),
        in_specs=[
            pl.BlockSpec((r, half), tile),
            pl.BlockSpec(memory_space=pl.ANY),
            pl.BlockSpec((1, 1, two_ff), expert),
            pl.BlockSpec(memory_space=pl.ANY),
            pl.BlockSpec((1, 1, d), expert),
        ],
        out_specs=pl.BlockSpec((r, half), tile),
        scratch_shapes=[pltpu.VMEM((2, d, two_ff), _F32), pltpu.VMEM((2, d_ff, d), _F32),
                        pltpu.VMEM((d, two_ff), _BF16), pltpu.VMEM((d_ff, d), _BF16),
                        pltpu.SemaphoreType.DMA((2, 2))],
    )
    return pl.pallas_call(
        functools.partial(_expert_kernel, d_ff),
        grid_spec=grid_spec,
        out_shape=jax.ShapeDtypeStruct((n_rows, half), jnp.int32),
        compiler_params=pltpu.CompilerParams(dimension_semantics=("arbitrary",),
                                             vmem_limit_bytes=_vmem_limit(vmem)),
        name="experts",
    )(*schedule, xb, w_gu, b_gu.reshape(n_experts, 1, two_ff), w_down, b_down.reshape(n_experts, 1, d))


def _sc_workers():
    info = plsc.get_sparse_core_info()
    return info.num_cores, info.num_subcores


def _dispatch(h2w, dest_c, n_rows):
    tokens, width = h2w.shape
    n_chunks, _, chunk = dest_c.shape
    nc, ns = _sc_workers()
    per_w = n_chunks // (nc * ns)
    assert per_w * nc * ns == n_chunks

    @functools.partial(
        pl.kernel, mesh=plsc.VectorSubcoreMesh(core_axis_name="c", subcore_axis_name="s"),
        out_type=jax.ShapeDtypeStruct((n_rows, width), h2w.dtype),
        scratch_types=[pltpu.VMEM(dest_c.shape[1:], jnp.int32), pltpu.VMEM((chunk, width), h2w.dtype)],
    )
    def scatter_rows(h_hbm, d_hbm, o_hbm, idx_v, rows_v):
        wid = lax.axis_index("s") * nc + lax.axis_index("c")

        @pl.loop(0, per_w)
        def _(j):
            blk = wid * per_w + j
            pltpu.sync_copy(d_hbm.at[blk], idx_v)
            pltpu.sync_copy(h_hbm.at[pl.ds(pl.multiple_of(blk * chunk, chunk), chunk)], rows_v)
            for k in range(TOP_K):
                pltpu.sync_copy(rows_v, o_hbm.at[idx_v.at[k]])

    return scatter_rows(h2w, dest_c)


def _collect(yb, dest_c):
    _, width = yb.shape
    n_chunks, _, chunk = dest_c.shape
    nc, ns = _sc_workers()
    per_w = n_chunks // (nc * ns)
    assert per_w * nc * ns == n_chunks

    @functools.partial(
        pl.kernel, mesh=plsc.VectorSubcoreMesh(core_axis_name="c", subcore_axis_name="s"),
        out_type=jax.ShapeDtypeStruct((TOP_K, n_chunks * chunk, width), yb.dtype),
        scratch_types=[pltpu.VMEM(dest_c.shape[1:], jnp.int32), pltpu.VMEM((chunk, width), yb.dtype)],
    )
    def gather_rows(y_hbm, d_hbm, o_hbm, idx_v, rows_v):
        wid = lax.axis_index("s") * nc + lax.axis_index("c")

        @pl.loop(0, per_w)
        def _(j):
            blk = wid * per_w + j
            pltpu.sync_copy(d_hbm.at[blk], idx_v)
            for k in range(TOP_K):
                pltpu.sync_copy(y_hbm.at[idx_v.at[k]], rows_v)
                pltpu.sync_copy(rows_v, o_hbm.at[k, pl.ds(pl.multiple_of(blk * chunk, chunk), chunk)])

    return gather_rows(yb, dest_c)


def _combine_kernel(alpha, steps, x1_ref, yg_ref, prob_ref, mod_ref, ln_g_ref, ln_b_ref, o_ref, ot_s):
    d = x1_ref.shape[1]
    batch = o_ref.shape[0]
    ffn_lo = jnp.zeros((x1_ref.shape[0], d // 2), _F32)
    ffn_hi = jnp.zeros((x1_ref.shape[0], d // 2), _F32)
    for k in range(TOP_K):
        lo, hi = _unpack_rows(yg_ref[k])
        ffn_lo = ffn_lo + prob_ref[:, k:k + 1] * lo
        ffn_hi = ffn_hi + prob_ref[:, k:k + 1] * hi
    ffn = jnp.concatenate([ffn_lo, ffn_hi], axis=1)
    gate = _rows(1.0 + mod_ref[:, 5 * d:6 * d], steps)
    out = _layer_norm(alpha * x1_ref[...] + gate * ffn, ln_g_ref[...], ln_b_ref[...])
    n_blk = d // V7X_LANES
    for j in range(n_blk):
        ot_s[j] = out[:, j * V7X_LANES:(j + 1) * V7X_LANES]
    for b in range(batch):
        for j in range(n_blk):
            o_ref[b, :, j * V7X_LANES:(j + 1) * V7X_LANES] = ot_s[j, pl.ds(b, steps, stride=batch), :]


def _combine(x1, yg, prob, mod, ln_g, ln_b, *, alpha, batch):
    tokens, d = x1.shape
    rows = COMBINE_ROWS
    steps = rows // batch
    const = lambda a: pl.BlockSpec(a.shape, lambda i: (0, 0))
    return pl.pallas_call(
        functools.partial(_combine_kernel, alpha, steps),
        grid=(tokens // rows,),
        in_specs=[
            pl.BlockSpec((rows, d), lambda i: (i, 0)),
            pl.BlockSpec((TOP_K, rows, d // 2), lambda i: (0, i, 0)),
            pl.BlockSpec((rows, V7X_LANES), lambda i: (i, 0)),
            const(mod), const(ln_g), const(ln_b),
        ],
        out_specs=pl.BlockSpec((batch, steps, d), lambda i: (0, i, 0)),
        out_shape=jax.ShapeDtypeStruct((batch, tokens // batch, d), _F32),
        scratch_shapes=[pltpu.VMEM((d // V7X_LANES, rows, V7X_LANES), _F32)],
        compiler_params=pltpu.CompilerParams(dimension_semantics=("parallel",)),
        name="combine",
    )(x1, yg, prob, mod, ln_g, ln_b)


def _block_diag(blocks):
    nb, n, a, b = blocks.shape
    eye = jnp.eye(n, dtype=blocks.dtype)
    return (eye[None, :, None, :, None] * blocks[:, :, :, None, :]).reshape(nb, n * a, n * b)


def _s5_params(lam_re, lam_im, log_dt, b_re, b_im, c_re, c_im):
    groups = lam_re.shape[0]
    nb = groups // S5_BLOCK_GROUPS
    dt = jnp.exp(log_dt)[:, None]
    mag = jnp.exp(lam_re * dt)
    ab_re, ab_im = mag * jnp.cos(lam_im * dt), mag * jnp.sin(lam_im * dt)
    den = lam_re * lam_re + lam_im * lam_im
    q_re = ((ab_re - 1.0) * lam_re + ab_im * lam_im) / den
    q_im = (ab_im * lam_re - (ab_re - 1.0) * lam_im) / den
    bb_re = q_re[..., None] * b_re - q_im[..., None] * b_im
    bb_im = q_re[..., None] * b_im + q_im[..., None] * b_re

    def per_block(a):
        return jnp.swapaxes(a.reshape(nb, S5_BLOCK_GROUPS, *a.shape[1:]), 2, 3)

    bmat = jnp.concatenate([_block_diag(per_block(bb_re)), _block_diag(per_block(bb_im))], axis=2)
    cmat = jnp.concatenate([_block_diag(per_block(c_re)), -_block_diag(per_block(c_im))], axis=1)
    s5a = jnp.broadcast_to(jnp.stack([ab_re, ab_im]).reshape(2, nb, 1, S5_BLOCK_STATES),
                           (2, nb, V7X_SUBLANES, S5_BLOCK_STATES))
    return dict(s5a=s5a, s5b=bmat.astype(_BF16), s5c=cmat.astype(_BF16))


def _mixer_vectors(d, b_in, conv_w, conv_b, b_rg_a, b_rg_x, lru_lambda, ln_g, ln_b, s5_d, b_router):
    pad = lambda v: jnp.pad(v, (0, d - v.shape[0]))
    s5w = s5_d.size
    rows = [b_in[0:d], b_in[d:2 * d], pad(b_in[2 * d:2 * d + s5w]), b_in[2 * d + s5w:3 * d + s5w],
            b_in[3 * d + s5w:4 * d + s5w], *conv_w, conv_b, b_rg_a, b_rg_x,
            -LRU_C * jax.nn.softplus(-lru_lambda), ln_g, ln_b, pad(s5_d.reshape(-1)), pad(b_router)]
    rows += [jnp.zeros((d,), _F32)] * (-len(rows) % V7X_SUBLANES)
    return jnp.stack(rows)


def kernel(x, c, w_ada, b_ada, w_in, b_in, conv_w, conv_b, w_rg_a, b_rg_a, w_rg_x, b_rg_x, lru_lambda, w_rnn_out, s5_lambda_re, s5_lambda_im, s5_log_dt, s5_b_re, s5_b_im, s5_c_re, s5_c_im, s5_d, w_glu, w_out, ln1_g, ln1_b, w_router, b_router, w_gu, b_gu, w_down, b_down, ln2_g, ln2_b):
    batch, seq, d = x.shape
    depth = w_ada.shape[0]
    n_experts = w_router.shape[-1]
    tokens = batch * seq
    alpha = (2.0 * depth) ** 0.25
    assert batch == V7X_SUBLANES and d % V7X_LANES == 0 and n_experts <= V7X_LANES
    assert seq % MIXER_STEPS == 0 and tokens % COMBINE_ROWS == 0 and tokens % EXPERT_ROWS == 0
    assert (MIXER_STEPS * batch) % V7X_LANES == 0

    for l in range(depth):
        mod = _ada(c, w_ada[l], b_ada[l])
        p = dict(
            vecs=_mixer_vectors(d, b_in[l], conv_w[l], conv_b[l], b_rg_a[l], b_rg_x[l], lru_lambda[l],
                                ln1_g[l], ln1_b[l], s5_d[l], b_router[l]),
            w_in=w_in[l].astype(_BF16),
            wg=jnp.concatenate([w_rg_a[l], w_rg_x[l]], axis=-1).astype(_BF16),
            w_rnn=w_rnn_out[l].astype(_BF16), w_glu=w_glu[l].astype(_BF16), w_out=w_out[l].astype(_BF16),
            w_r=jnp.pad(w_router[l], ((0, 0), (0, V7X_LANES - n_experts))).astype(_BF16),
            **_s5_params(s5_lambda_re[l], s5_lambda_im[l], s5_log_dt[l], s5_b_re[l], s5_b_im[l],
                         s5_c_re[l], s5_c_im[l]),
        )
        x1, h2w, dest_c, prob, cnt = _mixer(x, mod, p, alpha=alpha, n_experts=n_experts)

        r = EXPERT_ROWS
        blocks_per_region = tokens // r
        spare_blk = n_experts * blocks_per_region
        n_tiles = -(-(tokens * TOP_K + n_experts * (r - 1)) // r)
        counts = cnt[0, :n_experts].astype(jnp.int32)
        tiles_e = (counts + r - 1) // r
        tile_end = jnp.cumsum(tiles_e)
        n_used = tile_end[-1:]
        t_ids = jnp.arange(n_tiles, dtype=jnp.int32)
        done = (tile_end[None, :] <= t_ids[:, None]).astype(jnp.int32)
        tile_e = jnp.minimum(jnp.sum(done, axis=1), n_experts - 1)
        first_tile = jnp.sum(done * tiles_e[None, :], axis=1)
        used = t_ids < n_used
        tile_blk = jnp.where(used, tile_e * blocks_per_region + t_ids - first_tile, spare_blk)
        e_ids = jnp.arange(n_experts, dtype=jnp.int32)[None, :]
        own = (e_ids == tile_e[:, None]).astype(jnp.int32)
        valid = jnp.sum(own * counts[None, :], axis=1) - (t_ids - first_tile) * r
        step_rows = r // EXPERT_ROW_PATHS
        tile_rows = jnp.where(used, jnp.clip((valid + step_rows - 1) // step_rows * step_rows, step_rows, r), 0)
        has_tiles = (tiles_e > 0).astype(jnp.int32)[None, :]
        tile_first = (used & (t_ids == first_tile)).astype(jnp.int32)
        tile_slot = jnp.sum(done * has_tiles, axis=1) % 2
        later = jnp.where((has_tiles > 0) & (e_ids > tile_e[:, None]), e_ids, n_experts)
        tile_next = jnp.min(later, axis=1)
        tile_next = jnp.where(tile_next < n_experts, tile_next, -1)

        xb = _dispatch(h2w, dest_c, (spare_blk + 1) * r)
        yb = _experts(xb, (tile_e, tile_blk, tile_rows, tile_first, tile_slot, tile_next),
                      w_gu[l], b_gu[l], w_down[l], b_down[l])
        yg = _collect(yb, dest_c)
        x = _combine(x1, yg, prob, mod, ln2_g[l].reshape(1, -1), ln2_b[l].reshape(1, -1),
                     alpha=alpha, batch=batch)
    return x
```

```python
import functools

import jax
import jax.numpy as jnp
from jax import lax
from jax.experimental import pallas as pl
from jax.experimental.pallas import tpu as pltpu
from jax.experimental.pallas import tpu_sc as plsc

V7X_SUBLANES = 8
V7X_LANES = 128
V7X_VMEM_BYTES = 64 * 1024 * 1024

CONV_WIDTH = 4
LRU_C = 8.0
S5_GROUP = 16
S5_STATE = 64
TOP_K = 4
SWIGLU_LIMIT = 7.0
SWIGLU_ALPHA = 1.702
LN_EPS = 1e-5

S5_BLOCK_GROUPS = V7X_LANES // S5_GROUP
S5_BLOCK_STATES = S5_BLOCK_GROUPS * S5_STATE

(_VEC_B_X, _VEC_B_Y, _VEC_B_U5, _VEC_B_GA, _VEC_B_GB, _VEC_CONV_W) = range(6)
(_VEC_CONV_B, _VEC_B_RG_A, _VEC_B_RG_X, _VEC_LAMC, _VEC_LN_G, _VEC_LN_B, _VEC_S5_D, _VEC_B_ROUTER) = range(
    _VEC_CONV_W + CONV_WIDTH, _VEC_CONV_W + CONV_WIDTH + 8)

MIXER_STEPS = 64
EXPERT_ROWS = 512
EXPERT_ROW_PATHS = 4
COMBINE_ROWS = 1024
WEIGHT_CAST_ROWS = 64

_BF16 = jnp.bfloat16
_F32 = jnp.float32


def _dot(a, b):
    return jnp.dot(a, b, preferred_element_type=_F32)


def _sigmoid(v):
    return 0.5 * jnp.tanh(0.5 * v) + 0.5


def _vmem_limit(nbytes):
    return int(min(nbytes, V7X_VMEM_BYTES - 4 * 1024 * 1024))


def _layer_norm(z, gain, bias):
    mu = jnp.mean(z, axis=-1, keepdims=True)
    zc = z - mu
    var = jnp.mean(zc * zc, axis=-1, keepdims=True)
    return zc * lax.rsqrt(var + LN_EPS) * gain + bias


def _rows(v, steps):
    return jnp.tile(v, (steps, 1))


_HI_MASK = 0xFFFF0000


def _pack_rows(v):
    half = v.shape[1] // 2
    bits = lax.bitcast_convert_type(v.astype(_BF16).astype(_F32), jnp.uint32)
    packed = (bits[:, :half] >> 16) | (bits[:, half:] & jnp.uint32(_HI_MASK))
    return lax.bitcast_convert_type(packed, jnp.int32)


def _unpack_rows(w):
    bits = lax.bitcast_convert_type(w, jnp.uint32)
    lo = lax.bitcast_convert_type(bits << 16, _F32)
    hi = lax.bitcast_convert_type(bits & jnp.uint32(_HI_MASK), _F32)
    return lo, hi


def _ada_kernel(c_ref, w_ref, b_ref, o_ref):
    c = c_ref[...]
    c_act = (c * _sigmoid(c)).astype(_BF16)
    o_ref[...] = _dot(c_act, w_ref[...].astype(_BF16)) + b_ref[...]


def _ada(c, w_ada, b_ada):
    batch, d = c.shape
    n_out = w_ada.shape[1]
    return pl.pallas_call(
        _ada_kernel,
        grid=(n_out // d,),
        in_specs=[
            pl.BlockSpec((batch, d), lambda j: (0, 0)),
            pl.BlockSpec((d, d), lambda j: (0, j)),
            pl.BlockSpec((1, d), lambda j: (0, j)),
        ],
        out_specs=pl.BlockSpec((batch, d), lambda j: (0, j)),
        out_shape=jax.ShapeDtypeStruct((batch, n_out), _F32),
        name="ada",
    )(c, w_ada, b_ada.reshape(1, n_out))


def _mixer_kernel(alpha, steps, batch, d, n_s5_blocks, n_experts, region_rows,
                  x_ref, mod_ref, vecs_ref, w_in_ref, wg_ref, w_rnn_ref, s5a_ref, s5b_ref, s5c_ref,
                  w_glu_ref, w_out_ref, w_r_ref,
                  x1_ref, h2_ref, dest_ref, prob_ref, cnt_ref,
                  xt_s, xc_s, a_s, u_s, bu_s, u5_s, ya_s, ga_s, gb_s, h_state, s5_state, cnt_s):
    m = steps * batch
    halo = (CONV_WIDTH - 1) * batch
    s5w = n_s5_blocks * V7X_LANES
    n_blk = d // V7X_LANES
    bs = S5_BLOCK_STATES
    step = pl.program_id(0)

    @pl.when(step == 0)
    def _():
        xc_s[0:halo, :] = jnp.zeros((halo, d), _F32)
        h_state[...] = jnp.zeros_like(h_state)
        s5_state[...] = jnp.zeros_like(s5_state)
        cnt_s[...] = jnp.zeros_like(cnt_s)
        for ref in (xt_s, u_s, bu_s, u5_s, ya_s, ga_s, gb_s):
            ref[...] = jnp.zeros_like(ref)

    def mod(k):
        return mod_ref[:, k * d:(k + 1) * d]

    slot = lax.rem(step, 2)
    for b in range(batch):
        for j in range(n_blk):
            xt_s[slot, j, pl.ds(b, steps, stride=batch), :] = x_ref[b, :, j * V7X_LANES:(j + 1) * V7X_LANES]
    x = jnp.concatenate([xt_s[slot, j] for j in range(n_blk)], axis=1)
    hb = (x * _rows(1.0 + mod(1), steps) + _rows(mod(0), steps)).astype(_BF16)

    def vec(k, width=d):
        return vecs_ref[k:k + 1, 0:width]

    def in_proj(c0, width, bias_row):
        return _dot(hb, w_in_ref[:, c0:c0 + width]) + vec(bias_row, width)

    c0 = 2 * d
    c1 = c0 + s5w
    branch_a = _dot((ya_s[...] * u_s[...]).astype(_BF16), w_rnn_ref[...])
    y5 = jnp.concatenate(
        [_dot(bu_s[:, 2 * bs * j:2 * bs * (j + 1)].astype(_BF16), s5c_ref[j]) for j in range(n_s5_blocks)],
        axis=1) + vec(_VEC_S5_D, s5w) * u5_s[...]
    xc_s[halo:halo + m, :] = in_proj(0, d, _VEC_B_X)
    glu = _dot(jax.nn.gelu(y5).astype(_BF16), w_glu_ref[...])
    xr = jnp.zeros((m, d), _F32) + vec(_VEC_CONV_B)
    for k in range(CONV_WIDTH):
        xr = xr + vec(_VEC_CONV_W + k) * xc_s[k * batch:k * batch + m, :]
    xc_s[0:halo, :] = xc_s[m:m + halo, :]
    xrb = xr.astype(_BF16)
    gates = [_dot(xrb[:, j * V7X_LANES:(j + 1) * V7X_LANES], wg_ref[j]) for j in range(n_blk)]
    merged = (ga_s[...] * branch_a + gb_s[...] * (glu[:, :d] * _sigmoid(glu[:, d:]))).astype(_BF16)
    u5 = in_proj(c0, s5w, _VEC_B_U5)
    u5_s[...] = u5
    r_gate = _sigmoid(jnp.concatenate([g[:, :V7X_LANES] for g in gates], axis=1) + vec(_VEC_B_RG_A))
    i_gate = _sigmoid(jnp.concatenate([g[:, V7X_LANES:] for g in gates], axis=1) + vec(_VEC_B_RG_X))
    a = jnp.exp(vec(_VEC_LAMC) * r_gate)
    a_s[...] = a
    z = 1.0 - a * a
    u_s[...] = jnp.where(z > 0.0, z * lax.rsqrt(z), 0.0) * (i_gate * xr)
    mix = _dot(merged, w_out_ref[...])
    u5b = u5.astype(_BF16)
    for j in range(n_s5_blocks):
        bu_s[:, 2 * bs * j:2 * bs * (j + 1)] = _dot(u5b[:, j * V7X_LANES:(j + 1) * V7X_LANES], s5b_ref[j])
    hc = h_state[...]
    s5c = [(s5_state[:, 2 * bs * j:2 * bs * j + bs], s5_state[:, 2 * bs * j + bs:2 * bs * (j + 1)])
           for j in range(n_s5_blocks)]
    for t in range(steps):
        r0 = t * batch
        hc = a_s[r0:r0 + batch, :] * hc + u_s[r0:r0 + batch, :]
        u_s[r0:r0 + batch, :] = hc
        for j in range(n_s5_blocks):
            re0, im0 = 2 * bs * j, 2 * bs * j + bs
            re, im = s5c[j]
            ar, ai = s5a_ref[0, j], s5a_ref[1, j]
            nre = ar * re - ai * im + bu_s[r0:r0 + batch, re0:re0 + bs]
            nim = ar * im + ai * re + bu_s[r0:r0 + batch, im0:im0 + bs]
            bu_s[r0:r0 + batch, re0:re0 + bs] = nre
            bu_s[r0:r0 + batch, im0:im0 + bs] = nim
            s5c[j] = (nre, nim)
    h_state[...] = hc
    for j in range(n_s5_blocks):
        s5_state[:, 2 * bs * j:2 * bs * j + bs] = s5c[j][0]
        s5_state[:, 2 * bs * j + bs:2 * bs * (j + 1)] = s5c[j][1]
    x_prev = jnp.concatenate([xt_s[1 - slot, j] for j in range(n_blk)], axis=1)
    x1 = _layer_norm(alpha * x_prev + _rows(1.0 + mod(2), steps) * mix, vec(_VEC_LN_G), vec(_VEC_LN_B))
    x1_ref[...] = x1
    h2 = x1 * _rows(1.0 + mod(4), steps) + _rows(mod(3), steps)
    h2b = h2.astype(_BF16)
    h2_ref[...] = _pack_rows(h2)
    ya_s[...] = jax.nn.gelu(in_proj(d, d, _VEC_B_Y))
    lane = lax.broadcasted_iota(jnp.int32, (m, V7X_LANES), 1)
    lane_f = lane.astype(_F32)
    neg_inf = jnp.float32(-jnp.inf)
    logits = jnp.where(lane < n_experts, _dot(h2b, w_r_ref[...]) + vec(_VEC_B_ROUTER, V7X_LANES), neg_inf)
    ga_s[...] = _sigmoid(in_proj(c1, d, _VEC_B_GA))
    gb_s[...] = _sigmoid(in_proj(c1 + d, d, _VEC_B_GB))
    onehot = jnp.zeros((m, V7X_LANES), _F32)
    picks, vals = [], []
    for _ in range(TOP_K):
        v = jnp.max(logits, axis=-1, keepdims=True)
        p = jnp.min(jnp.where(logits == v, lane_f, float(V7X_LANES)), axis=-1, keepdims=True)
        hit = lane_f == p
        onehot = jnp.where(hit, 1.0, onehot)
        logits = jnp.where(hit, neg_inf, logits)
        picks.append(p)
        vals.append(v)
    exps = [jnp.exp(v - vals[0]) for v in vals]
    inv_den = 1.0 / functools.reduce(lambda s, e: s + e, exps)
    row = lax.broadcasted_iota(jnp.int32, (m, m), 0)
    col = lax.broadcasted_iota(jnp.int32, (m, m), 1)
    earlier = jnp.where(col < row, 1.0, 0.0).astype(_BF16)
    before = _dot(earlier, onehot.astype(_BF16)) + cnt_s[0:1, :]
    prob_out = jnp.zeros((m, V7X_LANES), _F32)
    dest_out = jnp.zeros((m, V7X_LANES), _F32)
    for k in range(TOP_K):
        rank_k = jnp.sum(jnp.where(lane_f == picks[k], before, 0.0), axis=-1, keepdims=True)
        prob_out = jnp.where(lane == k, exps[k] * inv_den, prob_out)
        dest_out = jnp.where(lane == k, picks[k] * float(region_rows) + rank_k, dest_out)
    prob_ref[...] = prob_out
    dest_t = dest_out.T[0:V7X_SUBLANES, :].astype(jnp.int32)
    for j in range(m // V7X_LANES):
        dest_ref[j] = dest_t[:, j * V7X_LANES:(j + 1) * V7X_LANES]
    has_prev = jnp.where(step > 0, 1.0, 0.0)
    cnt_new = cnt_s[...] + has_prev * jnp.sum(onehot, axis=0, keepdims=True)
    cnt_s[...] = cnt_new
    cnt_ref[...] = cnt_new


def _mixer(x, mod, p, *, alpha, n_experts):
    batch, seq, d = x.shape
    tokens = batch * seq
    steps = MIXER_STEPS
    m = steps * batch
    n_chunks = seq // steps
    n_s5_blocks = p["s5b"].shape[0]
    s5_lanes = n_s5_blocks * 2 * S5_BLOCK_STATES
    halo = (CONV_WIDTH - 1) * batch

    def const(a):
        nd = a.ndim
        return pl.BlockSpec(a.shape, lambda i, nd=nd: (0,) * nd, pipeline_mode=pl.Buffered(1))

    weights = [p["vecs"], p["w_in"], p["wg"], p["w_rnn"], p["s5a"], p["s5b"], p["s5c"], p["w_glu"], p["w_out"],
               p["w_r"]]
    prev = lambda i: jnp.maximum(i - 1, 0)
    row_spec = lambda width: pl.BlockSpec((m, width), lambda i: (prev(i), 0))
    chunks = m // V7X_LANES
    out_shape = (
        jax.ShapeDtypeStruct((tokens, d), _F32),
        jax.ShapeDtypeStruct((tokens, d // 2), jnp.int32),
        jax.ShapeDtypeStruct((tokens // V7X_LANES, V7X_SUBLANES, V7X_LANES), jnp.int32),
        jax.ShapeDtypeStruct((tokens, V7X_LANES), _F32),
        jax.ShapeDtypeStruct((V7X_SUBLANES, V7X_LANES), _F32),
    )
    act = pltpu.VMEM((m, d), _F32)
    scratch = [
        pltpu.VMEM((2, d // V7X_LANES, m, V7X_LANES), _F32),
        pltpu.VMEM((m + halo, d), _F32),
        act, act,
        pltpu.VMEM((m, s5_lanes), _F32),
        pltpu.VMEM((m, n_s5_blocks * V7X_LANES), _F32),
        act, act, act,
        pltpu.VMEM((batch, d), _F32),
        pltpu.VMEM((batch, s5_lanes), _F32),
        pltpu.VMEM((V7X_SUBLANES, V7X_LANES), _F32),
    ]
    weight_bytes = sum(w.size * w.dtype.itemsize for w in weights)
    act_bytes = m * d * 4
    vmem = weight_bytes + 32 * act_bytes
    kern = functools.partial(_mixer_kernel, alpha, steps, batch, d, n_s5_blocks, n_experts, tokens)
    return pl.pallas_call(
        kern,
        grid=(n_chunks + 1,),
        in_specs=[pl.BlockSpec((batch, steps, d), lambda i: (0, jnp.minimum(i, n_chunks - 1), 0)),
                  const(mod)] + [const(w) for w in weights],
        out_specs=(row_spec(d), row_spec(d // 2),
                   pl.BlockSpec((chunks, V7X_SUBLANES, V7X_LANES), lambda i: (prev(i), 0, 0)),
                   row_spec(V7X_LANES),
                   pl.BlockSpec((V7X_SUBLANES, V7X_LANES), lambda i: (0, 0))),
        out_shape=out_shape,
        scratch_shapes=scratch,
        compiler_params=pltpu.CompilerParams(dimension_semantics=("arbitrary",),
                                             vmem_limit_bytes=_vmem_limit(vmem)),
        name="mixer",
    )(x, mod, *weights)


def _expert_kernel(d_ff, tile_e_ref, tile_blk_ref, tile_rows_ref, tile_first_ref, tile_slot_ref, tile_next_ref,
                   x_ref, wgu_hbm, bgu_ref, wd_hbm, bd_ref, y_ref, wgu_f, wd_f, wgu_s, wd_s, sems):
    i = pl.program_id(0)
    d = wgu_f.shape[1]
    r = x_ref.shape[0]

    def weight_copies(expert, slot):
        return (pltpu.make_async_copy(wgu_hbm.at[expert], wgu_f.at[slot], sems.at[slot, 0]),
                pltpu.make_async_copy(wd_hbm.at[expert], wd_f.at[slot], sems.at[slot, 1]))

    @pl.when(tile_first_ref[i] == 1)
    def _():
        slot = tile_slot_ref[i]

        @pl.when(i == 0)
        def _():
            for cp in weight_copies(tile_e_ref[i], slot):
                cp.start()

        for cp in weight_copies(tile_e_ref[i], slot):
            cp.wait()

        @pl.when(tile_next_ref[i] >= 0)
        def _():
            for cp in weight_copies(tile_next_ref[i], 1 - slot):
                cp.start(priority=1)

        def cast(c, carry):
            r0 = pl.multiple_of(c * WEIGHT_CAST_ROWS, WEIGHT_CAST_ROWS)
            wgu_s[pl.ds(r0, WEIGHT_CAST_ROWS), :] = wgu_f[slot, pl.ds(r0, WEIGHT_CAST_ROWS), :].astype(_BF16)
            wd_s[pl.ds(r0, WEIGHT_CAST_ROWS), :] = wd_f[slot, pl.ds(r0, WEIGHT_CAST_ROWS), :].astype(_BF16)
            return carry

        lax.fori_loop(0, d // WEIGHT_CAST_ROWS, cast, 0)

    def mlp(rows):
        lo, hi = _unpack_rows(x_ref[0:rows, :])
        x = jnp.concatenate([lo, hi], axis=1).astype(_BF16)
        gu = _dot(x, wgu_s[...]) + bgu_ref[0]
        gate = jnp.minimum(gu[:, :d_ff], SWIGLU_LIMIT)
        up = jnp.clip(gu[:, d_ff:], -SWIGLU_LIMIT, SWIGLU_LIMIT)
        act = gate * _sigmoid(SWIGLU_ALPHA * gate) * (up + 1.0)
        y_ref[0:rows, :] = _pack_rows(_dot(act.astype(_BF16), wd_s[...]) + bd_ref[0])
        if rows < r:
            y_ref[rows:r, :] = jnp.zeros((r - rows, y_ref.shape[1]), y_ref.dtype)

    for rows in range(r // EXPERT_ROW_PATHS, r + 1, r // EXPERT_ROW_PATHS):
        pl.when(tile_rows_ref[i] == rows)(functools.partial(mlp, rows))

    @pl.when(tile_rows_ref[i] == 0)
    def _():
        y_ref[...] = jnp.zeros_like(y_ref)


def _experts(xb, schedule, w_gu, b_gu, w_down, b_down):
    n_rows, half = xb.shape
    n_experts, d, two_ff = w_gu.shape
    d_ff = two_ff // 2
    assert d_ff == d, "the weight cast loop walks w_gu and w_down rows together"
    r = EXPERT_ROWS
    vmem = 2 * (d * two_ff + d_ff * d) * 4 + (d * two_ff + d_ff * d) * 2 + 8 * r * half * 4 + 6 * r * two_ff * 4
    n_sched = len(schedule)
    tile = lambda i, *s: (s[1][i], 0)
    expert = lambda i, *s: (s[0][i], 0, 0)
    grid_spec = pltpu.PrefetchScalarGridSpec(
        num_scalar_prefetch=n_sched,
        grid=(schedule[0].shape[0],),
        in_specs=[
            pl.BlockSpec((r, half), tile),
            pl.BlockSpec(memory_space=pl.ANY),
            pl.BlockSpec((1, 1, two_ff), expert),
            pl.BlockSpec(memory_space=pl.ANY),
            pl.BlockSpec((1, 1, d), expert),
        ],
        out_specs=pl.BlockSpec((r, half), tile),
        scratch_shapes=[pltpu.VMEM((2, d, two_ff), _F32), pltpu.VMEM((2, d_ff, d), _F32),
                        pltpu.VMEM((d, two_ff), _BF16), pltpu.VMEM((d_ff, d), _BF16),
                        pltpu.SemaphoreType.DMA((2, 2))],
    )
    return pl.pallas_call(
        functools.partial(_expert_kernel, d_ff),
        grid_spec=grid_spec,
        out_shape=jax.ShapeDtypeStruct((n_rows, half), jnp.int32),
        compiler_params=pltpu.CompilerParams(dimension_semantics=("arbitrary",),
                                             vmem_limit_bytes=_vmem_limit(vmem)),
        name="experts",
    )(*schedule, xb, w_gu, b_gu.reshape(n_experts, 1, two_ff), w_down, b_down.reshape(n_experts, 1, d))


def _sc_workers():
    info = plsc.get_sparse_core_info()
    return info.num_cores, info.num_subcores


def _dispatch(h2w, dest_c, n_rows):
    tokens, width = h2w.shape
    n_chunks, _, chunk = dest_c.shape
    nc, ns = _sc_workers()
    per_w = n_chunks // (nc * ns)
    assert per_w * nc * ns == n_chunks

    @functools.partial(
        pl.kernel, mesh=plsc.VectorSubcoreMesh(core_axis_name="c", subcore_axis_name="s"),
        out_type=jax.ShapeDtypeStruct((n_rows, width), h2w.dtype),
        scratch_types=[pltpu.VMEM(dest_c.shape[1:], jnp.int32), pltpu.VMEM((chunk, width), h2w.dtype)],
    )
    def scatter_rows(h_hbm, d_hbm, o_hbm, idx_v, rows_v):
        wid = lax.axis_index("s") * nc + lax.axis_index("c")

        @pl.loop(0, per_w)
        def _(j):
            blk = wid * per_w + j
            pltpu.sync_copy(d_hbm.at[blk], idx_v)
            pltpu.sync_copy(h_hbm.at[pl.ds(pl.multiple_of(blk * chunk, chunk), chunk)], rows_v)
            for k in range(TOP_K):
                pltpu.sync_copy(rows_v, o_hbm.at[idx_v.at[k]])

    return scatter_rows(h2w, dest_c)


def _collect(yb, dest_c):
    _, width = yb.shape
    n_chunks, _, chunk = dest_c.shape
    nc, ns = _sc_workers()
    per_w = n_chunks // (nc * ns)
    assert per_w * nc * ns == n_chunks

    @functools.partial(
        pl.kernel, mesh=plsc.VectorSubcoreMesh(core_axis_name="c", subcore_axis_name="s"),
        out_type=jax.ShapeDtypeStruct((TOP_K, n_chunks * chunk, width), yb.dtype),
        scratch_types=[pltpu.VMEM(dest_c.shape[1:], jnp.int32), pltpu.VMEM((chunk, width), yb.dtype)],
    )
    def gather_rows(y_hbm, d_hbm, o_hbm, idx_v, rows_v):
        wid = lax.axis_index("s") * nc + lax.axis_index("c")

        @pl.loop(0, per_w)
        def _(j):
            blk = wid * per_w + j
            pltpu.sync_copy(d_hbm.at[blk], idx_v)
            for k in range(TOP_K):
                pltpu.sync_copy(y_hbm.at[idx_v.at[k]], rows_v)
                pltpu.sync_copy(rows_v, o_hbm.at[k, pl.ds(pl.multiple_of(blk * chunk, chunk), chunk)])

    return gather_rows(yb, dest_c)


def _combine_kernel(alpha, steps, x1_ref, yg_ref, prob_ref, mod_ref, ln_g_ref, ln_b_ref, o_ref, ot_s):
    d = x1_ref.shape[1]
    batch = o_ref.shape[0]
    ffn_lo = jnp.zeros((x1_ref.shape[0], d // 2), _F32)
    ffn_hi = jnp.zeros((x1_ref.shape[0], d // 2), _F32)
    for k in range(TOP_K):
        lo, hi = _unpack_rows(yg_ref[k])
        ffn_lo = ffn_lo + prob_ref[:, k:k + 1] * lo
        ffn_hi = ffn_hi + prob_ref[:, k:k + 1] * hi
    ffn = jnp.concatenate([ffn_lo, ffn_hi], axis=1)
    gate = _rows(1.0 + mod_ref[:, 5 * d:6 * d], steps)
    out = _layer_norm(alpha * x1_ref[...] + gate * ffn, ln_g_ref[...], ln_b_ref[...])
    n_blk = d // V7X_LANES
    for j in range(n_blk):
        ot_s[j] = out[:, j * V7X_LANES:(j + 1) * V7X_LANES]
    for b in range(batch):
        for j in range(n_blk):
            o_ref[b, :, j * V7X_LANES:(j + 1) * V7X_LANES] = ot_s[j, pl.ds(b, steps, stride=batch), :]


def _combine(x1, yg, prob, mod, ln_g, ln_b, *, alpha, batch):
    tokens, d = x1.shape
    rows = COMBINE_ROWS
    steps = rows // batch
    const = lambda a: pl.BlockSpec(a.shape, lambda i: (0, 0))
    return pl.pallas_call(
        functools.partial(_combine_kernel, alpha, steps),
        grid=(tokens // rows,),
        in_specs=[
            pl.BlockSpec((rows, d), lambda i: (i, 0)),
            pl.BlockSpec((TOP_K, rows, d // 2), lambda i: (0, i, 0)),
            pl.BlockSpec((rows, V7X_LANES), lambda i: (i, 0)),
            const(mod), const(ln_g), const(ln_b),
        ],
        out_specs=pl.BlockSpec((batch, steps, d), lambda i: (0, i, 0)),
        out_shape=jax.ShapeDtypeStruct((batch, tokens // batch, d), _F32),
        scratch_shapes=[pltpu.VMEM((d // V7X_LANES, rows, V7X_LANES), _F32)],
        compiler_params=pltpu.CompilerParams(dimension_semantics=("parallel",)),
        name="combine",
    )(x1, yg, prob, mod, ln_g, ln_b)


def _block_diag(blocks):
    nb, n, a, b = blocks.shape
    eye = jnp.eye(n, dtype=blocks.dtype)
    return (eye[None, :, None, :, None] * blocks[:, :, :, None, :]).reshape(nb, n * a, n * b)


def _s5_params(lam_re, lam_im, log_dt, b_re, b_im, c_re, c_im):
    groups = lam_re.shape[0]
    nb = groups // S5_BLOCK_GROUPS
    dt = jnp.exp(log_dt)[:, None]
    mag = jnp.exp(lam_re * dt)
    ab_re, ab_im = mag * jnp.cos(lam_im * dt), mag * jnp.sin(lam_im * dt)
    den = lam_re * lam_re + lam_im * lam_im
    q_re = ((ab_re - 1.0) * lam_re + ab_im * lam_im) / den
    q_im = (ab_im * lam_re - (ab_re - 1.0) * lam_im) / den
    bb_re = q_re[..., None] * b_re - q_im[..., None] * b_im
    bb_im = q_re[..., None] * b_im + q_im[..., None] * b_re

    def per_block(a):
        return jnp.swapaxes(a.reshape(nb, S5_BLOCK_GROUPS, *a.shape[1:]), 2, 3)

    bmat = jnp.concatenate([_block_diag(per_block(bb_re)), _block_diag(per_block(bb_im))], axis=2)
    cmat = jnp.concatenate([_block_diag(per_block(c_re)), -_block_diag(per_block(c_im))], axis=1)
    s5a = jnp.broadcast_to(jnp.stack([ab_re, ab_im]).reshape(2, nb, 1, S5_BLOCK_STATES),
                           (2, nb, V7X_SUBLANES, S5_BLOCK_STATES))
    return dict(s5a=s5a, s5b=bmat.astype(_BF16), s5c=cmat.astype(_BF16))


def _mixer_vectors(d, b_in, conv_w, conv_b, b_rg_a, b_rg_x, lru_lambda, ln_g, ln_b, s5_d, b_router):
    pad = lambda v: jnp.pad(v, (0, d - v.shape[0]))
    s5w = s5_d.size
    rows = [b_in[0:d], b_in[d:2 * d], pad(b_in[2 * d:2 * d + s5w]), b_in[2 * d + s5w:3 * d + s5w],
            b_in[3 * d + s5w:4 * d + s5w], *conv_w, conv_b, b_rg_a, b_rg_x,
            -LRU_C * jax.nn.softplus(-lru_lambda), ln_g, ln_b, pad(s5_d.reshape(-1)), pad(b_router)]
    rows += [jnp.zeros((d,), _F32)] * (-len(rows) % V7X_SUBLANES)
    return jnp.stack(rows)


def kernel(x, c, w_ada, b_ada, w_in, b_in, conv_w, conv_b, w_rg_a, b_rg_a, w_rg_x, b_rg_x, lru_lambda, w_rnn_out, s5_lambda_re, s5_lambda_im, s5_log_dt, s5_b_re, s5_b_im, s5_c_re, s5_c_im, s5_d, w_glu, w_out, ln1_g, ln1_b, w_router, b_router, w_gu, b_gu, w_down, b_down, ln2_g, ln2_b):
    batch, seq, d = x.shape
    depth = w_ada.shape[0]
    n_experts = w_router.shape[-1]
    tokens = batch * seq
    alpha = (2.0 * depth) ** 0.25
    assert batch == V7X_SUBLANES and d % V7X_LANES == 0 and n_experts <= V7X_LANES
    assert seq % MIXER_STEPS == 0 and tokens % COMBINE_ROWS == 0 and tokens % EXPERT_ROWS == 0
    assert (MIXER_STEPS * batch) % V7X_LANES == 0

    for l in range(depth):
        mod = _ada(c, w_ada[l], b_ada[l])
        p = dict(
            vecs=_mixer_vectors(d, b_in[l], conv_w[l], conv_b[l], b_rg_a[l], b_rg_x[l], lru_lambda[l],
                                ln1_g[l], ln1_b[l], s5_d[l], b_router[l]),
            w_in=w_in[l].astype(_BF16),
            wg=jnp.concatenate([w_rg_a[l], w_rg_x[l]], axis=-1).astype(_BF16),
            w_rnn=w_rnn_out[l].astype(_BF16), w_glu=w_glu[l].astype(_BF16), w_out=w_out[l].astype(_BF16),
            w_r=jnp.pad(w_router[l], ((0, 0), (0, V7X_LANES - n_experts))).astype(_BF16),
            **_s5_params(s5_lambda_re[l], s5_lambda_im[l], s5_log_dt[l], s5_b_re[l], s5_b_im[l],
                         s5_c_re[l], s5_c_im[l]),
        )
        x1, h2w, dest_c, prob, cnt = _mixer(x, mod, p, alpha=alpha, n_experts=n_experts)

        r = EXPERT_ROWS
        blocks_per_region = tokens // r
        spare_blk = n_experts * blocks_per_region
        n_tiles = -(-(tokens * TOP_K + n_experts * (r - 1)) // r)
        counts = cnt[0, :n_experts].astype(jnp.int32)
        tiles_e = (counts + r - 1) // r
        tile_end = jnp.cumsum(tiles_e)
        n_used = tile_end[-1:]
        t_ids = jnp.arange(n_tiles, dtype=jnp.int32)
        done = (tile_end[None, :] <= t_ids[:, None]).astype(jnp.int32)
        tile_e = jnp.minimum(jnp.sum(done, axis=1), n_experts - 1)
        first_tile = jnp.sum(done * tiles_e[None, :], axis=1)
        used = t_ids < n_used
        tile_blk = jnp.where(used, tile_e * blocks_per_region + t_ids - first_tile, spare_blk)
        e_ids = jnp.arange(n_experts, dtype=jnp.int32)[None, :]
        own = (e_ids == tile_e[:, None]).astype(jnp.int32)
        valid = jnp.sum(own * counts[None, :], axis=1) - (t_ids - first_tile) * r
        step_rows = r // EXPERT_ROW_PATHS
        tile_rows = jnp.where(used, jnp.clip((valid + step_rows - 1) // step_rows * step_rows, step_rows, r), 0)
        has_tiles = (tiles_e > 0).astype(jnp.int32)[None, :]
        tile_first = (used & (t_ids == first_tile)).astype(jnp.int32)
        tile_slot = jnp.sum(done * has_tiles, axis=1) % 2
        later = jnp.where((has_tiles > 0) & (e_ids > tile_e[:, None]), e_ids, n_experts)
        tile_next = jnp.min(later, axis=1)
        tile_next = jnp.where(tile_next < n_experts, tile_next, -1)

        xb = _dispatch(h2w, dest_c, (spare_blk + 1) * r)
        yb = _experts(xb, (tile_e, tile_blk, tile_rows, tile_first, tile_slot, tile_next),
                      w_gu[l], b_gu[l], w_down[l], b_down[l])
        yg = _collect(yb, dest_c)
        x = _combine(x1, yg, prob, mod, ln2_g[l].reshape(1, -1), ln2_b[l].reshape(1, -1),
                     alpha=alpha, batch=batch)
    return x
```

```python
import functools

import jax
import jax.numpy as jnp
from jax import lax
from jax.experimental import pallas as pl
from jax.experimental.pallas import tpu as pltpu
from jax.experimental.pallas import tpu_sc as plsc

V7X_SUBLANES = 8
V7X_LANES = 128
V7X_VMEM_BYTES = 64 * 1024 * 1024

CONV_WIDTH = 4
LRU_C = 8.0
S5_GROUP = 16
S5_STATE = 64
TOP_K = 4
SWIGLU_LIMIT = 7.0
SWIGLU_ALPHA = 1.702
LN_EPS = 1e-5

S5_BLOCK_GROUPS = V7X_LANES // S5_GROUP
S5_BLOCK_STATES = S5_BLOCK_GROUPS * S5_STATE

(_VEC_B_X, _VEC_B_Y, _VEC_B_U5, _VEC_B_GA, _VEC_B_GB, _VEC_CONV_W) = range(6)
(_VEC_CONV_B, _VEC_B_RG_A, _VEC_B_RG_X, _VEC_LAMC, _VEC_LN_G, _VEC_LN_B, _VEC_S5_D, _VEC_B_ROUTER) = range(
    _VEC_CONV_W + CONV_WIDTH, _VEC_CONV_W + CONV_WIDTH + 8)

MIXER_STEPS = 64
EXPERT_ROWS = 512
EXPERT_ROW_PATHS = 4
COMBINE_ROWS = 1024
WEIGHT_CAST_ROWS = 64

_BF16 = jnp.bfloat16
_F32 = jnp.float32


def _dot(a, b):
    return jnp.dot(a, b, preferred_element_type=_F32)


def _sigmoid(v):
    return 0.5 * jnp.tanh(0.5 * v) + 0.5


def _vmem_limit(nbytes):
    return int(min(nbytes, V7X_VMEM_BYTES - 4 * 1024 * 1024))


def _layer_norm(z, gain, bias):
    mu = jnp.mean(z, axis=-1, keepdims=True)
    zc = z - mu
    var = jnp.mean(zc * zc, axis=-1, keepdims=True)
    return zc * lax.rsqrt(var + LN_EPS) * gain + bias


def _rows(v, steps):
    return jnp.tile(v, (steps, 1))


_HI_MASK = 0xFFFF0000


def _pack_rows(v):
    half = v.shape[1] // 2
    bits = lax.bitcast_convert_type(v.astype(_BF16).astype(_F32), jnp.uint32)
    packed = (bits[:, :half] >> 16) | (bits[:, half:] & jnp.uint32(_HI_MASK))
    return lax.bitcast_convert_type(packed, jnp.int32)


def _unpack_rows(w):
    bits = lax.bitcast_convert_type(w, jnp.uint32)
    lo = lax.bitcast_convert_type(bits << 16, _F32)
    hi = lax.bitcast_convert_type(bits & jnp.uint32(_HI_MASK), _F32)
    return lo, hi


def _ada_kernel(c_ref, w_ref, b_ref, o_ref):
    c = c_ref[...]
    c_act = (c * _sigmoid(c)).astype(_BF16)
    o_ref[...] = _dot(c_act, w_ref[...].astype(_BF16)) + b_ref[...]


def _ada(c, w_ada, b_ada):
    batch, d = c.shape
    n_out = w_ada.shape[1]
    return pl.pallas_call(
        _ada_kernel,
        grid=(n_out // d,),
        in_specs=[
            pl.BlockSpec((batch, d), lambda j: (0, 0)),
            pl.BlockSpec((d, d), lambda j: (0, j)),
            pl.BlockSpec((1, d), lambda j: (0, j)),
        ],
        out_specs=pl.BlockSpec((batch, d), lambda j: (0, j)),
        out_shape=jax.ShapeDtypeStruct((batch, n_out), _F32),
        name="ada",
    )(c, w_ada, b_ada.reshape(1, n_out))


def _mixer_kernel(alpha, steps, batch, d, n_s5_blocks, n_experts, region_rows,
                  x_ref, mod_ref, vecs_ref, w_in_ref, wg_ref, w_rnn_ref, s5a_ref, s5b_ref, s5c_ref,
                  w_glu_ref, w_out_ref, w_r_ref,
                  x1_ref, h2_ref, dest_ref, prob_ref, cnt_ref,
                  xt_s, xc_s, a_s, u_s, bu_s, u5_s, ya_s, ga_s, gb_s, h_state, s5_state, cnt_s):
    m = steps * batch
    halo = (CONV_WIDTH - 1) * batch
    s5w = n_s5_blocks * V7X_LANES
    n_blk = d // V7X_LANES
    bs = S5_BLOCK_STATES
    step = pl.program_id(0)

    @pl.when(step == 0)
    def _():
        xc_s[0:halo, :] = jnp.zeros((halo, d), _F32)
        h_state[...] = jnp.zeros_like(h_state)
        s5_state[...] = jnp.zeros_like(s5_state)
        cnt_s[...] = jnp.zeros_like(cnt_s)
        for ref in (xt_s, u_s, bu_s, u5_s, ya_s, ga_s, gb_s):
            ref[...] = jnp.zeros_like(ref)

    def mod(k):
        return mod_ref[:, k * d:(k + 1) * d]

    slot = lax.rem(step, 2)
    for b in range(batch):
        for j in range(n_blk):
            xt_s[slot, j, pl.ds(b, steps, stride=batch), :] = x_ref[b, :, j * V7X_LANES:(j + 1) * V7X_LANES]
    x = jnp.concatenate([xt_s[slot, j] for j in range(n_blk)], axis=1)
    hb = (x * _rows(1.0 + mod(1), steps) + _rows(mod(0), steps)).astype(_BF16)

    def vec(k, width=d):
        return vecs_ref[k:k + 1, 0:width]

    def in_proj(c0, width, bias_row):
        return _dot(hb, w_in_ref[:, c0:c0 + width]) + vec(bias_row, width)

    c0 = 2 * d
    c1 = c0 + s5w
    branch_a = _dot((ya_s[...] * u_s[...]).astype(_BF16), w_rnn_ref[...])
    y5 = jnp.concatenate(
        [_dot(bu_s[:, 2 * bs * j:2 * bs * (j + 1)].astype(_BF16), s5c_ref[j]) for j in range(n_s5_blocks)],
        axis=1) + vec(_VEC_S5_D, s5w) * u5_s[...]
    xc_s[halo:halo + m, :] = in_proj(0, d, _VEC_B_X)
    glu = _dot(jax.nn.gelu(y5).astype(_BF16), w_glu_ref[...])
    xr = jnp.zeros((m, d), _F32) + vec(_VEC_CONV_B)
    for k in range(CONV_WIDTH):
        xr = xr + vec(_VEC_CONV_W + k) * xc_s[k * batch:k * batch + m, :]
    xc_s[0:halo, :] = xc_s[m:m + halo, :]
    xrb = xr.astype(_BF16)
    gates = [_dot(xrb[:, j * V7X_LANES:(j + 1) * V7X_LANES], wg_ref[j]) for j in range(n_blk)]
    merged = (ga_s[...] * branch_a + gb_s[...] * (glu[:, :d] * _sigmoid(glu[:, d:]))).astype(_BF16)
    u5 = in_proj(c0, s5w, _VEC_B_U5)
    u5_s[...] = u5
    r_gate = _sigmoid(jnp.concatenate([g[:, :V7X_LANES] for g in gates], axis=1) + vec(_VEC_B_RG_A))
    i_gate = _sigmoid(jnp.concatenate([g[:, V7X_LANES:] for g in gates], axis=1) + vec(_VEC_B_RG_X))
    a = jnp.exp(vec(_VEC_LAMC) * r_gate)
    a_s[...] = a
    z = 1.0 - a * a
    u_s[...] = jnp.where(z > 0.0, z * lax.rsqrt(z), 0.0) * (i_gate * xr)
    mix = _dot(merged, w_out_ref[...])
    u5b = u5.astype(_BF16)
    for j in range(n_s5_blocks):
        bu_s[:, 2 * bs * j:2 * bs * (j + 1)] = _dot(u5b[:, j * V7X_LANES:(j + 1) * V7X_LANES], s5b_ref[j])
    hc = h_state[...]
    s5c = [(s5_state[:, 2 * bs * j:2 * bs * j + bs], s5_state[:, 2 * bs * j + bs:2 * bs * (j + 1)])
           for j in range(n_s5_blocks)]
    for t in range(steps):
        r0 = t * batch
        hc = a_s[r0:r0 + batch, :] * hc + u_s[r0:r0 + batch, :]
        u_s[r0:r0 + batch, :] = hc
        for j in range(n_s5_blocks):
            re0, im0 = 2 * bs * j, 2 * bs * j + bs
            re, im = s5c[j]
            ar, ai = s5a_ref[0, j], s5a_ref[1, j]
            nre = ar * re - ai * im + bu_s[r0:r0 + batch, re0:re0 + bs]
            nim = ar * im + ai * re + bu_s[r0:r0 + batch, im0:im0 + bs]
            bu_s[r0:r0 + batch, re0:re0 + bs] = nre
            bu_s[r0:r0 + batch, im0:im0 + bs] = nim
            s5c[j] = (nre, nim)
    h_state[...] = hc
    for j in range(n_s5_blocks):
        s5_state[:, 2 * bs * j:2 * bs * j + bs] = s5c[j][0]
        s5_state[:, 2 * bs * j + bs:2 * bs * (j + 1)] = s5c[j][1]
    x_prev = jnp.concatenate([xt_s[1 - slot, j] for j in range(n_blk)], axis=1)
    x1 = _layer_norm(alpha * x_prev + _rows(1.0 + mod(2), steps) * mix, vec(_VEC_LN_G), vec(_VEC_LN_B))
    x1_ref[...] = x1
    h2 = x1 * _rows(1.0 + mod(4), steps) + _rows(mod(3), steps)
    h2b = h2.astype(_BF16)
    h2_ref[...] = _pack_rows(h2)
    ya_s[...] = jax.nn.gelu(in_proj(d, d, _VEC_B_Y))
    lane = lax.broadcasted_iota(jnp.int32, (m, V7X_LANES), 1)
    lane_f = lane.astype(_F32)
    neg_inf = jnp.float32(-jnp.inf)
    logits = jnp.where(lane < n_experts, _dot(h2b, w_r_ref[...]) + vec(_VEC_B_ROUTER, V7X_LANES), neg_inf)
    ga_s[...] = _sigmoid(in_proj(c1, d, _VEC_B_GA))
    gb_s[...] = _sigmoid(in_proj(c1 + d, d, _VEC_B_GB))
    onehot = jnp.zeros((m, V7X_LANES), _F32)
    picks, vals = [], []
    for _ in range(TOP_K):
        v = jnp.max(logits, axis=-1, keepdims=True)
        p = jnp.min(jnp.where(logits == v, lane_f, float(V7X_LANES)), axis=-1, keepdims=True)
        hit = lane_f == p
        onehot = jnp.where(hit, 1.0, onehot)
        logits = jnp.where(hit, neg_inf, logits)
        picks.append(p)
        vals.append(v)
    exps = [jnp.exp(v - vals[0]) for v in vals]
    inv_den = 1.0 / functools.reduce(lambda s, e: s + e, exps)
    row = lax.broadcasted_iota(jnp.int32, (m, m), 0)
    col = lax.broadcasted_iota(jnp.int32, (m, m), 1)
    earlier = jnp.where(col < row, 1.0, 0.0).astype(_BF16)
    before = _dot(earlier, onehot.astype(_BF16)) + cnt_s[0:1, :]
    prob_out = jnp.zeros((m, V7X_LANES), _F32)
    dest_out = jnp.zeros((m, V7X_LANES), _F32)
    for k in range(TOP_K):
        rank_k = jnp.sum(jnp.where(lane_f == picks[k], before, 0.0), axis=-1, keepdims=True)
        prob_out = jnp.where(lane == k, exps[k] * inv_den, prob_out)
        dest_out = jnp.where(lane == k, picks[k] * float(region_rows) + rank_k, dest_out)
    prob_ref[...] = prob_out
    dest_t = dest_out.T[0:V7X_SUBLANES, :].astype(jnp.int32)
    for j in range(m // V7X_LANES):
        dest_ref[j] = dest_t[:, j * V7X_LANES:(j + 1) * V7X_LANES]
    has_prev = jnp.where(step > 0, 1.0, 0.0)
    cnt_new = cnt_s[...] + has_prev * jnp.sum(onehot, axis=0, keepdims=True)
    cnt_s[...] = cnt_new
    cnt_ref[...] = cnt_new


def _mixer(x, mod, p, *, alpha, n_experts):
    batch, seq, d = x.shape
    tokens = batch * seq
    steps = MIXER_STEPS
    m = steps * batch
    n_chunks = seq // steps
    n_s5_blocks = p["s5b"].shape[0]
    s5_lanes = n_s5_blocks * 2 * S5_BLOCK_STATES
    halo = (CONV_WIDTH - 1) * batch

    def const(a):
        nd = a.ndim
        return pl.BlockSpec(a.shape, lambda i, nd=nd: (0,) * nd, pipeline_mode=pl.Buffered(1))

    weights = [p["vecs"], p["w_in"], p["wg"], p["w_rnn"], p["s5a"], p["s5b"], p["s5c"], p["w_glu"], p["w_out"],
               p["w_r"]]
    prev = lambda i: jnp.maximum(i - 1, 0)
    row_spec = lambda width: pl.BlockSpec((m, width), lambda i: (prev(i), 0))
    chunks = m // V7X_LANES
    out_shape = (
        jax.ShapeDtypeStruct((tokens, d), _F32),
        jax.ShapeDtypeStruct((tokens, d // 2), jnp.int32),
        jax.ShapeDtypeStruct((tokens // V7X_LANES, V7X_SUBLANES, V7X_LANES), jnp.int32),
        jax.ShapeDtypeStruct((tokens, V7X_LANES), _F32),
        jax.ShapeDtypeStruct((V7X_SUBLANES, V7X_LANES), _F32),
    )
    act = pltpu.VMEM((m, d), _F32)
    scratch = [
        pltpu.VMEM((2, d // V7X_LANES, m, V7X_LANES), _F32),
        pltpu.VMEM((m + halo, d), _F32),
        act, act,
        pltpu.VMEM((m, s5_lanes), _F32),
        pltpu.VMEM((m, n_s5_blocks * V7X_LANES), _F32),
        act, act, act,
        pltpu.VMEM((batch, d), _F32),
        pltpu.VMEM((batch, s5_lanes), _F32),
        pltpu.VMEM((V7X_SUBLANES, V7X_LANES), _F32),
    ]
    weight_bytes = sum(w.size * w.dtype.itemsize for w in weights)
    act_bytes = m * d * 4
    vmem = weight_bytes + 32 * act_bytes
    kern = functools.partial(_mixer_kernel, alpha, steps, batch, d, n_s5_blocks, n_experts, tokens)
    return pl.pallas_call(
        kern,
        grid=(n_chunks + 1,),
        in_specs=[pl.BlockSpec((batch, steps, d), lambda i: (0, jnp.minimum(i, n_chunks - 1), 0)),
                  const(mod)] + [const(w) for w in weights],
        out_specs=(row_spec(d), row_spec(d // 2),
                   pl.BlockSpec((chunks, V7X_SUBLANES, V7X_LANES), lambda i: (prev(i), 0, 0)),
                   row_spec(V7X_LANES),
                   pl.BlockSpec((V7X_SUBLANES, V7X_LANES), lambda i: (0, 0))),
        out_shape=out_shape,
        scratch_shapes=scratch,
        compiler_params=pltpu.CompilerParams(dimension_semantics=("arbitrary",),
                                             vmem_limit_bytes=_vmem_limit(vmem)),
        name="mixer",
    )(x, mod, *weights)


def _expert_kernel(d_ff, tile_e_ref, tile_blk_ref, tile_rows_ref, tile_first_ref, tile_slot_ref, tile_next_ref,
                   x_ref, wgu_hbm, bgu_ref, wd_hbm, bd_ref, y_ref, wgu_f, wd_f, wgu_s, wd_s, sems):
    i = pl.program_id(0)
    d = wgu_f.shape[1]
    r = x_ref.shape[0]

    def weight_copies(expert, slot):
        return (pltpu.make_async_copy(wgu_hbm.at[expert], wgu_f.at[slot], sems.at[slot, 0]),
                pltpu.make_async_copy(wd_hbm.at[expert], wd_f.at[slot], sems.at[slot, 1]))

    @pl.when(tile_first_ref[i] == 1)
    def _():
        slot = tile_slot_ref[i]

        @pl.when(i == 0)
        def _():
            for cp in weight_copies(tile_e_ref[i], slot):
                cp.start()

        for cp in weight_copies(tile_e_ref[i], slot):
            cp.wait()

        @pl.when(tile_next_ref[i] >= 0)
        def _():
            for cp in weight_copies(tile_next_ref[i], 1 - slot):
                cp.start(priority=1)

        def cast(c, carry):
            r0 = pl.multiple_of(c * WEIGHT_CAST_ROWS, WEIGHT_CAST_ROWS)
            wgu_s[pl.ds(r0, WEIGHT_CAST_ROWS), :] = wgu_f[slot, pl.ds(r0, WEIGHT_CAST_ROWS), :].astype(_BF16)
            wd_s[pl.ds(r0, WEIGHT_CAST_ROWS), :] = wd_f[slot, pl.ds(r0, WEIGHT_CAST_ROWS), :].astype(_BF16)
            return carry

        lax.fori_loop(0, d // WEIGHT_CAST_ROWS, cast, 0)

    def mlp(rows):
        lo, hi = _unpack_rows(x_ref[0:rows, :])
        x = jnp.concatenate([lo, hi], axis=1).astype(_BF16)
        gu = _dot(x, wgu_s[...]) + bgu_ref[0]
        gate = jnp.minimum(gu[:, :d_ff], SWIGLU_LIMIT)
        up = jnp.clip(gu[:, d_ff:], -SWIGLU_LIMIT, SWIGLU_LIMIT)
        act = gate * _sigmoid(SWIGLU_ALPHA * gate) * (up + 1.0)
        y_ref[0:rows, :] = _pack_rows(_dot(act.astype(_BF16), wd_s[...]) + bd_ref[0])
        if rows < r:
            y_ref[rows:r, :] = jnp.zeros((r - rows, y_ref.shape[1]), y_ref.dtype)

    for rows in range(r // EXPERT_ROW_PATHS, r + 1, r // EXPERT_ROW_PATHS):
        pl.when(tile_rows_ref[i] == rows)(functools.partial(mlp, rows))

    @pl.when(tile_rows_ref[i] == 0)
    def _():
        y_ref[...] = jnp.zeros_like(y_ref)


def _experts(xb, schedule, w_gu, b_gu, w_down, b_down):
    n_rows, half = xb.shape
    n_experts, d, two_ff = w_gu.shape
    d_ff = two_ff // 2
    assert d_ff == d, "the weight cast loop walks w_gu and w_down rows together"
    r = EXPERT_ROWS
    vmem = 2 * (d * two_ff + d_ff * d) * 4 + (d * two_ff + d_ff * d) * 2 + 8 * r * half * 4 + 6 * r * two_ff * 4
    n_sched = len(schedule)
    tile = lambda i, *s: (s[1][i], 0)
    expert = lambda i, *s: (s[0][i], 0, 0)
    grid_spec = pltpu.PrefetchScalarGridSpec(
        num_scalar_prefetch=n_sched,
        grid=(schedule[0].shape[0],),
        in_specs=[
            pl.BlockSpec((r, half), tile),
            pl.BlockSpec(memory_space=pl.ANY),
            pl.BlockSpec((1, 1, two_ff), expert),
            pl.BlockSpec(memory_space=pl.ANY),
            pl.BlockSpec((1, 1, d), expert),
        ],
        out_specs=pl.BlockSpec((r, half), tile),
        scratch_shapes=[pltpu.VMEM((2, d, two_ff), _F32), pltpu.VMEM((2, d_ff, d), _F32),
                        pltpu.VMEM((d, two_ff), _BF16), pltpu.VMEM((d_ff, d), _BF16),
                        pltpu.SemaphoreType.DMA((2, 2))],
    )
    return pl.pallas_call(
        functools.partial(_expert_kernel, d_ff),
        grid_spec=grid_spec,
        out_shape=jax.ShapeDtypeStruct((n_rows, half), jnp.int32),
        compiler_params=pltpu.CompilerParams(dimension_semantics=("arbitrary",),
                                             vmem_limit_bytes=_vmem_limit(vmem)),
        name="experts",
    )(*schedule, xb, w_gu, b_gu.reshape(n_experts, 1, two_ff), w_down, b_down.reshape(n_experts, 1, d))


def _sc_workers():
    info = plsc.get_sparse_core_info()
    return info.num_cores, info.num_subcores


def _dispatch(h2w, dest_c, n_rows):
    tokens, width = h2w.shape
    n_chunks, _, chunk = dest_c.shape
    nc, ns = _sc_workers()
    per_w = n_chunks // (nc * ns)
    assert per_w * nc * ns == n_chunks

    @functools.partial(
        pl.kernel, mesh=plsc.VectorSubcoreMesh(core_axis_name="c", subcore_axis_name="s"),
        out_type=jax.ShapeDtypeStruct((n_rows, width), h2w.dtype),
        scratch_types=[pltpu.VMEM(dest_c.shape[1:], jnp.int32), pltpu.VMEM((chunk, width), h2w.dtype)],
    )
    def scatter_rows(h_hbm, d_hbm, o_hbm, idx_v, rows_v):
        wid = lax.axis_index("s") * nc + lax.axis_index("c")

        @pl.loop(0, per_w)
        def _(j):
            blk = wid * per_w + j
            pltpu.sync_copy(d_hbm.at[blk], idx_v)
            pltpu.sync_copy(h_hbm.at[pl.ds(pl.multiple_of(blk * chunk, chunk), chunk)], rows_v)
            for k in range(TOP_K):
                pltpu.sync_copy(rows_v, o_hbm.at[idx_v.at[k]])

    return scatter_rows(h2w, dest_c)


def _collect(yb, dest_c):
    _, width = yb.shape
    n_chunks, _, chunk = dest_c.shape
    nc, ns = _sc_workers()
    per_w = n_chunks // (nc * ns)
    assert per_w * nc * ns == n_chunks

    @functools.partial(
        pl.kernel, mesh=plsc.VectorSubcoreMesh(core_axis_name="c", subcore_axis_name="s"),
        out_type=jax.ShapeDtypeStruct((TOP_K, n_chunks * chunk, width), yb.dtype),
        scratch_types=[pltpu.VMEM((per_w,) + dest_c.shape[1:], jnp.int32), pltpu.VMEM((chunk, width), yb.dtype)],
    )
    def gather_rows(y_hbm, d_hbm, o_hbm, idx_v, rows_v):
        wid = lax.axis_index("s") * nc + lax.axis_index("c")
        pltpu.sync_copy(d_hbm.at[pl.ds(wid * per_w, per_w)], idx_v)

        @pl.loop(0, per_w)
        def _(j):
            blk = wid * per_w + j
            for k in range(TOP_K):
                pltpu.sync_copy(y_hbm.at[idx_v.at[j, k]], rows_v)
                pltpu.sync_copy(rows_v, o_hbm.at[k, pl.ds(pl.multiple_of(blk * chunk, chunk), chunk)])

    return gather_rows(yb, dest_c)


def _combine_kernel(alpha, steps, x1_ref, yg_ref, prob_ref, mod_ref, ln_g_ref, ln_b_ref, o_ref, ot_s):
    d = x1_ref.shape[1]
    batch = o_ref.shape[0]
    ffn_lo = jnp.zeros((x1_ref.shape[0], d // 2), _F32)
    ffn_hi = jnp.zeros((x1_ref.shape[0], d // 2), _F32)
    for k in range(TOP_K):
        lo, hi = _unpack_rows(yg_ref[k])
        ffn_lo = ffn_lo + prob_ref[:, k:k + 1] * lo
        ffn_hi = ffn_hi + prob_ref[:, k:k + 1] * hi
    ffn = jnp.concatenate([ffn_lo, ffn_hi], axis=1)
    gate = _rows(1.0 + mod_ref[:, 5 * d:6 * d], steps)
    out = _layer_norm(alpha * x1_ref[...] + gate * ffn, ln_g_ref[...], ln_b_ref[...])
    n_blk = d // V7X_LANES
    for j in range(n_blk):
        ot_s[j] = out[:, j * V7X_LANES:(j + 1) * V7X_LANES]
    for b in range(batch):
        for j in range(n_blk):
            o_ref[b, :, j * V7X_LANES:(j + 1) * V7X_LANES] = ot_s[j, pl.ds(b, steps, stride=batch), :]


def _combine(x1, yg, prob, mod, ln_g, ln_b, *, alpha, batch):
    tokens, d = x1.shape
    rows = COMBINE_ROWS
    steps = rows // batch
    const = lambda a: pl.BlockSpec(a.shape, lambda i: (0, 0))
    return pl.pallas_call(
        functools.partial(_combine_kernel, alpha, steps),
        grid=(tokens // rows,),
        in_specs=[
            pl.BlockSpec((rows, d), lambda i: (i, 0)),
            pl.BlockSpec((TOP_K, rows, d // 2), lambda i: (0, i, 0)),
            pl.BlockSpec((rows, V7X_LANES), lambda i: (i, 0)),
            const(mod), const(ln_g), const(ln_b),
        ],
        out_specs=pl.BlockSpec((batch, steps, d), lambda i: (0, i, 0)),
        out_shape=jax.ShapeDtypeStruct((batch, tokens // batch, d), _F32),
        scratch_shapes=[pltpu.VMEM((d // V7X_LANES, rows, V7X_LANES), _F32)],
        compiler_params=pltpu.CompilerParams(dimension_semantics=("parallel",)),
        name="combine",
    )(x1, yg, prob, mod, ln_g, ln_b)


def _block_diag(blocks):
    nb, n, a, b = blocks.shape
    eye = jnp.eye(n, dtype=blocks.dtype)
    return (eye[None, :, None, :, None] * blocks[:, :, :, None, :]).reshape(nb, n * a, n * b)


def _s5_params(lam_re, lam_im, log_dt, b_re, b_im, c_re, c_im):
    groups = lam_re.shape[0]
    nb = groups // S5_BLOCK_GROUPS
    dt = jnp.exp(log_dt)[:, None]
    mag = jnp.exp(lam_re * dt)
    ab_re, ab_im = mag * jnp.cos(lam_im * dt), mag * jnp.sin(lam_im * dt)
    den = lam_re * lam_re + lam_im * lam_im
    q_re = ((ab_re - 1.0) * lam_re + ab_im * lam_im) / den
    q_im = (ab_im * lam_re - (ab_re - 1.0) * lam_im) / den
    bb_re = q_re[..., None] * b_re - q_im[..., None] * b_im
    bb_im = q_re[..., None] * b_im + q_im[..., None] * b_re

    def per_block(a):
        return jnp.swapaxes(a.reshape(nb, S5_BLOCK_GROUPS, *a.shape[1:]), 2, 3)

    bmat = jnp.concatenate([_block_diag(per_block(bb_re)), _block_diag(per_block(bb_im))], axis=2)
    cmat = jnp.concatenate([_block_diag(per_block(c_re)), -_block_diag(per_block(c_im))], axis=1)
    s5a = jnp.broadcast_to(jnp.stack([ab_re, ab_im]).reshape(2, nb, 1, S5_BLOCK_STATES),
                           (2, nb, V7X_SUBLANES, S5_BLOCK_STATES))
    return dict(s5a=s5a, s5b=bmat.astype(_BF16), s5c=cmat.astype(_BF16))


def _mixer_vectors(d, b_in, conv_w, conv_b, b_rg_a, b_rg_x, lru_lambda, ln_g, ln_b, s5_d, b_router):
    pad = lambda v: jnp.pad(v, (0, d - v.shape[0]))
    s5w = s5_d.size
    rows = [b_in[0:d], b_in[d:2 * d], pad(b_in[2 * d:2 * d + s5w]), b_in[2 * d + s5w:3 * d + s5w],
            b_in[3 * d + s5w:4 * d + s5w], *conv_w, conv_b, b_rg_a, b_rg_x,
            -LRU_C * jax.nn.softplus(-lru_lambda), ln_g, ln_b, pad(s5_d.reshape(-1)), pad(b_router)]
    rows += [jnp.zeros((d,), _F32)] * (-len(rows) % V7X_SUBLANES)
    return jnp.stack(rows)


def kernel(x, c, w_ada, b_ada, w_in, b_in, conv_w, conv_b, w_rg_a, b_rg_a, w_rg_x, b_rg_x, lru_lambda, w_rnn_out, s5_lambda_re, s5_lambda_im, s5_log_dt, s5_b_re, s5_b_im, s5_c_re, s5_c_im, s5_d, w_glu, w_out, ln1_g, ln1_b, w_router, b_router, w_gu, b_gu, w_down, b_down, ln2_g, ln2_b):
    batch, seq, d = x.shape
    depth = w_ada.shape[0]
    n_experts = w_router.shape[-1]
    tokens = batch * seq
    alpha = (2.0 * depth) ** 0.25
    assert batch == V7X_SUBLANES and d % V7X_LANES == 0 and n_experts <= V7X_LANES
    assert seq % MIXER_STEPS == 0 and tokens % COMBINE_ROWS == 0 and tokens % EXPERT_ROWS == 0
    assert (MIXER_STEPS * batch) % V7X_LANES == 0

    for l in range(depth):
        mod = _ada(c, w_ada[l], b_ada[l])
        p = dict(
            vecs=_mixer_vectors(d, b_in[l], conv_w[l], conv_b[l], b_rg_a[l], b_rg_x[l], lru_lambda[l],
                                ln1_g[l], ln1_b[l], s5_d[l], b_router[l]),
            w_in=w_in[l].astype(_BF16),
            wg=jnp.concatenate([w_rg_a[l], w_rg_x[l]], axis=-1).astype(_BF16),
            w_rnn=w_rnn_out[l].astype(_BF16), w_glu=w_glu[l].astype(_BF16), w_out=w_out[l].astype(_BF16),
            w_r=jnp.pad(w_router[l], ((0, 0), (0, V7X_LANES - n_experts))).astype(_BF16),
            **_s5_params(s5_lambda_re[l], s5_lambda_im[l], s5_log_dt[l], s5_b_re[l], s5_b_im[l],
                         s5_c_re[l], s5_c_im[l]),
        )
        x1, h2w, dest_c, prob, cnt = _mixer(x, mod, p, alpha=alpha, n_experts=n_experts)

        r = EXPERT_ROWS
        blocks_per_region = tokens // r
        spare_blk = n_experts * blocks_per_region
        n_tiles = -(-(tokens * TOP_K + n_experts * (r - 1)) // r)
        counts = cnt[0, :n_experts].astype(jnp.int32)
        tiles_e = (counts + r - 1) // r
        tile_end = jnp.cumsum(tiles_e)
        n_used = tile_end[-1:]
        t_ids = jnp.arange(n_tiles, dtype=jnp.int32)
        done = (tile_end[None, :] <= t_ids[:, None]).astype(jnp.int32)
        tile_e = jnp.minimum(jnp.sum(done, axis=1), n_experts - 1)
        first_tile = jnp.sum(done * tiles_e[None, :], axis=1)
        used = t_ids < n_used
        tile_blk = jnp.where(used, tile_e * blocks_per_region + t_ids - first_tile, spare_blk)
        e_ids = jnp.arange(n_experts, dtype=jnp.int32)[None, :]
        own = (e_ids == tile_e[:, None]).astype(jnp.int32)
        valid = jnp.sum(own * counts[None, :], axis=1) - (t_ids - first_tile) * r
        step_rows = r // EXPERT_ROW_PATHS
        tile_rows = jnp.where(used, jnp.clip((valid + step_rows - 1) // step_rows * step_rows, step_rows, r), 0)
        has_tiles = (tiles_e > 0).astype(jnp.int32)[None, :]
        tile_first = (used & (t_ids == first_tile)).astype(jnp.int32)
        tile_slot = jnp.sum(done * has_tiles, axis=1) % 2
        later = jnp.where((has_tiles > 0) & (e_ids > tile_e[:, None]), e_ids, n_experts)
        tile_next = jnp.min(later, axis=1)
        tile_next = jnp.where(tile_next < n_experts, tile_next, -1)

        xb = _dispatch(h2w, dest_c, (spare_blk + 1) * r)
        yb = _experts(xb, (tile_e, tile_blk, tile_rows, tile_first, tile_slot, tile_next),
                      w_gu[l], b_gu[l], w_down[l], b_down[l])
        yg = _collect(yb, dest_c)
        x = _combine(x1, yg, prob, mod, ln2_g[l].reshape(1, -1), ln2_b[l].reshape(1, -1),
                     alpha=alpha, batch=batch)
    return x
```
